```python
import math
import jax, jax.numpy as jnp
from jax import lax
import numpy as np

D_MODEL = 1024
BATCH = 8
SEQ = 2048
DEPTH = 2
DEC_BATCH = 128
DEC_SEQ = 4
PAST_LEN = 16384
PAGE_SIZE = 128

MLSTM_WIDTH = D_MODEL // 2
MLSTM_HEADS = 4
MLSTM_DH = MLSTM_WIDTH // MLSTM_HEADS
MLSTM_CHUNK = 64
SGU_WIDTH = D_MODEL // 4
SGU_HEADS = 4
SGU_DH = SGU_WIDTH // SGU_HEADS
SGU_CHUNK = 128
POOL_WIDTH = D_MODEL - MLSTM_WIDTH - SGU_WIDTH
POOL_WINDOWS = (2, 4, 8, 16)
POOL_GROUPS = len(POOL_WINDOWS)
POOL_DG = POOL_WIDTH // POOL_GROUPS
POOL_BUF = max(POOL_WINDOWS) - 1
PEER_HEADS = 8
PEER_NKEYS = 128
PEER_EXPERTS = PEER_NKEYS ** 2
PEER_TOPK = 16
PEER_DQ = 256
PEER_DK = PEER_DQ // 2
PEER_BLOCK = 128
ALPHA = (2 * DEPTH) ** 0.25
BETA = (8 * DEPTH) ** -0.25
LN_EPS = 1e-5
IN_COLS = 4 * MLSTM_WIDTH + 2 * MLSTM_HEADS + 2 * SGU_WIDTH + POOL_WIDTH
IN_SPLITS = [int(s) for s in np.cumsum([MLSTM_WIDTH] * 4 + [2 * MLSTM_HEADS] + [SGU_WIDTH] * 2 + [POOL_WIDTH])[:-1]]

kernel_name = 'hymba_mlstm_sgu_pool_peer_deepnorm_step'


def layer_norm(x, g, b, eps=LN_EPS):
    xf = x.astype(jnp.float32)
    mu = xf.mean(-1, keepdims=True)
    var = jnp.mean(jnp.square(xf - mu), -1, keepdims=True)
    return ((xf - mu) * lax.rsqrt(var + eps) * g + b).astype(x.dtype)


def mlstm_chunk(state, q, k, v, ig, lf):
    C, n, m = state
    L = q.shape[1]
    b = jnp.cumsum(lf, axis=1)
    causal = jnp.tril(jnp.ones((L, L), bool))
    dlog = b[:, :, None, :] - b[:, None, :, :] + ig[:, None, :, :]
    dlog = jnp.where(causal[None, :, :, None], dlog, -jnp.inf)
    inter = b + m[:, None, :]
    m_t = jnp.maximum(inter, dlog.max(axis=2))
    w_intra = jnp.exp(dlog - m_t[:, :, None, :])
    w_inter = jnp.exp(inter - m_t)
    a = w_intra * jnp.einsum('bthd,bshd->btsh', q, k)
    num = jnp.einsum('btsh,bshd->bthd', a, v) + w_inter[..., None] * jnp.einsum('bhvk,bthk->bthv', C, q)
    den = a.sum(2) + w_inter * jnp.einsum('bhk,bthk->bth', n, q)
    h = num / jnp.maximum(jnp.abs(den), jnp.exp(-m_t))[..., None]
    b_end = b[:, -1]
    dend = b_end[:, None, :] - b + ig
    m_new = jnp.maximum(b_end + m, dend.max(1))
    wc = jnp.exp(dend - m_new[:, None, :])
    dec = jnp.exp(b_end + m - m_new)
    C_new = dec[..., None, None] * C + jnp.einsum('bsh,bshv,bshk->bhvk', wc, v, k)
    n_new = dec[..., None] * n + jnp.einsum('bsh,bshk->bhk', wc, k)
    return (C_new, n_new, m_new), h


def mlstm_scan(state, q, k, v, ig, lf):
    B, T, H, Dh = q.shape
    lc = MLSTM_CHUNK if T % MLSTM_CHUNK == 0 else T
    nc = T // lc

    def to_chunks(a):
        return jnp.moveaxis(a.reshape((B, nc, lc) + a.shape[2:]), 1, 0)

    def step(carry, xs):
        return mlstm_chunk(carry, *xs)

    state, h = lax.scan(step, state, tuple(to_chunks(a) for a in (q, k, v, ig, lf)))
    return state, jnp.moveaxis(h, 0, 1).reshape(B, T, H, Dh)


def pool_mix(p_in, buf, pos0, w_pool, pool_scale):
    B, T, C = p_in.shape
    P = buf.shape[1]
    wmax = max(POOL_WINDOWS)
    xcat = jnp.concatenate([buf.astype(p_in.dtype), p_in], axis=1)
    xpad = jnp.concatenate([jnp.zeros((B, wmax, C), jnp.float32), xcat.astype(jnp.float32)], axis=1)
    cs = jnp.cumsum(xpad, axis=1)
    pos = pos0 + jnp.arange(T)
    start = wmax + P
    outs = []
    for gi, w in enumerate(POOL_WINDOWS):
        sl = slice(gi * POOL_DG, (gi + 1) * POOL_DG)
        wsum = cs[:, start:start + T, sl] - cs[:, start - w:start - w + T, sl]
        cnt = jnp.minimum(pos + 1, w).astype(jnp.float32)[None, :, None]
        outs.append(wsum / cnt - xpad[:, start:start + T, sl])
    pooled = jnp.stack(outs, axis=2)
    y = jnp.einsum('btgd,gde->btge', pooled, w_pool.astype(jnp.float32)).reshape(B, T, C) * pool_scale
    return y.astype(p_in.dtype), xcat[:, -POOL_BUF:]


def token_mixers(xm, mstate, pool_buf, pos0, w_in, b_gate, mh_g, sgu_g, sgu_b, w_s, b_s, w_pool, pool_scale, w_o):
    B, T, _ = xm.shape
    f32 = jnp.float32
    proj = jnp.einsum('btd,de->bte', xm, w_in)
    q, k, v, o, gates, u_s, v_s, p_in = jnp.split(proj, IN_SPLITS, axis=-1)
    hd = lambda a: a.reshape(B, T, MLSTM_HEADS, MLSTM_DH).astype(f32)
    gates = gates.astype(f32) + b_gate
    ig = gates[..., :MLSTM_HEADS]
    lf = jax.nn.log_sigmoid(gates[..., MLSTM_HEADS:])
    mstate, h = mlstm_scan(mstate, hd(q) * MLSTM_DH ** -0.5, hd(k), hd(v), ig, lf)
    h = layer_norm(h, mh_g.reshape(MLSTM_HEADS, MLSTM_DH), 0.0).reshape(B, T, MLSTM_WIDTH)
    y_a = (jax.nn.sigmoid(o.astype(f32)) * h).astype(xm.dtype)
    lc = min(T, SGU_CHUNK)
    vn = layer_norm(v_s.reshape(B, T, SGU_HEADS, SGU_DH), sgu_g.reshape(SGU_HEADS, SGU_DH), sgu_b.reshape(SGU_HEADS, SGU_DH))
    ws = jnp.where(jnp.tril(jnp.ones((lc, lc), bool)), w_s[:, :lc, :lc], 0.0)
    vc = vn.reshape(B, T // lc, lc, SGU_HEADS, SGU_DH)
    mix = jnp.einsum('gts,bcsgd->bctgd', ws, vc) + jnp.swapaxes(b_s[:, :lc], 0, 1)[:, :, None]
    y_b = u_s * mix.reshape(B, T, SGU_WIDTH).astype(xm.dtype)
    y_c, pool_buf = pool_mix(p_in, pool_buf, pos0, w_pool, pool_scale)
    y = jnp.einsum('bte,ed->btd', jnp.concatenate([y_a, y_b, y_c], axis=-1), w_o)
    return y, mstate, pool_buf, vn.reshape(B, T, SGU_WIDTH)


def peer(xm, w_pq, peer_keys, peer_u, peer_v):
    B, T, D = xm.shape
    n = B * T
    x2 = xm.reshape(n, D)
    q = jnp.einsum('nd,de->ne', x2, w_pq).reshape(n, PEER_HEADS, 2, PEER_DK)
    s = jnp.einsum('nhpd,hpkd->nhpk', q, peer_keys).astype(jnp.float32)
    s1, i1 = lax.top_k(s[:, :, 0], PEER_TOPK)
    s2, i2 = lax.top_k(s[:, :, 1], PEER_TOPK)
    cand = (s1[..., :, None] + s2[..., None, :]).reshape(n, PEER_HEADS, PEER_TOPK ** 2)
    cidx = (i1[..., :, None] * PEER_NKEYS + i2[..., None, :]).reshape(n, PEER_HEADS, PEER_TOPK ** 2)
    top_s, top_pos = lax.top_k(cand, PEER_TOPK)
    idx = jnp.take_along_axis(cidx, top_pos, axis=-1)
    g = jax.nn.softmax(top_s, axis=-1)
    pad = (-n) % PEER_BLOCK
    nb = (n + pad) // PEER_BLOCK
    xb = jnp.pad(x2, ((0, pad), (0, 0))).reshape(nb, PEER_BLOCK, D)
    ib = jnp.pad(idx, ((0, pad), (0, 0), (0, 0))).reshape(nb, PEER_BLOCK, PEER_HEADS, PEER_TOPK)
    gb = jnp.pad(g, ((0, pad), (0, 0), (0, 0))).reshape(nb, PEER_BLOCK, PEER_HEADS, PEER_TOPK)

    def expert_block(args):
        xk, ik, gk = args
        act = jax.nn.gelu(jnp.einsum('nd,nhkd->nhk', xk, peer_u[ik]).astype(jnp.float32), approximate=False)
        coef = (gk * act).astype(xk.dtype)
        return jnp.einsum('nhk,nhkd->nd', coef, peer_v[ik])

    out = lax.map(expert_block, (xb, ib, gb)).reshape(nb * PEER_BLOCK, D)[:n]
    return out.reshape(B, T, D)


def trunk_layer(x, c, mstate, pool_buf, pos0, w_ada, b_ada, w_in, b_gate, mh_g, sgu_g, sgu_b, w_s, b_s,
                w_pool, pool_scale, w_o, ln1_g, ln1_b, w_pq, peer_keys, peer_u, peer_v, ln2_g, ln2_b):
    ada = jnp.einsum('bd,de->be', jax.nn.silu(c), w_ada) + b_ada
    sh1, sc1, g1, sh2, sc2, g2 = jnp.split(ada[:, None, :], 6, axis=-1)
    y, mstate, pool_buf, v_rows = token_mixers(x * (1 + sc1) + sh1, mstate, pool_buf, pos0, w_in, b_gate, mh_g,
                                               sgu_g, sgu_b, w_s, b_s, w_pool, pool_scale, w_o)
    x = layer_norm(ALPHA * x + g1 * y, ln1_g, ln1_b)
    y = peer(x * (1 + sc2) + sh2, w_pq, peer_keys, peer_u, peer_v)
    x = layer_norm(ALPHA * x + g2 * y, ln2_g, ln2_b)
    return x, mstate, pool_buf, v_rows


def setup_inputs(seed: int = 0) -> dict:
    key = jax.random.key(seed)
    ks = iter(jax.random.split(key, 32))
    f32 = jnp.float32
    nrm = lambda shape, s: jax.random.normal(next(ks), shape, f32) * s
    D = D_MODEL
    H = MLSTM_HEADS
    fbias = jnp.broadcast_to(jnp.linspace(3.0, 6.0, H), (DEPTH, H))
    b_gate = jnp.concatenate([jnp.zeros((DEPTH, H), f32), fbias], axis=-1)
    return {
        'x_prompt': nrm((BATCH, SEQ, D), 1.0),
        'x_sample': nrm((DEC_BATCH, DEC_SEQ, D), 1.0),
        'state_mlstm_C': nrm((DEPTH, DEC_BATCH, H, MLSTM_DH, MLSTM_DH), 0.1),
        'state_mlstm_n': nrm((DEPTH, DEC_BATCH, H, MLSTM_DH), 0.5),
        'state_mlstm_m': nrm((DEPTH, DEC_BATCH, H), 0.5),
        'state_pool': nrm((DEPTH, DEC_BATCH, POOL_BUF, POOL_WIDTH), 1.0),
        'c_prompt': nrm((BATCH, D), 1.0),
        'c_sample': nrm((DEC_BATCH, D), 1.0),
        'w_ada': nrm((DEPTH, D, 6 * D), 0.5 * D ** -0.5),
        'b_ada': nrm((DEPTH, 6 * D), 0.01),
        'w_in': nrm((DEPTH, D, IN_COLS), D ** -0.5),
        'b_gate': b_gate + nrm((DEPTH, 2 * H), 0.1),
        'mh_g': 1.0 + nrm((DEPTH, MLSTM_WIDTH), 0.01),
        'sgu_g': 1.0 + nrm((DEPTH, SGU_WIDTH), 0.01),
        'sgu_b': nrm((DEPTH, SGU_WIDTH), 0.01),
        'w_s': nrm((DEPTH, SGU_HEADS, SGU_CHUNK, SGU_CHUNK), SGU_CHUNK ** -0.5),
        'b_s': 1.0 + nrm((DEPTH, SGU_HEADS, SGU_CHUNK), 0.01),
        'w_pool': nrm((DEPTH, POOL_GROUPS, POOL_DG, POOL_DG), POOL_DG ** -0.5),
        'pool_scale': 1.0 + nrm((DEPTH, POOL_WIDTH), 0.01),
        'w_o': nrm((DEPTH, D, D), BETA * D ** -0.5),
        'ln1_g': 1.0 + nrm((DEPTH, D), 0.01),
        'ln1_b': nrm((DEPTH, D), 0.01),
        'w_pq': nrm((DEPTH, D, PEER_HEADS * PEER_DQ), D ** -0.5),
        'peer_keys': nrm((DEPTH, PEER_HEADS, 2, PEER_NKEYS, PEER_DK), PEER_DK ** -0.5),
        'peer_u': nrm((DEPTH, PEER_EXPERTS, D), D ** -0.5),
        'peer_v': nrm((DEPTH, PEER_EXPERTS, D), BETA * PEER_HEADS ** -0.5),
        'ln2_g': 1.0 + nrm((DEPTH, D), 0.01),
        'ln2_b': nrm((DEPTH, D), 0.01),
    }


def reference(x_prompt, x_sample, state_mlstm_C, state_mlstm_n, state_mlstm_m, state_pool, c_prompt, c_sample,
              w_ada, b_ada, w_in, b_gate, mh_g, sgu_g, sgu_b, w_s, b_s, w_pool, pool_scale, w_o, ln1_g, ln1_b,
              w_pq, peer_keys, peer_u, peer_v, ln2_g, ln2_b):
    f32 = jnp.float32
    Bp = x_prompt.shape[0]
    xp, xs = x_prompt, x_sample
    pC, pn, pm, pp, sC, sn, sm, sp, sv = ([] for _ in range(9))
    for l in range(DEPTH):
        lp = (w_ada[l], b_ada[l], w_in[l], b_gate[l], mh_g[l], sgu_g[l], sgu_b[l], w_s[l], b_s[l], w_pool[l],
              pool_scale[l], w_o[l], ln1_g[l], ln1_b[l], w_pq[l], peer_keys[l], peer_u[l], peer_v[l], ln2_g[l], ln2_b[l])
        st0 = (jnp.zeros((Bp, MLSTM_HEADS, MLSTM_DH, MLSTM_DH), f32),
               jnp.zeros((Bp, MLSTM_HEADS, MLSTM_DH), f32),
               jnp.zeros((Bp, MLSTM_HEADS), f32))
        buf0 = jnp.zeros((Bp, 0, POOL_WIDTH), x_prompt.dtype)
        xp, (C, n, m), buf, _ = trunk_layer(xp, c_prompt, st0, buf0, 0, *lp)
        pC.append(C); pn.append(n); pm.append(m); pp.append(buf)
        st = (state_mlstm_C[l].astype(f32), state_mlstm_n[l].astype(f32), state_mlstm_m[l].astype(f32))
        xs, (C, n, m), buf, v_rows = trunk_layer(xs, c_sample, st, state_pool[l], PAST_LEN, *lp)
        sC.append(C); sn.append(n); sm.append(m); sp.append(buf); sv.append(v_rows)
    return (xp, xs, jnp.stack(pC), jnp.stack(pn), jnp.stack(pm), jnp.stack(pp),
            jnp.stack(sC), jnp.stack(sn), jnp.stack(sm), jnp.stack(sp), jnp.stack(sv))
```

```python
import functools

import jax
import jax.numpy as jnp
import numpy as np
from jax import lax
from jax.experimental import pallas as pl
from jax.experimental.pallas import tpu as pltpu

F32 = jnp.float32
BF16 = jnp.bfloat16
HIGHEST = lax.Precision.HIGHEST

D_MODEL = 1024
DEPTH = 2
N_ADA = 6
MLSTM_HEADS = 4
MLSTM_DH = 128
MLSTM_WIDTH = MLSTM_HEADS * MLSTM_DH
SGU_WIDTH = 256
SGU_HEADS = 4
SGU_DH = SGU_WIDTH // SGU_HEADS
SGU_CHUNK = 128
POOL_WIDTH = 256
POOL_WINDOWS = (2, 4, 8, 16)
POOL_HIST = 16
PEER_HEADS = 8
PEER_NKEYS = 128
PEER_TOPK = 16
PEER_DK = 128
PEER_EXPERTS = PEER_NKEYS * PEER_NKEYS
PAST_LEN = 16384
ALPHA = (2 * DEPTH) ** 0.25
LN_EPS = 1e-5

LANES = 128
SUBLANES = 8
VMEM_LIMIT_BYTES = 56 * 1024 * 1024

COL_Q = 0
COL_K = MLSTM_WIDTH
COL_V = 2 * MLSTM_WIDTH
COL_O = 3 * MLSTM_WIDTH
COL_GATE = 4 * MLSTM_WIDTH
COL_U = COL_GATE + LANES
COL_VS = COL_U + SGU_WIDTH
COL_P = COL_VS + SGU_WIDTH
IN_COLS_PADDED = COL_P + POOL_WIDTH

NT_DIMS = (((1,), (1,)), ((), ()))
TN_DIMS = (((0,), (0,)), ((), ()))


def _layer_norm(x):
    mu = jnp.mean(x, axis=-1, keepdims=True)
    d = x - mu
    var = jnp.mean(d * d, axis=-1, keepdims=True)
    return d * lax.rsqrt(var + LN_EPS)


def _ada_kernel(c_ref, w_ref, b_ref, o_ref):
    c = c_ref[...]
    s = c * jax.nn.sigmoid(c)
    o_ref[0] = jnp.dot(s, w_ref[0], precision=HIGHEST, preferred_element_type=F32) + b_ref[0]


def _ada(c_all, w_ada, b_ada):
    rows = c_all.shape[0]
    cols = w_ada.shape[-1]
    tile = 1536
    return pl.pallas_call(
        _ada_kernel,
        grid=(DEPTH, cols // tile),
        in_specs=[
            pl.BlockSpec((rows, D_MODEL), lambda l, j: (0, 0)),
            pl.BlockSpec((1, D_MODEL, tile), lambda l, j: (l, 0, j)),
            pl.BlockSpec((1, 1, tile), lambda l, j: (l, 0, j)),
        ],
        out_specs=pl.BlockSpec((1, rows, tile), lambda l, j: (l, 0, j)),
        out_shape=jax.ShapeDtypeStruct((DEPTH, rows, cols), F32),
        compiler_params=pltpu.CompilerParams(vmem_limit_bytes=VMEM_LIMIT_BYTES),
        name="ada",
    )(c_all, w_ada, b_ada.reshape(DEPTH, 1, cols))


def _mlstm_chunk(q, k, v, ig_col, ig_row, b_col, b_row, causal, c_state, n_state, m_state):
    length = q.shape[0]
    dlog = jnp.where(causal, b_col - b_row + ig_row, -jnp.inf)
    inter = b_col + m_state
    m_t = jnp.maximum(inter, jnp.max(dlog, axis=1, keepdims=True))
    w_intra = jnp.exp(dlog - m_t)
    w_inter = jnp.exp(inter - m_t)
    qb = q.astype(BF16)
    kb = k.astype(BF16)
    scores = lax.dot_general(qb, kb, NT_DIMS, preferred_element_type=F32)
    a = w_intra * scores
    num = jnp.dot(a.astype(BF16), v.astype(BF16), preferred_element_type=F32)
    num = num + w_inter * lax.dot_general(qb, c_state.astype(BF16), NT_DIMS, preferred_element_type=F32)
    den = jnp.sum(a, axis=1, keepdims=True) + w_inter * jnp.sum(q * n_state, axis=1, keepdims=True)
    h = num / jnp.maximum(jnp.abs(den), jnp.exp(-m_t))
    b_end = b_col[length - 1:length, :]
    dend = b_end - b_col + ig_col
    m_new = jnp.maximum(b_end + m_state, jnp.max(dend, axis=0, keepdims=True))
    wc = jnp.exp(dend - m_new)
    dec = jnp.exp(b_end + m_state - m_new)
    vw = (v * wc).astype(BF16)
    c_new = dec * c_state + lax.dot_general(vw, kb, TN_DIMS, preferred_element_type=F32)
    n_new = dec * n_state + jnp.sum(k * wc, axis=0, keepdims=True)
    return h, c_new, n_new, m_new


def _mixer_kernel(x_ref, mod_ref, c_in_ref, n_in_ref, m_in_ref, hist_ref,
                  w_in_ref, wgate_ref, bgate_ref, mhg_ref, sgug_ref, sgub_ref, ws_ref, bs_ref,
                  wpool_ref, pscale_ref, wo_ref, ln1g_ref, ln1b_ref,
                  x1_ref, c_out_ref, n_out_ref, m_out_ref, tail_ref, vn_ref,
                  proj_sc, ycat_sc, c_sc, n_sc, m_sc, carry_sc, pbuf_sc,
                  *, nb, rows, chunk, n_valid, sgu_blk, pos0):
    ci = pl.program_id(1)
    n_chunks = pl.num_programs(1)
    m_rows = nb * rows

    @pl.when(ci == 0)
    def _():
        c_sc[...] = c_in_ref[...]
        n_sc[...] = n_in_ref[...]
        m_sc[...] = m_in_ref[...]
        carry_sc[...] = hist_ref[...]

    x3 = x_ref[...]
    mod = mod_ref[...]
    sh1, sc1, g1 = mod[:, 0:1, :], mod[:, 1:2, :], mod[:, 2:3, :]
    h3 = x3 * (1.0 + sc1) + sh1
    h2d = h3.reshape(m_rows, D_MODEL)
    proj_sc[...] = jnp.dot(h2d.astype(BF16), w_in_ref[...], preferred_element_type=F32)
    proj_sc[:, COL_GATE:COL_GATE + LANES] = jnp.dot(h2d, wgate_ref[...], precision=HIGHEST,
                                                    preferred_element_type=F32)

    r_io = lax.broadcasted_iota(jnp.int32, (chunk, chunk), 0)
    c_io = lax.broadcasted_iota(jnp.int32, (chunk, chunk), 1)
    causal = c_io <= r_io
    tri = jnp.where(causal, 1.0, 0.0).astype(F32)
    row_id = lax.broadcasted_iota(jnp.int32, (chunk, 1), 0)
    valid_col = row_id < n_valid
    bgate = bgate_ref[...]
    mhg = mhg_ref[...]

    def seq_body(s, carry):
        for j in range(rows // chunk):
            row0 = s * rows + j * chunk
            if not isinstance(row0, int):
                row0 = pl.multiple_of(row0, SUBLANES)
            rsl = pl.ds(row0, chunk)
            gates = proj_sc[rsl, COL_GATE:COL_GATE + LANES] + bgate
            lf = jnp.where(valid_col, jax.nn.log_sigmoid(gates), 0.0)
            bcum = jnp.dot(tri, lf, precision=HIGHEST, preferred_element_type=F32)
            ig_all = jnp.where(valid_col, gates, -jnp.inf)
            bcum_t = bcum.T
            ig_t = ig_all.T
            for h in range(MLSTM_HEADS):
                hs = slice(h * MLSTM_DH, (h + 1) * MLSTM_DH)
                q = proj_sc[rsl, COL_Q + h * MLSTM_DH:COL_Q + (h + 1) * MLSTM_DH] * (MLSTM_DH ** -0.5)
                k = proj_sc[rsl, COL_K + h * MLSTM_DH:COL_K + (h + 1) * MLSTM_DH]
                v = proj_sc[rsl, COL_V + h * MLSTM_DH:COL_V + (h + 1) * MLSTM_DH]
                o = proj_sc[rsl, COL_O + h * MLSTM_DH:COL_O + (h + 1) * MLSTM_DH]
                fcol = MLSTM_HEADS + h
                hh, c_new, n_new, m_new = _mlstm_chunk(
                    q, k, v,
                    ig_all[:, h:h + 1], ig_t[h:h + 1, 0:chunk],
                    bcum[:, fcol:fcol + 1], bcum_t[fcol:fcol + 1, 0:chunk],
                    causal, c_sc[s, h], n_sc[s, pl.ds(h, 1), :], m_sc[s, pl.ds(h, 1), 0:1])
                c_sc[s, h] = c_new
                n_sc[s, pl.ds(h, 1), :] = n_new
                m_sc[s, pl.ds(h, 1), :] = jnp.broadcast_to(m_new, (1, LANES))
                hn = _layer_norm(hh) * mhg[:, hs]
                ycat_sc[rsl, hs] = jax.nn.sigmoid(o) * hn
        return carry

    if nb == 1:
        seq_body(0, 0)
    else:
        lax.fori_loop(0, nb, seq_body, 0)

    u_s = proj_sc[:, COL_U:COL_U + SGU_WIDTH]
    v_s = proj_sc[:, COL_VS:COL_VS + SGU_WIDTH]
    gr = lax.broadcasted_iota(jnp.int32, (SGU_WIDTH, SGU_WIDTH), 0) // SGU_DH
    gc = lax.broadcasted_iota(jnp.int32, (SGU_WIDTH, SGU_WIDTH), 1) // SGU_DH
    avg = jnp.where(gr == gc, 1.0 / SGU_DH, 0.0).astype(F32)
    mu = jnp.dot(v_s, avg, precision=HIGHEST, preferred_element_type=F32)
    dv = v_s - mu
    var = jnp.dot(dv * dv, avg, precision=HIGHEST, preferred_element_type=F32)
    vn = dv * lax.rsqrt(var + LN_EPS) * sgug_ref[...] + sgub_ref[...]
    if vn_ref is not None:
        vn_ref[...] = vn.reshape(nb, rows, SGU_WIDTH)
    sr = lax.broadcasted_iota(jnp.int32, (SGU_CHUNK, SGU_CHUNK), 0)
    scol = lax.broadcasted_iota(jnp.int32, (SGU_CHUNK, SGU_CHUNK), 1)
    smask = (scol <= sr) & ((sr // sgu_blk) == (scol // sgu_blk))
    lane_grp = lax.broadcasted_iota(jnp.int32, (SGU_CHUNK, SGU_WIDTH), 1) // SGU_DH
    vnb = vn.astype(BF16)
    for r in range(m_rows // SGU_CHUNK):
        rs = slice(r * SGU_CHUNK, (r + 1) * SGU_CHUNK)
        mix = jnp.zeros((SGU_CHUNK, SGU_WIDTH), F32)
        for g in range(SGU_HEADS):
            wg = jnp.where(smask, ws_ref[g], 0.0).astype(BF16)
            mg = jnp.dot(wg, vnb[rs], preferred_element_type=F32)
            mix = jnp.where(lane_grp == g, mg, mix)
        ycat_sc[rs, MLSTM_WIDTH:MLSTM_WIDTH + SGU_WIDTH] = u_s[rs] * (mix + bs_ref[...])

    pbuf_sc[:, 0:POOL_HIST, :] = carry_sc[...]
    pbuf_sc[:, POOL_HIST:POOL_HIST + rows, :] = proj_sc[:, COL_P:COL_P + POOL_WIDTH].reshape(nb, rows, POOL_WIDTH)
    lane_w = lax.broadcasted_iota(jnp.int32, (1, 1, POOL_WIDTH), 2) // (POOL_WIDTH // len(POOL_WINDOWS))
    x0 = pbuf_sc[:, POOL_HIST:POOL_HIST + rows, :]
    acc = x0
    wsum = jnp.zeros_like(x0)
    for kk in range(1, max(POOL_WINDOWS)):
        acc = acc + pbuf_sc[:, POOL_HIST - kk:POOL_HIST - kk + rows, :]
        if (kk + 1) in POOL_WINDOWS:
            wsum = jnp.where(lane_w == POOL_WINDOWS.index(kk + 1), acc, wsum)
    t_io = lax.broadcasted_iota(jnp.int32, (1, rows, POOL_WIDTH), 1)
    win = jnp.left_shift(2, lane_w)
    cnt = jnp.minimum(pos0 + ci * rows + t_io + 1, win).astype(F32)
    pooled = wsum / cnt - x0
    y_c = jnp.dot(pooled.reshape(m_rows, POOL_WIDTH).astype(BF16), wpool_ref[...],
                  preferred_element_type=F32) * pscale_ref[...]
    ycat_sc[:, MLSTM_WIDTH + SGU_WIDTH:D_MODEL] = y_c
    carry_sc[...] = pbuf_sc[:, rows:rows + POOL_HIST, :]

    y = jnp.dot(ycat_sc[...].astype(BF16), wo_ref[...], preferred_element_type=F32)
    z = ALPHA * x3 + g1 * y.reshape(nb, rows, D_MODEL)
    x1_ref[...] = _layer_norm(z) * ln1g_ref[...] + ln1b_ref[...]

    @pl.when(ci == n_chunks - 1)
    def _():
        c_out_ref[...] = c_sc[...]
        n_out_ref[...] = n_sc[...]
        m_out_ref[...] = m_sc[...]
        tail_ref[...] = pbuf_sc[:, n_valid + rows - chunk:n_valid + rows - chunk + POOL_HIST, :]


def _mixer(x, mod, c0, n0, m0, hist, lw, *, nb, rows, chunk, n_valid, sgu_blk, pos0, want_vn):
    bsz, seq, _ = x.shape
    grid = (bsz // nb, seq // rows)
    m_rows = nb * rows
    kern = functools.partial(_mixer_kernel, nb=nb, rows=rows, chunk=chunk, n_valid=n_valid,
                             sgu_blk=sgu_blk, pos0=pos0)
    if not want_vn:
        def kern(*refs, _k=kern):
            return _k(*refs[:24], None, *refs[24:])

    def full(shape):
        return pl.BlockSpec(shape, lambda b, c: (0,) * len(shape))

    per_b3 = lambda s1, s2: pl.BlockSpec((nb, s1, s2), lambda b, c: (b, 0, 0))
    in_specs = [
        pl.BlockSpec((nb, rows, D_MODEL), lambda b, c: (b, c, 0)),
        per_b3(SUBLANES, D_MODEL),
        pl.BlockSpec((nb, MLSTM_HEADS, MLSTM_DH, MLSTM_DH), lambda b, c: (b, 0, 0, 0)),
        per_b3(MLSTM_HEADS, MLSTM_DH),
        per_b3(MLSTM_HEADS, LANES),
        per_b3(POOL_HIST, POOL_WIDTH),
        full((D_MODEL, IN_COLS_PADDED)),
        full((D_MODEL, LANES)),
        full((1, LANES)),
        full((1, MLSTM_WIDTH)),
        full((1, SGU_WIDTH)),
        full((1, SGU_WIDTH)),
        full((SGU_HEADS, SGU_CHUNK, SGU_CHUNK)),
        full((SGU_CHUNK, SGU_WIDTH)),
        full((POOL_WIDTH, POOL_WIDTH)),
        full((1, POOL_WIDTH)),
        full((D_MODEL, D_MODEL)),
        full((1, D_MODEL)),
        full((1, D_MODEL)),
    ]
    out_specs = [
        pl.BlockSpec((nb, rows, D_MODEL), lambda b, c: (b, c, 0)),
        pl.BlockSpec((nb, MLSTM_HEADS, MLSTM_DH, MLSTM_DH), lambda b, c: (b, 0, 0, 0)),
        per_b3(MLSTM_HEADS, MLSTM_DH),
        per_b3(MLSTM_HEADS, LANES),
        per_b3(POOL_HIST, POOL_WIDTH),
    ]
    out_shape = [
        jax.ShapeDtypeStruct((bsz, seq, D_MODEL), F32),
        jax.ShapeDtypeStruct((bsz, MLSTM_HEADS, MLSTM_DH, MLSTM_DH), F32),
        jax.ShapeDtypeStruct((bsz, MLSTM_HEADS, MLSTM_DH), F32),
        jax.ShapeDtypeStruct((bsz, MLSTM_HEADS, LANES), F32),
        jax.ShapeDtypeStruct((bsz, POOL_HIST, POOL_WIDTH), F32),
    ]
    if want_vn:
        out_specs.append(pl.BlockSpec((nb, rows, SGU_WIDTH), lambda b, c: (b, c, 0)))
        out_shape.append(jax.ShapeDtypeStruct((bsz, seq, SGU_WIDTH), F32))
    scratch = [
        pltpu.VMEM((m_rows, IN_COLS_PADDED), F32),
        pltpu.VMEM((m_rows, D_MODEL), F32),
        pltpu.VMEM((nb, MLSTM_HEADS, MLSTM_DH, MLSTM_DH), F32),
        pltpu.VMEM((nb, MLSTM_HEADS, MLSTM_DH), F32),
        pltpu.VMEM((nb, MLSTM_HEADS, LANES), F32),
        pltpu.VMEM((nb, POOL_HIST, POOL_WIDTH), F32),
        pltpu.VMEM((nb, POOL_HIST + rows, POOL_WIDTH), F32),
    ]
    return pl.pallas_call(
        kern,
        grid=grid,
        in_specs=in_specs,
        out_specs=out_specs,
        out_shape=out_shape,
        scratch_shapes=scratch,
        compiler_params=pltpu.CompilerParams(
            dimension_semantics=("arbitrary", "arbitrary"), vmem_limit_bytes=VMEM_LIMIT_BYTES),
        name="mixer_sample" if want_vn else "mixer_prompt",
    )(x, mod, c0, n0, m0, hist, lw["w_in"], lw["w_gate"], lw["b_gate"], lw["mh_g"], lw["sgu_g"], lw["sgu_b"],
      lw["w_s_sample"] if want_vn else lw["w_s"], lw["b_s_sample"] if want_vn else lw["b_s"],
      lw["w_pool"], lw["pool_scale"], lw["w_o"], lw["ln1_g"], lw["ln1_b"])


def _sort16_pairs():
    pairs = []

    def merge(lo, hi, r):
        step = r * 2
        if step < hi - lo:
            merge(lo, hi, step)
            merge(lo + r, hi, step)
            for i in range(lo + r, hi - r, step):
                pairs.append((i, i + r))
        else:
            pairs.append((lo, lo + r))

    def sort(lo, hi):
        if hi - lo >= 1:
            mid = lo + (hi - lo) // 2
            sort(lo, mid)
            sort(mid + 1, hi)
            merge(lo, hi, 1)

    sort(0, PEER_TOPK - 1)
    return tuple(pairs)


SORT16_PAIRS = _sort16_pairs()


def _sort16_desc(vals):
    vals = list(vals)
    for i, j in SORT16_PAIRS:
        hi = jnp.maximum(vals[i], vals[j])
        lo = jnp.minimum(vals[i], vals[j])
        vals[i], vals[j] = hi, lo
    return vals


def _merge_top16(xs, ys):
    vals = [jnp.maximum(xs[i], ys[PEER_TOPK - 1 - i]) for i in range(PEER_TOPK)]
    d = PEER_TOPK // 2
    while d >= 1:
        for i in range(PEER_TOPK):
            if (i & d) == 0:
                hi = jnp.maximum(vals[i], vals[i + d])
                lo = jnp.minimum(vals[i], vals[i + d])
                vals[i], vals[i + d] = hi, lo
        d //= 2
    return vals


def _top16_desc(tiles):
    tiles = list(tiles)
    neg = jnp.full_like(tiles[0], -jnp.inf)
    while len(tiles) % PEER_TOPK:
        tiles.append(neg)
    best = None
    for g in range(len(tiles) // PEER_TOPK):
        grp = _sort16_desc(tiles[g * PEER_TOPK:(g + 1) * PEER_TOPK])
        best = grp if best is None else _merge_top16(best, grp)
    return best


S_PITCH = PEER_NKEYS + SUBLANES


def _peer_select_kernel(x_ref, sh_ref, sc_ref, wq_ref, keys_ref,
                        s1_ref, c1_ref, s2_ref, e2_ref, tau_ref,
                        q_sc, sa_sc, sb_sc, *, tn):
    s_sc = (sa_sc, sb_sc)
    x = x_ref[...]
    sh = sh_ref[...].reshape(-1, D_MODEL)
    sc = sc_ref[...].reshape(-1, D_MODEL)
    h2 = (x * (1.0 + sc) + sh).astype(BF16)
    q_sc[...] = lax.dot_general(wq_ref[...], h2, NT_DIMS, preferred_element_type=F32).astype(BF16)
    for h in range(PEER_HEADS):
        for p in range(2):
            r0 = (h * 2 + p) * PEER_DK
            s_hp = jnp.dot(keys_ref[h * 2 + p], q_sc[r0:r0 + PEER_DK, :], preferred_element_type=F32)
            for lg in range(tn // LANES):
                s_sc[p][lg, h * S_PITCH:h * S_PITCH + PEER_NKEYS, :] = s_hp[:, lg * LANES:(lg + 1) * LANES]
    for lg in range(tn // LANES):
        lanes = pl.ds(lg * LANES, LANES)
        tops = []
        for p in range(2):
            tiles = [s_sc[p][lg, pl.ds(i, PEER_HEADS, stride=S_PITCH), :] for i in range(PEER_NKEYS)]
            tops.append(_top16_desc(tiles))
        a, b = tops
        cands = []
        for r1 in range(PEER_TOPK):
            for r2 in range(PEER_TOPK // (r1 + 1)):
                cands.append(b[r2] + a[r1])
        c = _top16_desc(cands)
        tau = c[PEER_TOPK - 1]
        z = jnp.zeros_like(tau)
        for kk in range(PEER_TOPK):
            z = z + jnp.exp(c[kk] - c[0])
        zinv = 1.0 / z
        tau_ref[:, lanes] = tau
        for h in range(PEER_HEADS):
            rows_in = slice(h * S_PITCH, h * S_PITCH + PEER_NKEYS)
            rows_out = slice(h * PEER_NKEYS, (h + 1) * PEER_NKEYS)
            s1 = sa_sc[lg, rows_in, :]
            s2 = sb_sc[lg, rows_in, :]
            s1_ref[rows_out, lanes] = s1
            s2_ref[rows_out, lanes] = s2
            c1_ref[rows_out, lanes] = jnp.exp(s1 - a[0][h:h + 1, :]) * zinv[h:h + 1, :]
            e2_ref[rows_out, lanes] = jnp.exp(s2 - b[0][h:h + 1, :])


def _mod_spec(arr, tn, tiles_per_row):
    if arr.ndim == 3:
        return pl.BlockSpec((1, 1, D_MODEL), lambda i, *_: (i // tiles_per_row, 0, 0))
    return pl.BlockSpec((tn, D_MODEL), lambda i, *_: (i, 0))


def _peer_select(x1, sh2, sc2, lw, *, tn, tiles_per_row):
    n_tok = x1.shape[0]
    assert n_tok % tn == 0
    sel_rows = PEER_HEADS * PEER_NKEYS
    big = lambda: pl.BlockSpec((sel_rows, tn), lambda i: (0, i))
    return pl.pallas_call(
        functools.partial(_peer_select_kernel, tn=tn),
        grid=(n_tok // tn,),
        in_specs=[
            pl.BlockSpec((tn, D_MODEL), lambda i: (i, 0)),
            _mod_spec(sh2, tn, tiles_per_row),
            _mod_spec(sc2, tn, tiles_per_row),
            pl.BlockSpec((2 * PEER_HEADS * PEER_DK, D_MODEL), lambda i: (0, 0)),
            pl.BlockSpec((2 * PEER_HEADS, PEER_NKEYS, PEER_DK), lambda i: (0, 0, 0)),
        ],
        out_specs=[big(), big(), big(), big(), pl.BlockSpec((PEER_HEADS, tn), lambda i: (0, i))],
        out_shape=[jax.ShapeDtypeStruct((sel_rows, n_tok), F32)] * 4
        + [jax.ShapeDtypeStruct((PEER_HEADS, n_tok), F32)],
        scratch_shapes=[
            pltpu.VMEM((2 * PEER_HEADS * PEER_DK, tn), BF16),
            pltpu.VMEM((tn // LANES, PEER_HEADS * S_PITCH, LANES), F32),
            pltpu.VMEM((tn // LANES, PEER_HEADS * S_PITCH, LANES), F32),
        ],
        compiler_params=pltpu.CompilerParams(
            dimension_semantics=("arbitrary",), vmem_limit_bytes=VMEM_LIMIT_BYTES),
        name="peer_select",
    )(x1, sh2, sc2, lw["w_pq_t"], lw["peer_keys"])


def _peer_dense_kernel(x_ref, sh_ref, sc_ref, g_ref, s1_ref, c1_ref, s2_ref, e2_ref, tau_ref,
                       u_ref, vt_ref, lng_ref, lnb_ref, o_ref,
                       xb_sc, acc_sc, at_sc, coef_sc, *, tn, te):
    j = pl.program_id(1)
    n_j = pl.num_programs(1)

    @pl.when(j == 0)
    def _():
        sh = sh_ref[...].reshape(-1, D_MODEL)
        sc = sc_ref[...].reshape(-1, D_MODEL)
        xb_sc[...] = (x_ref[...] * (1.0 + sc) + sh).astype(BF16)
        acc_sc[...] = jnp.zeros_like(acc_sc)

    at_sc[...] = lax.dot_general(u_ref[...], xb_sc[...], NT_DIMS, preferred_element_type=F32)
    n_q = te // PEER_NKEYS
    assert n_q == SUBLANES
    for lg in range(tn // LANES):
        lanes = pl.ds(lg * LANES, LANES)
        s1_tiles, c1_tiles = [], []
        for h in range(PEER_HEADS):
            krows = pl.ds(pl.multiple_of(h * PEER_NKEYS + j * n_q, SUBLANES), SUBLANES)
            s1_tiles.append(s1_ref[krows, lanes])
            c1_tiles.append(c1_ref[krows, lanes])
        for q in range(n_q):
            rows = slice(q * PEER_NKEYS, (q + 1) * PEER_NKEYS)
            gate = jnp.zeros((PEER_NKEYS, LANES), F32)
            for h in range(PEER_HEADS):
                hrows = slice(h * PEER_NKEYS, (h + 1) * PEER_NKEYS)
                s1row = s1_tiles[h][q:q + 1, :]
                c1row = c1_tiles[h][q:q + 1, :]
                taurow = tau_ref[h:h + 1, lanes]
                picked = (s2_ref[hrows, lanes] + s1row) >= taurow
                gate = gate + jnp.where(picked, e2_ref[hrows, lanes] * c1row, 0.0)
            a = at_sc[rows, lanes]
            act = 0.5 * a * (1.0 + lax.erf(a * (2.0 ** -0.5)))
            coef_sc[rows, lanes] = (gate * act).astype(BF16)
    acc_sc[...] += jnp.dot(vt_ref[...], coef_sc[...], preferred_element_type=F32)

    @pl.when(j == n_j - 1)
    def _():
        g2 = g_ref[...].reshape(-1, D_MODEL)
        z = ALPHA * x_ref[...] + g2 * acc_sc[...].T
        o_ref[...] = _layer_norm(z) * lng_ref[...] + lnb_ref[...]


def _peer_dense(x1, sh2, sc2, g2, sel, lw, *, tn, te, tiles_per_row):
    n_tok = x1.shape[0]
    assert n_tok % tn == 0
    sel_rows = PEER_HEADS * PEER_NKEYS
    s1, c1, s2, e2, tau = sel
    big = lambda: pl.BlockSpec((sel_rows, tn), lambda i, j: (0, i))
    return pl.pallas_call(
        functools.partial(_peer_dense_kernel, tn=tn, te=te),
        grid=(n_tok // tn, PEER_EXPERTS // te),
        in_specs=[
            pl.BlockSpec((tn, D_MODEL), lambda i, j: (i, 0)),
            _mod_spec(sh2, tn, tiles_per_row),
            _mod_spec(sc2, tn, tiles_per_row),
            _mod_spec(g2, tn, tiles_per_row),
            big(), big(), big(), big(),
            pl.BlockSpec((PEER_HEADS, tn), lambda i, j: (0, i)),
            pl.BlockSpec((te, D_MODEL), lambda i, j: (j, 0)),
            pl.BlockSpec((D_MODEL, te), lambda i, j: (0, j)),
            pl.BlockSpec((1, D_MODEL), lambda i, j: (0, 0)),
            pl.BlockSpec((1, D_MODEL), lambda i, j: (0, 0)),
        ],
        out_specs=pl.BlockSpec((tn, D_MODEL), lambda i, j: (i, 0)),
        out_shape=jax.ShapeDtypeStruct((n_tok, D_MODEL), F32),
        scratch_shapes=[
            pltpu.VMEM((tn, D_MODEL), BF16),
            pltpu.VMEM((D_MODEL, tn), F32),
            pltpu.VMEM((te, tn), F32),
            pltpu.VMEM((te, tn), BF16),
        ],
        compiler_params=pltpu.CompilerParams(
            dimension_semantics=("arbitrary", "arbitrary"), vmem_limit_bytes=VMEM_LIMIT_BYTES),
        name="peer_dense",
    )(x1, sh2, sc2, g2, s1, c1, s2, e2, tau, lw["peer_u"], lw["peer_v_t"], lw["ln2_g"], lw["ln2_b"])


def _prep_layer(l, w_in, b_gate, mh_g, sgu_g, sgu_b, w_s, b_s, w_pool, pool_scale, w_o, ln1_g, ln1_b,
                w_pq, peer_keys, peer_u, peer_v, ln2_g, ln2_b, sample_rows):
    n_gate = 2 * MLSTM_HEADS
    g0 = 4 * MLSTM_WIDTH
    w = w_in[l]
    w_gate = jnp.pad(w[:, g0:g0 + n_gate], ((0, 0), (0, LANES - n_gate)))
    w_in_p = jnp.concatenate([w[:, :g0], w_gate, w[:, g0 + n_gate:]], axis=1).astype(BF16)
    bs_full = jnp.repeat(jnp.swapaxes(b_s[l], 0, 1), SGU_DH, axis=1)
    reps = SGU_CHUNK // sample_rows
    eye = jnp.eye(len(POOL_WINDOWS), dtype=F32)
    dg = POOL_WIDTH // len(POOL_WINDOWS)
    w_pool_bd = (eye[:, None, :, None] * w_pool[l][:, :, None, :]).reshape(POOL_WIDTH, POOL_WIDTH)
    row = lambda a: a.reshape(1, -1)
    return {
        "w_in": w_in_p,
        "w_gate": w_gate,
        "b_gate": jnp.pad(b_gate[l], (0, LANES - n_gate)).reshape(1, LANES),
        "mh_g": row(mh_g[l]), "sgu_g": row(sgu_g[l]), "sgu_b": row(sgu_b[l]),
        "w_s": w_s[l],
        "b_s": bs_full,
        "w_s_sample": jnp.tile(w_s[l][:, :sample_rows, :sample_rows], (1, reps, reps)),
        "b_s_sample": jnp.tile(bs_full[:sample_rows], (reps, 1)),
        "w_pool": w_pool_bd.astype(BF16),
        "pool_scale": row(pool_scale[l]),
        "w_o": w_o[l].astype(BF16),
        "ln1_g": row(ln1_g[l]), "ln1_b": row(ln1_b[l]),
        "w_pq_t": w_pq[l].T.astype(BF16),
        "peer_keys": peer_keys[l].reshape(2 * PEER_HEADS, PEER_NKEYS, PEER_DK).astype(BF16),
        "peer_u": peer_u[l].astype(BF16),
        "peer_v_t": peer_v[l].T.astype(BF16),
        "ln2_g": row(ln2_g[l]), "ln2_b": row(ln2_b[l]),
    }


def kernel(x_prompt, x_sample, state_mlstm_C, state_mlstm_n, state_mlstm_m, state_pool, c_prompt, c_sample,
           w_ada, b_ada, w_in, b_gate, mh_g, sgu_g, sgu_b, w_s, b_s, w_pool, pool_scale, w_o, ln1_g, ln1_b,
           w_pq, peer_keys, peer_u, peer_v, ln2_g, ln2_b):
    bp, seq, _ = x_prompt.shape
    bs, dec_seq, _ = x_sample.shape
    sample_rows = SUBLANES
    prompt_rows = min(seq, 256)
    peer_tn = 512
    peer_te = 1024
    sel_tn = 256

    ada = _ada(jnp.concatenate([c_prompt, c_sample], axis=0), w_ada, b_ada)

    xp = x_prompt
    xs = jnp.pad(x_sample, ((0, 0), (0, sample_rows - dec_seq), (0, 0)))
    zeros_c = jnp.zeros((bp, MLSTM_HEADS, MLSTM_DH, MLSTM_DH), F32)
    zeros_n = jnp.zeros((bp, MLSTM_HEADS, MLSTM_DH), F32)
    zeros_m = jnp.zeros((bp, MLSTM_HEADS, LANES), F32)
    zeros_hist = jnp.zeros((bp, POOL_HIST, POOL_WIDTH), F32)
    outs = [[] for _ in range(9)]
    for l in range(DEPTH):
        lw = _prep_layer(l, w_in, b_gate, mh_g, sgu_g, sgu_b, w_s, b_s, w_pool, pool_scale, w_o, ln1_g, ln1_b,
                         w_pq, peer_keys, peer_u, peer_v, ln2_g, ln2_b, sample_rows)
        mods = jnp.pad(ada[l].reshape(bp + bs, N_ADA, D_MODEL), ((0, 0), (0, SUBLANES - N_ADA), (0, 0)))
        mod_p, mod_s = mods[:bp], mods[bp:]

        x1p, cp, np_, mp, tailp = _mixer(
            xp, mod_p, zeros_c, zeros_n, zeros_m, zeros_hist, lw,
            nb=1, rows=prompt_rows, chunk=SGU_CHUNK, n_valid=SGU_CHUNK, sgu_blk=SGU_CHUNK, pos0=0,
            want_vn=False)
        m_in = jnp.broadcast_to(state_mlstm_m[l][:, :, None], (bs, MLSTM_HEADS, LANES))
        hist_s = jnp.pad(state_pool[l], ((0, 0), (1, 0), (0, 0)))
        x1s, cs, ns, ms, tails, vns = _mixer(
            xs, mod_s, state_mlstm_C[l], state_mlstm_n[l], m_in, hist_s, lw,
            nb=16, rows=sample_rows, chunk=sample_rows, n_valid=dec_seq, sgu_blk=sample_rows, pos0=PAST_LEN,
            want_vn=True)

        x1p2 = x1p.reshape(bp * seq, D_MODEL)
        shp, scp, gp = mod_p[:, 3:4], mod_p[:, 4:5], mod_p[:, 5:6]
        sel = _peer_select(x1p2, shp, scp, lw, tn=sel_tn, tiles_per_row=seq // sel_tn)
        xp = _peer_dense(x1p2, shp, scp, gp, sel, lw, tn=peer_tn, te=peer_te,
                         tiles_per_row=seq // peer_tn).reshape(bp, seq, D_MODEL)
        x1s2 = x1s[:, :dec_seq].reshape(bs * dec_seq, D_MODEL)
        shs, scs, gs = (jnp.repeat(mod_s[:, i], dec_seq, axis=0) for i in (3, 4, 5))
        n_s = bs * dec_seq
        sel = _peer_select(x1s2, shs, scs, lw, tn=min(sel_tn, n_s), tiles_per_row=1)
        xs_new = _peer_dense(x1s2, shs, scs, gs, sel, lw, tn=min(peer_tn, n_s), te=peer_te, tiles_per_row=1)
        xs_new = xs_new.reshape(bs, dec_seq, D_MODEL)
        xs = jnp.pad(xs_new, ((0, 0), (0, sample_rows - dec_seq), (0, 0)))

        for lst, val in zip(outs, (cp, np_, mp[:, :, 0], tailp[:, 1:], cs, ns, ms[:, :, 0], tails[:, 1:],
                                   vns[:, :dec_seq])):
            lst.append(val)
    return (xp, xs_new) + tuple(jnp.stack(o) for o in outs)
```

```python
import functools

import jax
import jax.numpy as jnp
import numpy as np
from jax import lax
from jax.experimental import pallas as pl
from jax.experimental.pallas import tpu as pltpu

F32 = jnp.float32
BF16 = jnp.bfloat16
HIGHEST = lax.Precision.HIGHEST

D_MODEL = 1024
DEPTH = 2
N_ADA = 6
MLSTM_HEADS = 4
MLSTM_DH = 128
MLSTM_WIDTH = MLSTM_HEADS * MLSTM_DH
SGU_WIDTH = 256
SGU_HEADS = 4
SGU_DH = SGU_WIDTH // SGU_HEADS
SGU_CHUNK = 128
POOL_WIDTH = 256
POOL_WINDOWS = (2, 4, 8, 16)
POOL_HIST = 16
PEER_HEADS = 8
PEER_NKEYS = 128
PEER_TOPK = 16
PEER_DK = 128
PEER_EXPERTS = PEER_NKEYS * PEER_NKEYS
PAST_LEN = 16384
ALPHA = (2 * DEPTH) ** 0.25
LN_EPS = 1e-5

LANES = 128
SUBLANES = 8
VMEM_LIMIT_BYTES = 56 * 1024 * 1024

COL_Q = 0
COL_K = MLSTM_WIDTH
COL_V = 2 * MLSTM_WIDTH
COL_O = 3 * MLSTM_WIDTH
COL_GATE = 4 * MLSTM_WIDTH
COL_U = COL_GATE + LANES
COL_VS = COL_U + SGU_WIDTH
COL_P = COL_VS + SGU_WIDTH
IN_COLS_PADDED = COL_P + POOL_WIDTH

NT_DIMS = (((1,), (1,)), ((), ()))
TN_DIMS = (((0,), (0,)), ((), ()))


def _layer_norm(x):
    mu = jnp.mean(x, axis=-1, keepdims=True)
    d = x - mu
    var = jnp.mean(d * d, axis=-1, keepdims=True)
    return d * lax.rsqrt(var + LN_EPS)


def _ada_kernel(c_ref, w_ref, b_ref, o_ref):
    c = c_ref[...]
    s = c * jax.nn.sigmoid(c)
    o_ref[0] = jnp.dot(s, w_ref[0], precision=HIGHEST, preferred_element_type=F32) + b_ref[0]


def _ada(c_all, w_ada, b_ada):
    rows = c_all.shape[0]
    cols = w_ada.shape[-1]
    tile = 1536
    return pl.pallas_call(
        _ada_kernel,
        grid=(DEPTH, cols // tile),
        in_specs=[
            pl.BlockSpec((rows, D_MODEL), lambda l, j: (0, 0)),
            pl.BlockSpec((1, D_MODEL, tile), lambda l, j: (l, 0, j)),
            pl.BlockSpec((1, 1, tile), lambda l, j: (l, 0, j)),
        ],
        out_specs=pl.BlockSpec((1, rows, tile), lambda l, j: (l, 0, j)),
        out_shape=jax.ShapeDtypeStruct((DEPTH, rows, cols), F32),
        compiler_params=pltpu.CompilerParams(vmem_limit_bytes=VMEM_LIMIT_BYTES),
        name="ada",
    )(c_all, w_ada, b_ada.reshape(DEPTH, 1, cols))


def _mlstm_chunk(q, k, v, ig_col, ig_row, b_col, b_row, causal, c_state, n_state, m_state):
    length = q.shape[0]
    dlog = jnp.where(causal, b_col - b_row + ig_row, -jnp.inf)
    inter = b_col + m_state
    m_t = jnp.maximum(inter, jnp.max(dlog, axis=1, keepdims=True))
    w_intra = jnp.exp(dlog - m_t)
    w_inter = jnp.exp(inter - m_t)
    qb = q.astype(BF16)
    kb = k.astype(BF16)
    scores = lax.dot_general(qb, kb, NT_DIMS, preferred_element_type=F32)
    a = w_intra * scores
    num = jnp.dot(a.astype(BF16), v.astype(BF16), preferred_element_type=F32)
    num = num + w_inter * lax.dot_general(qb, c_state.astype(BF16), NT_DIMS, preferred_element_type=F32)
    den = jnp.sum(a, axis=1, keepdims=True) + w_inter * jnp.sum(q * n_state, axis=1, keepdims=True)
    h = num / jnp.maximum(jnp.abs(den), jnp.exp(-m_t))
    b_end = b_col[length - 1:length, :]
    dend = b_end - b_col + ig_col
    m_new = jnp.maximum(b_end + m_state, jnp.max(dend, axis=0, keepdims=True))
    wc = jnp.exp(dend - m_new)
    dec = jnp.exp(b_end + m_state - m_new)
    vw = (v * wc).astype(BF16)
    c_new = dec * c_state + lax.dot_general(vw, kb, TN_DIMS, preferred_element_type=F32)
    n_new = dec * n_state + jnp.sum(k * wc, axis=0, keepdims=True)
    return h, c_new, n_new, m_new


def _mixer_kernel(x_ref, mod_ref, c_in_ref, n_in_ref, m_in_ref, hist_ref,
                  w_in_ref, wgate_ref, bgate_ref, mhg_ref, sgug_ref, sgub_ref, ws_ref, bs_ref,
                  wpool_ref, pscale_ref, wo_ref, ln1g_ref, ln1b_ref,
                  x1_ref, c_out_ref, n_out_ref, m_out_ref, tail_ref, vn_ref,
                  proj_sc, ycat_sc, c_sc, n_sc, m_sc, carry_sc, pbuf_sc,
                  *, nb, rows, chunk, n_valid, sgu_blk, pos0):
    ci = pl.program_id(1)
    n_chunks = pl.num_programs(1)
    m_rows = nb * rows

    @pl.when(ci == 0)
    def _():
        c_sc[...] = c_in_ref[...]
        n_sc[...] = n_in_ref[...]
        m_sc[...] = m_in_ref[...]
        carry_sc[...] = hist_ref[...]

    x3 = x_ref[...]
    mod = mod_ref[...]
    sh1, sc1, g1 = mod[:, 0:1, :], mod[:, 1:2, :], mod[:, 2:3, :]
    h3 = x3 * (1.0 + sc1) + sh1
    h2d = h3.reshape(m_rows, D_MODEL)
    proj_sc[...] = jnp.dot(h2d.astype(BF16), w_in_ref[...], preferred_element_type=F32)
    proj_sc[:, COL_GATE:COL_GATE + LANES] = jnp.dot(h2d, wgate_ref[...], precision=HIGHEST,
                                                    preferred_element_type=F32)

    r_io = lax.broadcasted_iota(jnp.int32, (chunk, chunk), 0)
    c_io = lax.broadcasted_iota(jnp.int32, (chunk, chunk), 1)
    causal = c_io <= r_io
    tri = jnp.where(causal, 1.0, 0.0).astype(F32)
    row_id = lax.broadcasted_iota(jnp.int32, (chunk, 1), 0)
    valid_col = row_id < n_valid
    bgate = bgate_ref[...]
    mhg = mhg_ref[...]

    def seq_body(s, carry):
        for j in range(rows // chunk):
            row0 = s * rows + j * chunk
            if not isinstance(row0, int):
                row0 = pl.multiple_of(row0, SUBLANES)
            rsl = pl.ds(row0, chunk)
            gates = proj_sc[rsl, COL_GATE:COL_GATE + LANES] + bgate
            lf = jnp.where(valid_col, jax.nn.log_sigmoid(gates), 0.0)
            bcum = jnp.dot(tri, lf, precision=HIGHEST, preferred_element_type=F32)
            ig_all = jnp.where(valid_col, gates, -jnp.inf)
            bcum_t = bcum.T
            ig_t = ig_all.T
            for h in range(MLSTM_HEADS):
                hs = slice(h * MLSTM_DH, (h + 1) * MLSTM_DH)
                q = proj_sc[rsl, COL_Q + h * MLSTM_DH:COL_Q + (h + 1) * MLSTM_DH] * (MLSTM_DH ** -0.5)
                k = proj_sc[rsl, COL_K + h * MLSTM_DH:COL_K + (h + 1) * MLSTM_DH]
                v = proj_sc[rsl, COL_V + h * MLSTM_DH:COL_V + (h + 1) * MLSTM_DH]
                o = proj_sc[rsl, COL_O + h * MLSTM_DH:COL_O + (h + 1) * MLSTM_DH]
                fcol = MLSTM_HEADS + h
                hh, c_new, n_new, m_new = _mlstm_chunk(
                    q, k, v,
                    ig_all[:, h:h + 1], ig_t[h:h + 1, 0:chunk],
                    bcum[:, fcol:fcol + 1], bcum_t[fcol:fcol + 1, 0:chunk],
                    causal, c_sc[s, h], n_sc[s, pl.ds(h, 1), :], m_sc[s, pl.ds(h, 1), 0:1])
                c_sc[s, h] = c_new
                n_sc[s, pl.ds(h, 1), :] = n_new
                m_sc[s, pl.ds(h, 1), :] = jnp.broadcast_to(m_new, (1, LANES))
                hn = _layer_norm(hh) * mhg[:, hs]
                ycat_sc[rsl, hs] = jax.nn.sigmoid(o) * hn
        return carry

    if nb == 1:
        seq_body(0, 0)
    else:
        lax.fori_loop(0, nb, seq_body, 0)

    u_s = proj_sc[:, COL_U:COL_U + SGU_WIDTH]
    v_s = proj_sc[:, COL_VS:COL_VS + SGU_WIDTH]
    gr = lax.broadcasted_iota(jnp.int32, (SGU_WIDTH, SGU_WIDTH), 0) // SGU_DH
    gc = lax.broadcasted_iota(jnp.int32, (SGU_WIDTH, SGU_WIDTH), 1) // SGU_DH
    avg = jnp.where(gr == gc, 1.0 / SGU_DH, 0.0).astype(F32)
    mu = jnp.dot(v_s, avg, precision=HIGHEST, preferred_element_type=F32)
    dv = v_s - mu
    var = jnp.dot(dv * dv, avg, precision=HIGHEST, preferred_element_type=F32)
    vn = dv * lax.rsqrt(var + LN_EPS) * sgug_ref[...] + sgub_ref[...]
    if vn_ref is not None:
        vn_ref[...] = vn.reshape(nb, rows, SGU_WIDTH)
    sr = lax.broadcasted_iota(jnp.int32, (SGU_CHUNK, SGU_CHUNK), 0)
    scol = lax.broadcasted_iota(jnp.int32, (SGU_CHUNK, SGU_CHUNK), 1)
    smask = (scol <= sr) & ((sr // sgu_blk) == (scol // sgu_blk))
    lane_grp = lax.broadcasted_iota(jnp.int32, (SGU_CHUNK, SGU_WIDTH), 1) // SGU_DH
    vnb = vn.astype(BF16)
    for r in range(m_rows // SGU_CHUNK):
        rs = slice(r * SGU_CHUNK, (r + 1) * SGU_CHUNK)
        mix = jnp.zeros((SGU_CHUNK, SGU_WIDTH), F32)
        for g in range(SGU_HEADS):
            wg = jnp.where(smask, ws_ref[g], 0.0).astype(BF16)
            mg = jnp.dot(wg, vnb[rs], preferred_element_type=F32)
            mix = jnp.where(lane_grp == g, mg, mix)
        ycat_sc[rs, MLSTM_WIDTH:MLSTM_WIDTH + SGU_WIDTH] = u_s[rs] * (mix + bs_ref[...])

    pbuf_sc[:, 0:POOL_HIST, :] = carry_sc[...]
    pbuf_sc[:, POOL_HIST:POOL_HIST + rows, :] = proj_sc[:, COL_P:COL_P + POOL_WIDTH].reshape(nb, rows, POOL_WIDTH)
    lane_w = lax.broadcasted_iota(jnp.int32, (1, 1, POOL_WIDTH), 2) // (POOL_WIDTH // len(POOL_WINDOWS))
    x0 = pbuf_sc[:, POOL_HIST:POOL_HIST + rows, :]
    acc = x0
    wsum = jnp.zeros_like(x0)
    for kk in range(1, max(POOL_WINDOWS)):
        acc = acc + pbuf_sc[:, POOL_HIST - kk:POOL_HIST - kk + rows, :]
        if (kk + 1) in POOL_WINDOWS:
            wsum = jnp.where(lane_w == POOL_WINDOWS.index(kk + 1), acc, wsum)
    t_io = lax.broadcasted_iota(jnp.int32, (1, rows, POOL_WIDTH), 1)
    win = jnp.left_shift(2, lane_w)
    cnt = jnp.minimum(pos0 + ci * rows + t_io + 1, win).astype(F32)
    pooled = wsum / cnt - x0
    y_c = jnp.dot(pooled.reshape(m_rows, POOL_WIDTH).astype(BF16), wpool_ref[...],
                  preferred_element_type=F32) * pscale_ref[...]
    ycat_sc[:, MLSTM_WIDTH + SGU_WIDTH:D_MODEL] = y_c
    carry_sc[...] = pbuf_sc[:, rows:rows + POOL_HIST, :]

    y = jnp.dot(ycat_sc[...].astype(BF16), wo_ref[...], preferred_element_type=F32)
    z = ALPHA * x3 + g1 * y.reshape(nb, rows, D_MODEL)
    x1_ref[...] = _layer_norm(z) * ln1g_ref[...] + ln1b_ref[...]

    @pl.when(ci == n_chunks - 1)
    def _():
        c_out_ref[...] = c_sc[...]
        n_out_ref[...] = n_sc[...]
        m_out_ref[...] = m_sc[...]
        tail_ref[...] = pbuf_sc[:, n_valid + rows - chunk:n_valid + rows - chunk + POOL_HIST, :]


def _mixer(x, mod, c0, n0, m0, hist, lw, *, nb, rows, chunk, n_valid, sgu_blk, pos0, want_vn):
    bsz, seq, _ = x.shape
    grid = (bsz // nb, seq // rows)
    m_rows = nb * rows
    kern = functools.partial(_mixer_kernel, nb=nb, rows=rows, chunk=chunk, n_valid=n_valid,
                             sgu_blk=sgu_blk, pos0=pos0)
    if not want_vn:
        def kern(*refs, _k=kern):
            return _k(*refs[:24], None, *refs[24:])

    def full(shape):
        return pl.BlockSpec(shape, lambda b, c: (0,) * len(shape))

    per_b3 = lambda s1, s2: pl.BlockSpec((nb, s1, s2), lambda b, c: (b, 0, 0))
    in_specs = [
        pl.BlockSpec((nb, rows, D_MODEL), lambda b, c: (b, c, 0)),
        per_b3(SUBLANES, D_MODEL),
        pl.BlockSpec((nb, MLSTM_HEADS, MLSTM_DH, MLSTM_DH), lambda b, c: (b, 0, 0, 0)),
        per_b3(MLSTM_HEADS, MLSTM_DH),
        per_b3(MLSTM_HEADS, LANES),
        per_b3(POOL_HIST, POOL_WIDTH),
        full((D_MODEL, IN_COLS_PADDED)),
        full((D_MODEL, LANES)),
        full((1, LANES)),
        full((1, MLSTM_WIDTH)),
        full((1, SGU_WIDTH)),
        full((1, SGU_WIDTH)),
        full((SGU_HEADS, SGU_CHUNK, SGU_CHUNK)),
        full((SGU_CHUNK, SGU_WIDTH)),
        full((POOL_WIDTH, POOL_WIDTH)),
        full((1, POOL_WIDTH)),
        full((D_MODEL, D_MODEL)),
        full((1, D_MODEL)),
        full((1, D_MODEL)),
    ]
    out_specs = [
        pl.BlockSpec((nb, rows, D_MODEL), lambda b, c: (b, c, 0)),
        pl.BlockSpec((nb, MLSTM_HEADS, MLSTM_DH, MLSTM_DH), lambda b, c: (b, 0, 0, 0)),
        per_b3(MLSTM_HEADS, MLSTM_DH),
        per_b3(MLSTM_HEADS, LANES),
        per_b3(POOL_HIST, POOL_WIDTH),
    ]
    out_shape = [
        jax.ShapeDtypeStruct((bsz, seq, D_MODEL), F32),
        jax.ShapeDtypeStruct((bsz, MLSTM_HEADS, MLSTM_DH, MLSTM_DH), F32),
        jax.ShapeDtypeStruct((bsz, MLSTM_HEADS, MLSTM_DH), F32),
        jax.ShapeDtypeStruct((bsz, MLSTM_HEADS, LANES), F32),
        jax.ShapeDtypeStruct((bsz, POOL_HIST, POOL_WIDTH), F32),
    ]
    if want_vn:
        out_specs.append(pl.BlockSpec((nb, rows, SGU_WIDTH), lambda b, c: (b, c, 0)))
        out_shape.append(jax.ShapeDtypeStruct((bsz, seq, SGU_WIDTH), F32))
    scratch = [
        pltpu.VMEM((m_rows, IN_COLS_PADDED), F32),
        pltpu.VMEM((m_rows, D_MODEL), F32),
        pltpu.VMEM((nb, MLSTM_HEADS, MLSTM_DH, MLSTM_DH), F32),
        pltpu.VMEM((nb, MLSTM_HEADS, MLSTM_DH), F32),
        pltpu.VMEM((nb, MLSTM_HEADS, LANES), F32),
        pltpu.VMEM((nb, POOL_HIST, POOL_WIDTH), F32),
        pltpu.VMEM((nb, POOL_HIST + rows, POOL_WIDTH), F32),
    ]
    return pl.pallas_call(
        kern,
        grid=grid,
        in_specs=in_specs,
        out_specs=out_specs,
        out_shape=out_shape,
        scratch_shapes=scratch,
        compiler_params=pltpu.CompilerParams(
            dimension_semantics=("arbitrary", "arbitrary"), vmem_limit_bytes=VMEM_LIMIT_BYTES),
        name="mixer_sample" if want_vn else "mixer_prompt",
    )(x, mod, c0, n0, m0, hist, lw["w_in"], lw["w_gate"], lw["b_gate"], lw["mh_g"], lw["sgu_g"], lw["sgu_b"],
      lw["w_s_sample"] if want_vn else lw["w_s"], lw["b_s_sample"] if want_vn else lw["b_s"],
      lw["w_pool"], lw["pool_scale"], lw["w_o"], lw["ln1_g"], lw["ln1_b"])


def _sort16_pairs():
    pairs = []

    def merge(lo, hi, r):
        step = r * 2
        if step < hi - lo:
            merge(lo, hi, step)
            merge(lo + r, hi, step)
            for i in range(lo + r, hi - r, step):
                pairs.append((i, i + r))
        else:
            pairs.append((lo, lo + r))

    def sort(lo, hi):
        if hi - lo >= 1:
            mid = lo + (hi - lo) // 2
            sort(lo, mid)
            sort(mid + 1, hi)
            merge(lo, hi, 1)

    sort(0, PEER_TOPK - 1)
    return tuple(pairs)


SORT16_PAIRS = _sort16_pairs()


def _sort16_desc(vals):
    vals = list(vals)
    for i, j in SORT16_PAIRS:
        hi = jnp.maximum(vals[i], vals[j])
        lo = jnp.minimum(vals[i], vals[j])
        vals[i], vals[j] = hi, lo
    return vals


def _merge_top16(xs, ys):
    vals = [jnp.maximum(xs[i], ys[PEER_TOPK - 1 - i]) for i in range(PEER_TOPK)]
    d = PEER_TOPK // 2
    while d >= 1:
        for i in range(PEER_TOPK):
            if (i & d) == 0:
                hi = jnp.maximum(vals[i], vals[i + d])
                lo = jnp.minimum(vals[i], vals[i + d])
                vals[i], vals[i + d] = hi, lo
        d //= 2
    return vals


def _top16_desc(tiles):
    tiles = list(tiles)
    neg = jnp.full_like(tiles[0], -jnp.inf)
    while len(tiles) % PEER_TOPK:
        tiles.append(neg)
    best = None
    for g in range(len(tiles) // PEER_TOPK):
        grp = _sort16_desc(tiles[g * PEER_TOPK:(g + 1) * PEER_TOPK])
        best = grp if best is None else _merge_top16(best, grp)
    return best


S_PITCH = PEER_NKEYS + SUBLANES


def _peer_select_kernel(x_ref, sh_ref, sc_ref, wq_ref, keys_ref,
                        s1_ref, c1_ref, s2_ref, e2_ref, tau_ref,
                        q_sc, sa_sc, sb_sc, *, tn):
    s_sc = (sa_sc, sb_sc)
    x = x_ref[...]
    sh = sh_ref[...].reshape(-1, D_MODEL)
    sc = sc_ref[...].reshape(-1, D_MODEL)
    h2 = (x * (1.0 + sc) + sh).astype(BF16)
    q_sc[...] = lax.dot_general(wq_ref[...], h2, NT_DIMS, preferred_element_type=F32).astype(BF16)
    for h in range(PEER_HEADS):
        for p in range(2):
            r0 = (h * 2 + p) * PEER_DK
            s_hp = jnp.dot(keys_ref[h * 2 + p], q_sc[r0:r0 + PEER_DK, :], preferred_element_type=F32)
            for lg in range(tn // LANES):
                s_sc[p][lg, h * S_PITCH:h * S_PITCH + PEER_NKEYS, :] = s_hp[:, lg * LANES:(lg + 1) * LANES]
    for lg in range(tn // LANES):
        lanes = pl.ds(lg * LANES, LANES)
        tops = []
        for p in range(2):
            tiles = [s_sc[p][lg, pl.ds(i, PEER_HEADS, stride=S_PITCH), :] for i in range(PEER_NKEYS)]
            tops.append(_top16_desc(tiles))
        a, b = tops
        cands = []
        for r1 in range(PEER_TOPK):
            for r2 in range(PEER_TOPK // (r1 + 1)):
                cands.append(b[r2] + a[r1])
        c = _top16_desc(cands)
        tau = c[PEER_TOPK - 1]
        z = jnp.zeros_like(tau)
        for kk in range(PEER_TOPK):
            z = z + jnp.exp(c[kk] - c[0])
        zinv = 1.0 / z
        tau_ref[lg] = tau
        for h in range(PEER_HEADS):
            rows_in = slice(h * S_PITCH, h * S_PITCH + PEER_NKEYS)
            rows_out = slice(h * PEER_NKEYS, (h + 1) * PEER_NKEYS)
            s1 = sa_sc[lg, rows_in, :]
            s2 = sb_sc[lg, rows_in, :]
            s1_ref[lg, rows_out, :] = s1
            s2_ref[lg, rows_out, :] = s2
            c1_ref[lg, rows_out, :] = jnp.exp(s1 - a[0][h:h + 1, :]) * zinv[h:h + 1, :]
            e2_ref[lg, rows_out, :] = jnp.exp(s2 - b[0][h:h + 1, :])


def _mod_spec(arr, tn, tiles_per_row):
    if arr.ndim == 3:
        return pl.BlockSpec((1, 1, D_MODEL), lambda i, *_: (i // tiles_per_row, 0, 0))
    return pl.BlockSpec((tn, D_MODEL), lambda i, *_: (i, 0))


def _peer_select(x1, sh2, sc2, lw, *, tn, tiles_per_row):
    n_tok = x1.shape[0]
    assert n_tok % tn == 0
    sel_rows = PEER_HEADS * PEER_NKEYS
    n_lg = tn // LANES
    big = lambda: pl.BlockSpec((n_lg, sel_rows, LANES), lambda i: (i, 0, 0))
    return pl.pallas_call(
        functools.partial(_peer_select_kernel, tn=tn),
        grid=(n_tok // tn,),
        in_specs=[
            pl.BlockSpec((tn, D_MODEL), lambda i: (i, 0)),
            _mod_spec(sh2, tn, tiles_per_row),
            _mod_spec(sc2, tn, tiles_per_row),
            pl.BlockSpec((2 * PEER_HEADS * PEER_DK, D_MODEL), lambda i: (0, 0)),
            pl.BlockSpec((2 * PEER_HEADS, PEER_NKEYS, PEER_DK), lambda i: (0, 0, 0)),
        ],
        out_specs=[big(), big(), big(), big(), pl.BlockSpec((n_lg, PEER_HEADS, LANES), lambda i: (i, 0, 0))],
        out_shape=[jax.ShapeDtypeStruct((n_tok // LANES, sel_rows, LANES), F32)] * 4
        + [jax.ShapeDtypeStruct((n_tok // LANES, PEER_HEADS, LANES), F32)],
        scratch_shapes=[
            pltpu.VMEM((2 * PEER_HEADS * PEER_DK, tn), BF16),
            pltpu.VMEM((tn // LANES, PEER_HEADS * S_PITCH, LANES), F32),
            pltpu.VMEM((tn // LANES, PEER_HEADS * S_PITCH, LANES), F32),
        ],
        compiler_params=pltpu.CompilerParams(
            dimension_semantics=("arbitrary",), vmem_limit_bytes=VMEM_LIMIT_BYTES),
        name="peer_select",
    )(x1, sh2, sc2, lw["w_pq_t"], lw["peer_keys"])


GATE_SUB = 64
PEER_STAGE = 2 * PEER_NKEYS


def _peer_dense_kernel(x_ref, sh_ref, sc_ref, g_ref, s1_ref, c1_ref, s2_ref, e2_ref, tau_ref,
                       u_ref, vt_ref, lng_ref, lnb_ref, o_ref,
                       xb_sc, acc_sc, at_a, at_b, coef_sc, *, tn, te):
    j = pl.program_id(1)
    n_j = pl.num_programs(1) - 1
    stage = at_a.shape[0]
    n_st = te // stage
    out_chunk = D_MODEL // n_st
    assert n_st % 2 == 0
    par = lax.rem(j, 2)

    @pl.when(j == 0)
    def _():
        sh = sh_ref[...].reshape(-1, D_MODEL)
        sc = sc_ref[...].reshape(-1, D_MODEL)
        xb_sc[...] = (x_ref[...] * (1.0 + sc) + sh).T.astype(BF16)
        acc_sc[...] = jnp.zeros_like(acc_sc)
        coef_sc[1] = jnp.zeros(coef_sc.shape[1:], BF16)

    def scores(st):
        rows = pl.ds(pl.multiple_of(st * stage, stage), stage)
        return jnp.dot(u_ref[rows, :], xb_sc[...], preferred_element_type=F32)

    def values_chunk(st):
        rows = pl.ds(pl.multiple_of(st * out_chunk, out_chunk), out_chunk)
        acc_sc[rows, :] += jnp.dot(vt_ref[rows, :], coef_sc[1 - par], preferred_element_type=F32)

    def gate_stage(st, at_ref):
        for qq in range(stage // PEER_NKEYS):
            key1 = (j * n_st + st) * (stage // PEER_NKEYS) + qq
            for lg in range(tn // LANES):
                lanes = slice(lg * LANES, (lg + 1) * LANES)
                for sub in range(PEER_NKEYS // GATE_SUB):
                    gate = jnp.zeros((GATE_SUB // SUBLANES, SUBLANES, LANES), F32)
                    for h in range(PEER_HEADS):
                        hrows = slice(h * PEER_NKEYS + sub * GATE_SUB, h * PEER_NKEYS + (sub + 1) * GATE_SUB)
                        s1b = s1_ref[lg, pl.ds(h * PEER_NKEYS + key1, SUBLANES, stride=0), :]
                        c1b = c1_ref[lg, pl.ds(h * PEER_NKEYS + key1, SUBLANES, stride=0), :]
                        taub = tau_ref[lg, pl.ds(h, SUBLANES, stride=0), :]
                        s2 = s2_ref[lg, hrows, :].reshape(gate.shape)
                        e2 = e2_ref[lg, hrows, :].reshape(gate.shape)
                        gate = gate + jnp.where((s2 + s1b) >= taub, e2 * c1b, 0.0)
                    r0 = qq * PEER_NKEYS + sub * GATE_SUB
                    a = at_ref[r0:r0 + GATE_SUB, lanes]
                    act = 0.5 * a * (1.0 + lax.erf(a * (2.0 ** -0.5)))
                    out_rows = pl.ds(pl.multiple_of(st * stage + r0, GATE_SUB), GATE_SUB)
                    coef_sc[par, out_rows, lanes] = (gate.reshape(GATE_SUB, LANES) * act).astype(BF16)

    @pl.when(j < n_j)
    def _():
        at_a[...] = scores(0)

        def stage_pair(k, carry):
            for half, (at_cur, at_nxt) in enumerate(((at_a, at_b), (at_b, at_a))):
                st = 2 * k + half
                at_nxt[...] = scores(jnp.minimum(st + 1, n_st - 1))
                values_chunk(st)
                gate_stage(st, at_cur)
            return carry

        lax.fori_loop(0, n_st // 2, stage_pair, 0)

    @pl.when(j == n_j)
    def _():
        for st in range(n_st):
            values_chunk(st)
        g2 = g_ref[...].reshape(-1, D_MODEL)
        z = ALPHA * x_ref[...] + g2 * acc_sc[...].T
        o_ref[...] = _layer_norm(z) * lng_ref[...] + lnb_ref[...]


def _peer_dense(x1, sh2, sc2, g2, sel, lw, *, tn, te, tiles_per_row):
    n_tok = x1.shape[0]
    assert n_tok % tn == 0
    sel_rows = PEER_HEADS * PEER_NKEYS
    n_lg = tn // LANES
    s1, c1, s2, e2, tau = sel
    big = lambda: pl.BlockSpec((n_lg, sel_rows, LANES), lambda i, j: (i, 0, 0))
    n_j = PEER_EXPERTS // te
    return pl.pallas_call(
        functools.partial(_peer_dense_kernel, tn=tn, te=te),
        grid=(n_tok // tn, n_j + 1),
        in_specs=[
            pl.BlockSpec((tn, D_MODEL), lambda i, j: (i, 0)),
            _mod_spec(sh2, tn, tiles_per_row),
            _mod_spec(sc2, tn, tiles_per_row),
            _mod_spec(g2, tn, tiles_per_row),
            big(), big(), big(), big(),
            pl.BlockSpec((n_lg, PEER_HEADS, LANES), lambda i, j: (i, 0, 0)),
            pl.BlockSpec((te, D_MODEL), lambda i, j: (jnp.minimum(j, n_j - 1), 0)),
            pl.BlockSpec((D_MODEL, te), lambda i, j: (0, jnp.maximum(j - 1, 0))),
            pl.BlockSpec((1, D_MODEL), lambda i, j: (0, 0)),
            pl.BlockSpec((1, D_MODEL), lambda i, j: (0, 0)),
        ],
        out_specs=pl.BlockSpec((tn, D_MODEL), lambda i, j: (i, 0)),
        out_shape=jax.ShapeDtypeStruct((n_tok, D_MODEL), F32),
        scratch_shapes=[
            pltpu.VMEM((D_MODEL, tn), BF16),
            pltpu.VMEM((D_MODEL, tn), F32),
            pltpu.VMEM((PEER_STAGE, tn), F32),
            pltpu.VMEM((PEER_STAGE, tn), F32),
            pltpu.VMEM((2, te, tn), BF16),
        ],
        compiler_params=pltpu.CompilerParams(
            dimension_semantics=("arbitrary", "arbitrary"), vmem_limit_bytes=VMEM_LIMIT_BYTES),
        name="peer_dense",
    )(x1, sh2, sc2, g2, s1, c1, s2, e2, tau, lw["peer_u"], lw["peer_v_t"], lw["ln2_g"], lw["ln2_b"])


def _prep_layer(l, w_in, b_gate, mh_g, sgu_g, sgu_b, w_s, b_s, w_pool, pool_scale, w_o, ln1_g, ln1_b,
                w_pq, peer_keys, peer_u, peer_v, ln2_g, ln2_b, sample_rows):
    n_gate = 2 * MLSTM_HEADS
    g0 = 4 * MLSTM_WIDTH
    w = w_in[l]
    w_gate = jnp.pad(w[:, g0:g0 + n_gate], ((0, 0), (0, LANES - n_gate)))
    w_in_p = jnp.concatenate([w[:, :g0], w_gate, w[:, g0 + n_gate:]], axis=1).astype(BF16)
    bs_full = jnp.repeat(jnp.swapaxes(b_s[l], 0, 1), SGU_DH, axis=1)
    reps = SGU_CHUNK // sample_rows
    eye = jnp.eye(len(POOL_WINDOWS), dtype=F32)
    dg = POOL_WIDTH // len(POOL_WINDOWS)
    w_pool_bd = (eye[:, None, :, None] * w_pool[l][:, :, None, :]).reshape(POOL_WIDTH, POOL_WIDTH)
    row = lambda a: a.reshape(1, -1)
    return {
        "w_in": w_in_p,
        "w_gate": w_gate,
        "b_gate": jnp.pad(b_gate[l], (0, LANES - n_gate)).reshape(1, LANES),
        "mh_g": row(mh_g[l]), "sgu_g": row(sgu_g[l]), "sgu_b": row(sgu_b[l]),
        "w_s": w_s[l],
        "b_s": bs_full,
        "w_s_sample": jnp.tile(w_s[l][:, :sample_rows, :sample_rows], (1, reps, reps)),
        "b_s_sample": jnp.tile(bs_full[:sample_rows], (reps, 1)),
        "w_pool": w_pool_bd.astype(BF16),
        "pool_scale": row(pool_scale[l]),
        "w_o": w_o[l].astype(BF16),
        "ln1_g": row(ln1_g[l]), "ln1_b": row(ln1_b[l]),
        "w_pq_t": w_pq[l].T.astype(BF16),
        "peer_keys": peer_keys[l].reshape(2 * PEER_HEADS, PEER_NKEYS, PEER_DK).astype(BF16),
        "peer_u": peer_u[l].astype(BF16),
        "peer_v_t": peer_v[l].T.astype(BF16),
        "ln2_g": row(ln2_g[l]), "ln2_b": row(ln2_b[l]),
    }


def kernel(x_prompt, x_sample, state_mlstm_C, state_mlstm_n, state_mlstm_m, state_pool, c_prompt, c_sample,
           w_ada, b_ada, w_in, b_gate, mh_g, sgu_g, sgu_b, w_s, b_s, w_pool, pool_scale, w_o, ln1_g, ln1_b,
           w_pq, peer_keys, peer_u, peer_v, ln2_g, ln2_b):
    bp, seq, _ = x_prompt.shape
    bs, dec_seq, _ = x_sample.shape
    sample_rows = SUBLANES
    prompt_rows = min(seq, 256)
    peer_tn = 512
    peer_te = 1024
    sel_tn = 256

    ada = _ada(jnp.concatenate([c_prompt, c_sample], axis=0), w_ada, b_ada)

    xp = x_prompt
    xs = jnp.pad(x_sample, ((0, 0), (0, sample_rows - dec_seq), (0, 0)))
    zeros_c = jnp.zeros((bp, MLSTM_HEADS, MLSTM_DH, MLSTM_DH), F32)
    zeros_n = jnp.zeros((bp, MLSTM_HEADS, MLSTM_DH), F32)
    zeros_m = jnp.zeros((bp, MLSTM_HEADS, LANES), F32)
    zeros_hist = jnp.zeros((bp, POOL_HIST, POOL_WIDTH), F32)
    outs = [[] for _ in range(9)]
    for l in range(DEPTH):
        lw = _prep_layer(l, w_in, b_gate, mh_g, sgu_g, sgu_b, w_s, b_s, w_pool, pool_scale, w_o, ln1_g, ln1_b,
                         w_pq, peer_keys, peer_u, peer_v, ln2_g, ln2_b, sample_rows)
        mods = jnp.pad(ada[l].reshape(bp + bs, N_ADA, D_MODEL), ((0, 0), (0, SUBLANES - N_ADA), (0, 0)))
        mod_p, mod_s = mods[:bp], mods[bp:]

        x1p, cp, np_, mp, tailp = _mixer(
            xp, mod_p, zeros_c, zeros_n, zeros_m, zeros_hist, lw,
            nb=1, rows=prompt_rows, chunk=SGU_CHUNK, n_valid=SGU_CHUNK, sgu_blk=SGU_CHUNK, pos0=0,
            want_vn=False)
        m_in = jnp.broadcast_to(state_mlstm_m[l][:, :, None], (bs, MLSTM_HEADS, LANES))
        hist_s = jnp.pad(state_pool[l], ((0, 0), (1, 0), (0, 0)))
        x1s, cs, ns, ms, tails, vns = _mixer(
            xs, mod_s, state_mlstm_C[l], state_mlstm_n[l], m_in, hist_s, lw,
            nb=16, rows=sample_rows, chunk=sample_rows, n_valid=dec_seq, sgu_blk=sample_rows, pos0=PAST_LEN,
            want_vn=True)

        x1p2 = x1p.reshape(bp * seq, D_MODEL)
        shp, scp, gp = mod_p[:, 3:4], mod_p[:, 4:5], mod_p[:, 5:6]
        sel = _peer_select(x1p2, shp, scp, lw, tn=sel_tn, tiles_per_row=seq // sel_tn)
        xp = _peer_dense(x1p2, shp, scp, gp, sel, lw, tn=peer_tn, te=peer_te,
                         tiles_per_row=seq // peer_tn).reshape(bp, seq, D_MODEL)
        x1s2 = x1s[:, :dec_seq].reshape(bs * dec_seq, D_MODEL)
        shs, scs, gs = (jnp.repeat(mod_s[:, i], dec_seq, axis=0) for i in (3, 4, 5))
        n_s = bs * dec_seq
        sel = _peer_select(x1s2, shs, scs, lw, tn=min(sel_tn, n_s), tiles_per_row=1)
        xs_new = _peer_dense(x1s2, shs, scs, gs, sel, lw, tn=min(peer_tn, n_s), te=peer_te, tiles_per_row=1)
        xs_new = xs_new.reshape(bs, dec_seq, D_MODEL)
        xs = jnp.pad(xs_new, ((0, 0), (0, sample_rows - dec_seq), (0, 0)))

        for lst, val in zip(outs, (cp, np_, mp[:, :, 0], tailp[:, 1:], cs, ns, ms[:, :, 0], tails[:, 1:],
                                   vns[:, :dec_seq])):
            lst.append(val)
    return (xp, xs_new) + tuple(jnp.stack(o) for o in outs)
```

```python
import functools

import jax
import jax.numpy as jnp
import numpy as np
from jax import lax
from jax.experimental import pallas as pl
from jax.experimental.pallas import tpu as pltpu

F32 = jnp.float32
BF16 = jnp.bfloat16
HIGHEST = lax.Precision.HIGHEST

D_MODEL = 1024
DEPTH = 2
N_ADA = 6
MLSTM_HEADS = 4
MLSTM_DH = 128
MLSTM_WIDTH = MLSTM_HEADS * MLSTM_DH
SGU_WIDTH = 256
SGU_HEADS = 4
SGU_DH = SGU_WIDTH // SGU_HEADS
SGU_CHUNK = 128
POOL_WIDTH = 256
POOL_WINDOWS = (2, 4, 8, 16)
POOL_HIST = 16
PEER_HEADS = 8
PEER_NKEYS = 128
PEER_TOPK = 16
PEER_DK = 128
PEER_EXPERTS = PEER_NKEYS * PEER_NKEYS
PAST_LEN = 16384
ALPHA = (2 * DEPTH) ** 0.25
LN_EPS = 1e-5

LANES = 128
SUBLANES = 8
VMEM_LIMIT_BYTES = 56 * 1024 * 1024

COL_Q = 0
COL_K = MLSTM_WIDTH
COL_V = 2 * MLSTM_WIDTH
COL_O = 3 * MLSTM_WIDTH
COL_GATE = 4 * MLSTM_WIDTH
COL_U = COL_GATE + LANES
COL_VS = COL_U + SGU_WIDTH
COL_P = COL_VS + SGU_WIDTH
IN_COLS_PADDED = COL_P + POOL_WIDTH

NT_DIMS = (((1,), (1,)), ((), ()))
TN_DIMS = (((0,), (0,)), ((), ()))


def _layer_norm(x):
    mu = jnp.mean(x, axis=-1, keepdims=True)
    d = x - mu
    var = jnp.mean(d * d, axis=-1, keepdims=True)
    return d * lax.rsqrt(var + LN_EPS)


def _ada_kernel(c_ref, w_ref, b_ref, o_ref):
    c = c_ref[...]
    s = c * jax.nn.sigmoid(c)
    o_ref[0] = jnp.dot(s, w_ref[0], precision=HIGHEST, preferred_element_type=F32) + b_ref[0]


def _ada(c_all, w_ada, b_ada):
    rows = c_all.shape[0]
    cols = w_ada.shape[-1]
    tile = 1536
    return pl.pallas_call(
        _ada_kernel,
        grid=(DEPTH, cols // tile),
        in_specs=[
            pl.BlockSpec((rows, D_MODEL), lambda l, j: (0, 0)),
            pl.BlockSpec((1, D_MODEL, tile), lambda l, j: (l, 0, j)),
            pl.BlockSpec((1, 1, tile), lambda l, j: (l, 0, j)),
        ],
        out_specs=pl.BlockSpec((1, rows, tile), lambda l, j: (l, 0, j)),
        out_shape=jax.ShapeDtypeStruct((DEPTH, rows, cols), F32),
        compiler_params=pltpu.CompilerParams(vmem_limit_bytes=VMEM_LIMIT_BYTES),
        name="ada",
    )(c_all, w_ada, b_ada.reshape(DEPTH, 1, cols))


def _mlstm_chunk(q, k, v, ig_col, ig_row, b_col, b_row, causal, c_state, n_state, m_state):
    length = q.shape[0]
    dlog = jnp.where(causal, b_col - b_row + ig_row, -jnp.inf)
    inter = b_col + m_state
    m_t = jnp.maximum(inter, jnp.max(dlog, axis=1, keepdims=True))
    w_intra = jnp.exp(dlog - m_t)
    w_inter = jnp.exp(inter - m_t)
    qb = q.astype(BF16)
    kb = k.astype(BF16)
    scores = lax.dot_general(qb, kb, NT_DIMS, preferred_element_type=F32)
    a = w_intra * scores
    num = jnp.dot(a.astype(BF16), v.astype(BF16), preferred_element_type=F32)
    num = num + w_inter * lax.dot_general(qb, c_state.astype(BF16), NT_DIMS, preferred_element_type=F32)
    den = jnp.sum(a, axis=1, keepdims=True) + w_inter * jnp.sum(q * n_state, axis=1, keepdims=True)
    h = num / jnp.maximum(jnp.abs(den), jnp.exp(-m_t))
    b_end = b_col[length - 1:length, :]
    dend = b_end - b_col + ig_col
    m_new = jnp.maximum(b_end + m_state, jnp.max(dend, axis=0, keepdims=True))
    wc = jnp.exp(dend - m_new)
    dec = jnp.exp(b_end + m_state - m_new)
    vw = (v * wc).astype(BF16)
    c_new = dec * c_state + lax.dot_general(vw, kb, TN_DIMS, preferred_element_type=F32)
    n_new = dec * n_state + jnp.sum(k * wc, axis=0, keepdims=True)
    return h, c_new, n_new, m_new


def _mixer_kernel(x_ref, mod_ref, c_in_ref, n_in_ref, m_in_ref, hist_ref,
                  w_in_ref, wgate_ref, bgate_ref, mhg_ref, sgug_ref, sgub_ref, ws_ref, bs_ref,
                  wpool_ref, pscale_ref, wo_ref, ln1g_ref, ln1b_ref,
                  x1_ref, c_out_ref, n_out_ref, m_out_ref, tail_ref, vn_ref,
                  proj_sc, ycat_sc, c_sc, n_sc, m_sc, carry_sc, pbuf_sc,
                  *, nb, rows, chunk, n_valid, sgu_blk, pos0):
    ci = pl.program_id(1)
    n_chunks = pl.num_programs(1)
    m_rows = nb * rows

    @pl.when(ci == 0)
    def _():
        c_sc[...] = c_in_ref[...]
        n_sc[...] = n_in_ref[...]
        m_sc[...] = m_in_ref[...]
        carry_sc[...] = hist_ref[...]

    x3 = x_ref[...]
    mod = mod_ref[...]
    sh1, sc1, g1 = mod[:, 0:1, :], mod[:, 1:2, :], mod[:, 2:3, :]
    h3 = x3 * (1.0 + sc1) + sh1
    h2d = h3.reshape(m_rows, D_MODEL)
    proj_sc[...] = jnp.dot(h2d.astype(BF16), w_in_ref[...], preferred_element_type=F32)
    proj_sc[:, COL_GATE:COL_GATE + LANES] = jnp.dot(h2d, wgate_ref[...], precision=HIGHEST,
                                                    preferred_element_type=F32)

    r_io = lax.broadcasted_iota(jnp.int32, (chunk, chunk), 0)
    c_io = lax.broadcasted_iota(jnp.int32, (chunk, chunk), 1)
    causal = c_io <= r_io
    tri = jnp.where(causal, 1.0, 0.0).astype(F32)
    row_id = lax.broadcasted_iota(jnp.int32, (chunk, 1), 0)
    valid_col = row_id < n_valid
    bgate = bgate_ref[...]
    mhg = mhg_ref[...]

    def seq_body(s, carry):
        for j in range(rows // chunk):
            row0 = s * rows + j * chunk
            if not isinstance(row0, int):
                row0 = pl.multiple_of(row0, SUBLANES)
            rsl = pl.ds(row0, chunk)
            gates = proj_sc[rsl, COL_GATE:COL_GATE + LANES] + bgate
            lf = jnp.where(valid_col, jax.nn.log_sigmoid(gates), 0.0)
            bcum = jnp.dot(tri, lf, precision=HIGHEST, preferred_element_type=F32)
            ig_all = jnp.where(valid_col, gates, -jnp.inf)
            bcum_t = bcum.T
            ig_t = ig_all.T
            for h in range(MLSTM_HEADS):
                hs = slice(h * MLSTM_DH, (h + 1) * MLSTM_DH)
                q = proj_sc[rsl, COL_Q + h * MLSTM_DH:COL_Q + (h + 1) * MLSTM_DH] * (MLSTM_DH ** -0.5)
                k = proj_sc[rsl, COL_K + h * MLSTM_DH:COL_K + (h + 1) * MLSTM_DH]
                v = proj_sc[rsl, COL_V + h * MLSTM_DH:COL_V + (h + 1) * MLSTM_DH]
                o = proj_sc[rsl, COL_O + h * MLSTM_DH:COL_O + (h + 1) * MLSTM_DH]
                fcol = MLSTM_HEADS + h
                hh, c_new, n_new, m_new = _mlstm_chunk(
                    q, k, v,
                    ig_all[:, h:h + 1], ig_t[h:h + 1, 0:chunk],
                    bcum[:, fcol:fcol + 1], bcum_t[fcol:fcol + 1, 0:chunk],
                    causal, c_sc[s, h], n_sc[s, pl.ds(h, 1), :], m_sc[s, pl.ds(h, 1), 0:1])
                c_sc[s, h] = c_new
                n_sc[s, pl.ds(h, 1), :] = n_new
                m_sc[s, pl.ds(h, 1), :] = jnp.broadcast_to(m_new, (1, LANES))
                hn = _layer_norm(hh) * mhg[:, hs]
                ycat_sc[rsl, hs] = jax.nn.sigmoid(o) * hn
        return carry

    if nb == 1:
        seq_body(0, 0)
    else:
        lax.fori_loop(0, nb, seq_body, 0)

    u_s = proj_sc[:, COL_U:COL_U + SGU_WIDTH]
    v_s = proj_sc[:, COL_VS:COL_VS + SGU_WIDTH]
    gr = lax.broadcasted_iota(jnp.int32, (SGU_WIDTH, SGU_WIDTH), 0) // SGU_DH
    gc = lax.broadcasted_iota(jnp.int32, (SGU_WIDTH, SGU_WIDTH), 1) // SGU_DH
    avg = jnp.where(gr == gc, 1.0 / SGU_DH, 0.0).astype(F32)
    mu = jnp.dot(v_s, avg, precision=HIGHEST, preferred_element_type=F32)
    dv = v_s - mu
    var = jnp.dot(dv * dv, avg, precision=HIGHEST, preferred_element_type=F32)
    vn = dv * lax.rsqrt(var + LN_EPS) * sgug_ref[...] + sgub_ref[...]
    if vn_ref is not None:
        vn_ref[...] = vn.reshape(nb, rows, SGU_WIDTH)
    sr = lax.broadcasted_iota(jnp.int32, (SGU_CHUNK, SGU_CHUNK), 0)
    scol = lax.broadcasted_iota(jnp.int32, (SGU_CHUNK, SGU_CHUNK), 1)
    smask = (scol <= sr) & ((sr // sgu_blk) == (scol // sgu_blk))
    lane_grp = lax.broadcasted_iota(jnp.int32, (SGU_CHUNK, SGU_WIDTH), 1) // SGU_DH
    vnb = vn.astype(BF16)
    for r in range(m_rows // SGU_CHUNK):
        rs = slice(r * SGU_CHUNK, (r + 1) * SGU_CHUNK)
        mix = jnp.zeros((SGU_CHUNK, SGU_WIDTH), F32)
        for g in range(SGU_HEADS):
            wg = jnp.where(smask, ws_ref[g], 0.0).astype(BF16)
            mg = jnp.dot(wg, vnb[rs], preferred_element_type=F32)
            mix = jnp.where(lane_grp == g, mg, mix)
        ycat_sc[rs, MLSTM_WIDTH:MLSTM_WIDTH + SGU_WIDTH] = u_s[rs] * (mix + bs_ref[...])

    pbuf_sc[:, 0:POOL_HIST, :] = carry_sc[...]
    pbuf_sc[:, POOL_HIST:POOL_HIST + rows, :] = proj_sc[:, COL_P:COL_P + POOL_WIDTH].reshape(nb, rows, POOL_WIDTH)
    lane_w = lax.broadcasted_iota(jnp.int32, (1, 1, POOL_WIDTH), 2) // (POOL_WIDTH // len(POOL_WINDOWS))
    x0 = pbuf_sc[:, POOL_HIST:POOL_HIST + rows, :]
    acc = x0
    wsum = jnp.zeros_like(x0)
    for kk in range(1, max(POOL_WINDOWS)):
        acc = acc + pbuf_sc[:, POOL_HIST - kk:POOL_HIST - kk + rows, :]
        if (kk + 1) in POOL_WINDOWS:
            wsum = jnp.where(lane_w == POOL_WINDOWS.index(kk + 1), acc, wsum)
    t_io = lax.broadcasted_iota(jnp.int32, (1, rows, POOL_WIDTH), 1)
    win = jnp.left_shift(2, lane_w)
    cnt = jnp.minimum(pos0 + ci * rows + t_io + 1, win).astype(F32)
    pooled = wsum / cnt - x0
    y_c = jnp.dot(pooled.reshape(m_rows, POOL_WIDTH).astype(BF16), wpool_ref[...],
                  preferred_element_type=F32) * pscale_ref[...]
    ycat_sc[:, MLSTM_WIDTH + SGU_WIDTH:D_MODEL] = y_c
    carry_sc[...] = pbuf_sc[:, rows:rows + POOL_HIST, :]

    y = jnp.dot(ycat_sc[...].astype(BF16), wo_ref[...], preferred_element_type=F32)
    z = ALPHA * x3 + g1 * y.reshape(nb, rows, D_MODEL)
    x1_ref[...] = _layer_norm(z) * ln1g_ref[...] + ln1b_ref[...]

    @pl.when(ci == n_chunks - 1)
    def _():
        c_out_ref[...] = c_sc[...]
        n_out_ref[...] = n_sc[...]
        m_out_ref[...] = m_sc[...]
        tail_ref[...] = pbuf_sc[:, n_valid + rows - chunk:n_valid + rows - chunk + POOL_HIST, :]


def _mixer(x, mod, c0, n0, m0, hist, lw, *, nb, rows, chunk, n_valid, sgu_blk, pos0, want_vn):
    bsz, seq, _ = x.shape
    grid = (bsz // nb, seq // rows)
    m_rows = nb * rows
    kern = functools.partial(_mixer_kernel, nb=nb, rows=rows, chunk=chunk, n_valid=n_valid,
                             sgu_blk=sgu_blk, pos0=pos0)
    if not want_vn:
        def kern(*refs, _k=kern):
            return _k(*refs[:24], None, *refs[24:])

    def full(shape):
        return pl.BlockSpec(shape, lambda b, c: (0,) * len(shape))

    per_b3 = lambda s1, s2: pl.BlockSpec((nb, s1, s2), lambda b, c: (b, 0, 0))
    in_specs = [
        pl.BlockSpec((nb, rows, D_MODEL), lambda b, c: (b, c, 0)),
        per_b3(SUBLANES, D_MODEL),
        pl.BlockSpec((nb, MLSTM_HEADS, MLSTM_DH, MLSTM_DH), lambda b, c: (b, 0, 0, 0)),
        per_b3(MLSTM_HEADS, MLSTM_DH),
        per_b3(MLSTM_HEADS, LANES),
        per_b3(POOL_HIST, POOL_WIDTH),
        full((D_MODEL, IN_COLS_PADDED)),
        full((D_MODEL, LANES)),
        full((1, LANES)),
        full((1, MLSTM_WIDTH)),
        full((1, SGU_WIDTH)),
        full((1, SGU_WIDTH)),
        full((SGU_HEADS, SGU_CHUNK, SGU_CHUNK)),
        full((SGU_CHUNK, SGU_WIDTH)),
        full((POOL_WIDTH, POOL_WIDTH)),
        full((1, POOL_WIDTH)),
        full((D_MODEL, D_MODEL)),
        full((1, D_MODEL)),
        full((1, D_MODEL)),
    ]
    out_specs = [
        pl.BlockSpec((nb, rows, D_MODEL), lambda b, c: (b, c, 0)),
        pl.BlockSpec((nb, MLSTM_HEADS, MLSTM_DH, MLSTM_DH), lambda b, c: (b, 0, 0, 0)),
        per_b3(MLSTM_HEADS, MLSTM_DH),
        per_b3(MLSTM_HEADS, LANES),
        per_b3(POOL_HIST, POOL_WIDTH),
    ]
    out_shape = [
        jax.ShapeDtypeStruct((bsz, seq, D_MODEL), F32),
        jax.ShapeDtypeStruct((bsz, MLSTM_HEADS, MLSTM_DH, MLSTM_DH), F32),
        jax.ShapeDtypeStruct((bsz, MLSTM_HEADS, MLSTM_DH), F32),
        jax.ShapeDtypeStruct((bsz, MLSTM_HEADS, LANES), F32),
        jax.ShapeDtypeStruct((bsz, POOL_HIST, POOL_WIDTH), F32),
    ]
    if want_vn:
        out_specs.append(pl.BlockSpec((nb, rows, SGU_WIDTH), lambda b, c: (b, c, 0)))
        out_shape.append(jax.ShapeDtypeStruct((bsz, seq, SGU_WIDTH), F32))
    scratch = [
        pltpu.VMEM((m_rows, IN_COLS_PADDED), F32),
        pltpu.VMEM((m_rows, D_MODEL), F32),
        pltpu.VMEM((nb, MLSTM_HEADS, MLSTM_DH, MLSTM_DH), F32),
        pltpu.VMEM((nb, MLSTM_HEADS, MLSTM_DH), F32),
        pltpu.VMEM((nb, MLSTM_HEADS, LANES), F32),
        pltpu.VMEM((nb, POOL_HIST, POOL_WIDTH), F32),
        pltpu.VMEM((nb, POOL_HIST + rows, POOL_WIDTH), F32),
    ]
    return pl.pallas_call(
        kern,
        grid=grid,
        in_specs=in_specs,
        out_specs=out_specs,
        out_shape=out_shape,
        scratch_shapes=scratch,
        compiler_params=pltpu.CompilerParams(
            dimension_semantics=("arbitrary", "arbitrary"), vmem_limit_bytes=VMEM_LIMIT_BYTES),
        name="mixer_sample" if want_vn else "mixer_prompt",
    )(x, mod, c0, n0, m0, hist, lw["w_in"], lw["w_gate"], lw["b_gate"], lw["mh_g"], lw["sgu_g"], lw["sgu_b"],
      lw["w_s_sample"] if want_vn else lw["w_s"], lw["b_s_sample"] if want_vn else lw["b_s"],
      lw["w_pool"], lw["pool_scale"], lw["w_o"], lw["ln1_g"], lw["ln1_b"])


def _sort16_pairs():
    pairs = []

    def merge(lo, hi, r):
        step = r * 2
        if step < hi - lo:
            merge(lo, hi, step)
            merge(lo + r, hi, step)
            for i in range(lo + r, hi - r, step):
                pairs.append((i, i + r))
        else:
            pairs.append((lo, lo + r))

    def sort(lo, hi):
        if hi - lo >= 1:
            mid = lo + (hi - lo) // 2
            sort(lo, mid)
            sort(mid + 1, hi)
            merge(lo, hi, 1)

    sort(0, PEER_TOPK - 1)
    return tuple(pairs)


SORT16_PAIRS = _sort16_pairs()


def _sort16_desc(vals):
    vals = list(vals)
    for i, j in SORT16_PAIRS:
        hi = jnp.maximum(vals[i], vals[j])
        lo = jnp.minimum(vals[i], vals[j])
        vals[i], vals[j] = hi, lo
    return vals


def _merge_top16(xs, ys):
    vals = [jnp.maximum(xs[i], ys[PEER_TOPK - 1 - i]) for i in range(PEER_TOPK)]
    d = PEER_TOPK // 2
    while d >= 1:
        for i in range(PEER_TOPK):
            if (i & d) == 0:
                hi = jnp.maximum(vals[i], vals[i + d])
                lo = jnp.minimum(vals[i], vals[i + d])
                vals[i], vals[i + d] = hi, lo
        d //= 2
    return vals


def _top16_desc(tiles):
    tiles = list(tiles)
    neg = jnp.full_like(tiles[0], -jnp.inf)
    while len(tiles) % PEER_TOPK:
        tiles.append(neg)
    best = None
    for g in range(len(tiles) // PEER_TOPK):
        grp = _sort16_desc(tiles[g * PEER_TOPK:(g + 1) * PEER_TOPK])
        best = grp if best is None else _merge_top16(best, grp)
    return best


S_PITCH = PEER_NKEYS + SUBLANES


def _peer_select_kernel(x_ref, sh_ref, sc_ref, wq_ref, keys_ref,
                        s1_ref, c1_ref, s2_ref, e2_ref, tau_ref,
                        q_sc, sa_sc, sb_sc, *, tn):
    s_sc = (sa_sc, sb_sc)
    x = x_ref[...]
    sh = sh_ref[...].reshape(-1, D_MODEL)
    sc = sc_ref[...].reshape(-1, D_MODEL)
    h2 = (x * (1.0 + sc) + sh).astype(BF16)
    q_sc[...] = lax.dot_general(wq_ref[...], h2, NT_DIMS, preferred_element_type=F32).astype(BF16)
    for h in range(PEER_HEADS):
        for p in range(2):
            r0 = (h * 2 + p) * PEER_DK
            s_hp = jnp.dot(keys_ref[h * 2 + p], q_sc[r0:r0 + PEER_DK, :], preferred_element_type=F32)
            for lg in range(tn // LANES):
                s_sc[p][lg, h * S_PITCH:h * S_PITCH + PEER_NKEYS, :] = s_hp[:, lg * LANES:(lg + 1) * LANES]
    for lg in range(tn // LANES):
        lanes = pl.ds(lg * LANES, LANES)
        tops = []
        for p in range(2):
            tiles = [s_sc[p][lg, pl.ds(i, PEER_HEADS, stride=S_PITCH), :] for i in range(PEER_NKEYS)]
            tops.append(_top16_desc(tiles))
        a, b = tops
        cands = []
        for r1 in range(PEER_TOPK):
            for r2 in range(PEER_TOPK // (r1 + 1)):
                cands.append(b[r2] + a[r1])
        c = _top16_desc(cands)
        tau = c[PEER_TOPK - 1]
        z = jnp.zeros_like(tau)
        for kk in range(PEER_TOPK):
            z = z + jnp.exp(c[kk] - c[0])
        zinv = 1.0 / z
        tau_ref[lg] = tau
        for h in range(PEER_HEADS):
            rows_in = slice(h * S_PITCH, h * S_PITCH + PEER_NKEYS)
            rows_out = slice(h * PEER_NKEYS, (h + 1) * PEER_NKEYS)
            s1 = sa_sc[lg, rows_in, :]
            s2 = sb_sc[lg, rows_in, :]
            s1_ref[lg, rows_out, :] = s1
            s2_ref[lg, rows_out, :] = s2
            c1_ref[lg, rows_out, :] = jnp.exp(s1 - a[0][h:h + 1, :]) * zinv[h:h + 1, :]
            e2_ref[lg, rows_out, :] = jnp.exp(s2 - b[0][h:h + 1, :])


def _mod_spec(arr, tn, tiles_per_row):
    if arr.ndim == 3:
        return pl.BlockSpec((1, 1, D_MODEL), lambda i, *_: (i // tiles_per_row, 0, 0))
    return pl.BlockSpec((tn, D_MODEL), lambda i, *_: (i, 0))


def _peer_select(x1, sh2, sc2, lw, *, tn, tiles_per_row):
    n_tok = x1.shape[0]
    assert n_tok % tn == 0
    sel_rows = PEER_HEADS * PEER_NKEYS
    n_lg = tn // LANES
    big = lambda: pl.BlockSpec((n_lg, sel_rows, LANES), lambda i: (i, 0, 0))
    return pl.pallas_call(
        functools.partial(_peer_select_kernel, tn=tn),
        grid=(n_tok // tn,),
        in_specs=[
            pl.BlockSpec((tn, D_MODEL), lambda i: (i, 0)),
            _mod_spec(sh2, tn, tiles_per_row),
            _mod_spec(sc2, tn, tiles_per_row),
            pl.BlockSpec((2 * PEER_HEADS * PEER_DK, D_MODEL), lambda i: (0, 0)),
            pl.BlockSpec((2 * PEER_HEADS, PEER_NKEYS, PEER_DK), lambda i: (0, 0, 0)),
        ],
        out_specs=[big(), big(), big(), big(), pl.BlockSpec((n_lg, PEER_HEADS, LANES), lambda i: (i, 0, 0))],
        out_shape=[jax.ShapeDtypeStruct((n_tok // LANES, sel_rows, LANES), F32)] * 4
        + [jax.ShapeDtypeStruct((n_tok // LANES, PEER_HEADS, LANES), F32)],
        scratch_shapes=[
            pltpu.VMEM((2 * PEER_HEADS * PEER_DK, tn), BF16),
            pltpu.VMEM((tn // LANES, PEER_HEADS * S_PITCH, LANES), F32),
            pltpu.VMEM((tn // LANES, PEER_HEADS * S_PITCH, LANES), F32),
        ],
        compiler_params=pltpu.CompilerParams(
            dimension_semantics=("arbitrary",), vmem_limit_bytes=VMEM_LIMIT_BYTES),
        name="peer_select",
    )(x1, sh2, sc2, lw["w_pq_t"], lw["peer_keys"])


GATE_SUB = 64
PEER_STAGE = 2 * PEER_NKEYS


def _peer_dense_kernel(x_ref, sh_ref, sc_ref, g_ref, s1_ref, c1_ref, s2_ref, e2_ref, tau_ref,
                       u_ref, vt_ref, lng_ref, lnb_ref, o_ref,
                       xb_sc, acc_sc, at_a, at_b, coef_sc, s1b_sc, c1b_sc, taub_sc, *, tn, te):
    j = pl.program_id(1)
    n_j = pl.num_programs(1) - 1
    stage = at_a.shape[0]
    n_st = te // stage
    keys_per_tile = te // PEER_NKEYS
    out_chunk = D_MODEL // n_st
    assert n_st % 2 == 0
    par = lax.rem(j, 2)

    @pl.when(j == 0)
    def _():
        sh = sh_ref[...].reshape(-1, D_MODEL)
        sc = sc_ref[...].reshape(-1, D_MODEL)
        xb_sc[...] = (x_ref[...] * (1.0 + sc) + sh).T.astype(BF16)
        acc_sc[...] = jnp.zeros_like(acc_sc)
        coef_sc[1] = jnp.zeros(coef_sc.shape[1:], BF16)
        for lg in range(tn // LANES):
            tau_t = tau_ref[lg]
            for h in range(PEER_HEADS):
                taub_sc[lg * PEER_HEADS + h] = jnp.broadcast_to(tau_t[h:h + 1, :], (SUBLANES, LANES))

    def fill_key_tables():
        for lg in range(tn // LANES):
            for h in range(PEER_HEADS):
                krows = pl.ds(pl.multiple_of(h * PEER_NKEYS + j * keys_per_tile, SUBLANES), keys_per_tile)
                s1_t = s1_ref[lg, krows, :]
                c1_t = c1_ref[lg, krows, :]
                for k in range(keys_per_tile):
                    s1b_sc[k, lg * PEER_HEADS + h] = jnp.broadcast_to(s1_t[k:k + 1, :], (SUBLANES, LANES))
                    c1b_sc[k, lg * PEER_HEADS + h] = jnp.broadcast_to(c1_t[k:k + 1, :], (SUBLANES, LANES))

    def scores(st):
        rows = pl.ds(pl.multiple_of(st * stage, stage), stage)
        return jnp.dot(u_ref[rows, :], xb_sc[...], preferred_element_type=F32)

    def values_chunk(st):
        rows = pl.ds(pl.multiple_of(st * out_chunk, out_chunk), out_chunk)
        acc_sc[rows, :] += jnp.dot(vt_ref[rows, :], coef_sc[1 - par], preferred_element_type=F32)

    def gate_stage(st, at_ref):
        for qq in range(stage // PEER_NKEYS):
            key_in_tile = st * (stage // PEER_NKEYS) + qq
            for lg in range(tn // LANES):
                lanes = slice(lg * LANES, (lg + 1) * LANES)
                for sub in range(PEER_NKEYS // GATE_SUB):
                    gate = jnp.zeros((GATE_SUB // SUBLANES, SUBLANES, LANES), F32)
                    for h in range(PEER_HEADS):
                        hrows = slice(h * PEER_NKEYS + sub * GATE_SUB, h * PEER_NKEYS + (sub + 1) * GATE_SUB)
                        s1b = s1b_sc[key_in_tile, lg * PEER_HEADS + h]
                        c1b = c1b_sc[key_in_tile, lg * PEER_HEADS + h]
                        taub = taub_sc[lg * PEER_HEADS + h]
                        s2 = s2_ref[lg, hrows, :].reshape(gate.shape)
                        e2 = e2_ref[lg, hrows, :].reshape(gate.shape)
                        gate = gate + jnp.where((s2 + s1b) >= taub, e2 * c1b, 0.0)
                    r0 = qq * PEER_NKEYS + sub * GATE_SUB
                    a = at_ref[r0:r0 + GATE_SUB, lanes]
                    act = 0.5 * a * (1.0 + lax.erf(a * (2.0 ** -0.5)))
                    out_rows = pl.ds(pl.multiple_of(st * stage + r0, GATE_SUB), GATE_SUB)
                    coef_sc[par, out_rows, lanes] = (gate.reshape(GATE_SUB, LANES) * act).astype(BF16)

    @pl.when(j < n_j)
    def _():
        at_a[...] = scores(0)
        fill_key_tables()

        def stage_pair(k, carry):
            for half, (at_cur, at_nxt) in enumerate(((at_a, at_b), (at_b, at_a))):
                st = 2 * k + half
                at_nxt[...] = scores(jnp.minimum(st + 1, n_st - 1))
                values_chunk(st)
                gate_stage(st, at_cur)
            return carry

        lax.fori_loop(0, n_st // 2, stage_pair, 0)

    @pl.when(j == n_j)
    def _():
        for st in range(n_st):
            values_chunk(st)
        g2 = g_ref[...].reshape(-1, D_MODEL)
        z = ALPHA * x_ref[...] + g2 * acc_sc[...].T
        o_ref[...] = _layer_norm(z) * lng_ref[...] + lnb_ref[...]


def _peer_dense(x1, sh2, sc2, g2, sel, lw, *, tn, te, tiles_per_row):
    n_tok = x1.shape[0]
    assert n_tok % tn == 0
    sel_rows = PEER_HEADS * PEER_NKEYS
    n_lg = tn // LANES
    s1, c1, s2, e2, tau = sel
    big = lambda: pl.BlockSpec((n_lg, sel_rows, LANES), lambda i, j: (i, 0, 0))
    n_j = PEER_EXPERTS // te
    return pl.pallas_call(
        functools.partial(_peer_dense_kernel, tn=tn, te=te),
        grid=(n_tok // tn, n_j + 1),
        in_specs=[
            pl.BlockSpec((tn, D_MODEL), lambda i, j: (i, 0)),
            _mod_spec(sh2, tn, tiles_per_row),
            _mod_spec(sc2, tn, tiles_per_row),
            _mod_spec(g2, tn, tiles_per_row),
            big(), big(), big(), big(),
            pl.BlockSpec((n_lg, PEER_HEADS, LANES), lambda i, j: (i, 0, 0)),
            pl.BlockSpec((te, D_MODEL), lambda i, j: (jnp.minimum(j, n_j - 1), 0)),
            pl.BlockSpec((D_MODEL, te), lambda i, j: (0, jnp.maximum(j - 1, 0))),
            pl.BlockSpec((1, D_MODEL), lambda i, j: (0, 0)),
            pl.BlockSpec((1, D_MODEL), lambda i, j: (0, 0)),
        ],
        out_specs=pl.BlockSpec((tn, D_MODEL), lambda i, j: (i, 0)),
        out_shape=jax.ShapeDtypeStruct((n_tok, D_MODEL), F32),
        scratch_shapes=[
            pltpu.VMEM((D_MODEL, tn), BF16),
            pltpu.VMEM((D_MODEL, tn), F32),
            pltpu.VMEM((PEER_STAGE, tn), F32),
            pltpu.VMEM((PEER_STAGE, tn), F32),
            pltpu.VMEM((2, te, tn), BF16),
            pltpu.VMEM((te // PEER_NKEYS, n_lg * PEER_HEADS, SUBLANES, LANES), F32),
            pltpu.VMEM((te // PEER_NKEYS, n_lg * PEER_HEADS, SUBLANES, LANES), F32),
            pltpu.VMEM((n_lg * PEER_HEADS, SUBLANES, LANES), F32),
        ],
        compiler_params=pltpu.CompilerParams(
            dimension_semantics=("arbitrary", "arbitrary"), vmem_limit_bytes=VMEM_LIMIT_BYTES),
        name="peer_dense",
    )(x1, sh2, sc2, g2, s1, c1, s2, e2, tau, lw["peer_u"], lw["peer_v_t"], lw["ln2_g"], lw["ln2_b"])


def _prep_layer(l, w_in, b_gate, mh_g, sgu_g, sgu_b, w_s, b_s, w_pool, pool_scale, w_o, ln1_g, ln1_b,
                w_pq, peer_keys, peer_u, peer_v, ln2_g, ln2_b, sample_rows):
    n_gate = 2 * MLSTM_HEADS
    g0 = 4 * MLSTM_WIDTH
    w = w_in[l]
    w_gate = jnp.pad(w[:, g0:g0 + n_gate], ((0, 0), (0, LANES - n_gate)))
    w_in_p = jnp.concatenate([w[:, :g0], w_gate, w[:, g0 + n_gate:]], axis=1).astype(BF16)
    bs_full = jnp.repeat(jnp.swapaxes(b_s[l], 0, 1), SGU_DH, axis=1)
    reps = SGU_CHUNK // sample_rows
    eye = jnp.eye(len(POOL_WINDOWS), dtype=F32)
    dg = POOL_WIDTH // len(POOL_WINDOWS)
    w_pool_bd = (eye[:, None, :, None] * w_pool[l][:, :, None, :]).reshape(POOL_WIDTH, POOL_WIDTH)
    row = lambda a: a.reshape(1, -1)
    return {
        "w_in": w_in_p,
        "w_gate": w_gate,
        "b_gate": jnp.pad(b_gate[l], (0, LANES - n_gate)).reshape(1, LANES),
        "mh_g": row(mh_g[l]), "sgu_g": row(sgu_g[l]), "sgu_b": row(sgu_b[l]),
        "w_s": w_s[l],
        "b_s": bs_full,
        "w_s_sample": jnp.tile(w_s[l][:, :sample_rows, :sample_rows], (1, reps, reps)),
        "b_s_sample": jnp.tile(bs_full[:sample_rows], (reps, 1)),
        "w_pool": w_pool_bd.astype(BF16),
        "pool_scale": row(pool_scale[l]),
        "w_o": w_o[l].astype(BF16),
        "ln1_g": row(ln1_g[l]), "ln1_b": row(ln1_b[l]),
        "w_pq_t": w_pq[l].T.astype(BF16),
        "peer_keys": peer_keys[l].reshape(2 * PEER_HEADS, PEER_NKEYS, PEER_DK).astype(BF16),
        "peer_u": peer_u[l].astype(BF16),
        "peer_v_t": peer_v[l].T.astype(BF16),
        "ln2_g": row(ln2_g[l]), "ln2_b": row(ln2_b[l]),
    }


def kernel(x_prompt, x_sample, state_mlstm_C, state_mlstm_n, state_mlstm_m, state_pool, c_prompt, c_sample,
           w_ada, b_ada, w_in, b_gate, mh_g, sgu_g, sgu_b, w_s, b_s, w_pool, pool_scale, w_o, ln1_g, ln1_b,
           w_pq, peer_keys, peer_u, peer_v, ln2_g, ln2_b):
    bp, seq, _ = x_prompt.shape
    bs, dec_seq, _ = x_sample.shape
    sample_rows = SUBLANES
    prompt_rows = min(seq, 256)
    peer_tn = 512
    peer_te = 1024
    sel_tn = 256

    ada = _ada(jnp.concatenate([c_prompt, c_sample], axis=0), w_ada, b_ada)

    xp = x_prompt
    xs = jnp.pad(x_sample, ((0, 0), (0, sample_rows - dec_seq), (0, 0)))
    zeros_c = jnp.zeros((bp, MLSTM_HEADS, MLSTM_DH, MLSTM_DH), F32)
    zeros_n = jnp.zeros((bp, MLSTM_HEADS, MLSTM_DH), F32)
    zeros_m = jnp.zeros((bp, MLSTM_HEADS, LANES), F32)
    zeros_hist = jnp.zeros((bp, POOL_HIST, POOL_WIDTH), F32)
    outs = [[] for _ in range(9)]
    for l in range(DEPTH):
        lw = _prep_layer(l, w_in, b_gate, mh_g, sgu_g, sgu_b, w_s, b_s, w_pool, pool_scale, w_o, ln1_g, ln1_b,
                         w_pq, peer_keys, peer_u, peer_v, ln2_g, ln2_b, sample_rows)
        mods = jnp.pad(ada[l].reshape(bp + bs, N_ADA, D_MODEL), ((0, 0), (0, SUBLANES - N_ADA), (0, 0)))
        mod_p, mod_s = mods[:bp], mods[bp:]

        x1p, cp, np_, mp, tailp = _mixer(
            xp, mod_p, zeros_c, zeros_n, zeros_m, zeros_hist, lw,
            nb=1, rows=prompt_rows, chunk=SGU_CHUNK, n_valid=SGU_CHUNK, sgu_blk=SGU_CHUNK, pos0=0,
            want_vn=False)
        m_in = jnp.broadcast_to(state_mlstm_m[l][:, :, None], (bs, MLSTM_HEADS, LANES))
        hist_s = jnp.pad(state_pool[l], ((0, 0), (1, 0), (0, 0)))
        x1s, cs, ns, ms, tails, vns = _mixer(
            xs, mod_s, state_mlstm_C[l], state_mlstm_n[l], m_in, hist_s, lw,
            nb=16, rows=sample_rows, chunk=sample_rows, n_valid=dec_seq, sgu_blk=sample_rows, pos0=PAST_LEN,
            want_vn=True)

        x1p2 = x1p.reshape(bp * seq, D_MODEL)
        shp, scp, gp = mod_p[:, 3:4], mod_p[:, 4:5], mod_p[:, 5:6]
        sel = _peer_select(x1p2, shp, scp, lw, tn=sel_tn, tiles_per_row=seq // sel_tn)
        xp = _peer_dense(x1p2, shp, scp, gp, sel, lw, tn=peer_tn, te=peer_te,
                         tiles_per_row=seq // peer_tn).reshape(bp, seq, D_MODEL)
        x1s2 = x1s[:, :dec_seq].reshape(bs * dec_seq, D_MODEL)
        shs, scs, gs = (jnp.repeat(mod_s[:, i], dec_seq, axis=0) for i in (3, 4, 5))
        n_s = bs * dec_seq
        sel = _peer_select(x1s2, shs, scs, lw, tn=min(sel_tn, n_s), tiles_per_row=1)
        xs_new = _peer_dense(x1s2, shs, scs, gs, sel, lw, tn=min(peer_tn, n_s), te=peer_te, tiles_per_row=1)
        xs_new = xs_new.reshape(bs, dec_seq, D_MODEL)
        xs = jnp.pad(xs_new, ((0, 0), (0, sample_rows - dec_seq), (0, 0)))

        for lst, val in zip(outs, (cp, np_, mp[:, :, 0], tailp[:, 1:], cs, ns, ms[:, :, 0], tails[:, 1:],
                                   vns[:, :dec_seq])):
            lst.append(val)
    return (xp, xs_new) + tuple(jnp.stack(o) for o in outs)
```

```python
import functools

import jax
import jax.numpy as jnp
import numpy as np
from jax import lax
from jax.experimental import pallas as pl
from jax.experimental.pallas import tpu as pltpu

F32 = jnp.float32
BF16 = jnp.bfloat16
HIGHEST = lax.Precision.HIGHEST

D_MODEL = 1024
DEPTH = 2
N_ADA = 6
MLSTM_HEADS = 4
MLSTM_DH = 128
MLSTM_WIDTH = MLSTM_HEADS * MLSTM_DH
SGU_WIDTH = 256
SGU_HEADS = 4
SGU_DH = SGU_WIDTH // SGU_HEADS
SGU_CHUNK = 128
POOL_WIDTH = 256
POOL_WINDOWS = (2, 4, 8, 16)
POOL_HIST = 16
PEER_HEADS = 8
PEER_NKEYS = 128
PEER_TOPK = 16
PEER_DK = 128
PEER_EXPERTS = PEER_NKEYS * PEER_NKEYS
PAST_LEN = 16384
ALPHA = (2 * DEPTH) ** 0.25
LN_EPS = 1e-5

LANES = 128
SUBLANES = 8
VMEM_LIMIT_BYTES = 56 * 1024 * 1024

COL_Q = 0
COL_K = MLSTM_WIDTH
COL_V = 2 * MLSTM_WIDTH
COL_O = 3 * MLSTM_WIDTH
COL_GATE = 4 * MLSTM_WIDTH
COL_U = COL_GATE + LANES
COL_VS = COL_U + SGU_WIDTH
COL_P = COL_VS + SGU_WIDTH
IN_COLS_PADDED = COL_P + POOL_WIDTH

NT_DIMS = (((1,), (1,)), ((), ()))
TN_DIMS = (((0,), (0,)), ((), ()))


def _layer_norm(x):
    mu = jnp.mean(x, axis=-1, keepdims=True)
    d = x - mu
    var = jnp.mean(d * d, axis=-1, keepdims=True)
    return d * lax.rsqrt(var + LN_EPS)


def _ada_kernel(c_ref, w_ref, b_ref, o_ref):
    c = c_ref[...]
    s = c * jax.nn.sigmoid(c)
    o_ref[0] = jnp.dot(s, w_ref[0], precision=HIGHEST, preferred_element_type=F32) + b_ref[0]


def _ada(c_all, w_ada, b_ada):
    rows = c_all.shape[0]
    cols = w_ada.shape[-1]
    tile = 1536
    return pl.pallas_call(
        _ada_kernel,
        grid=(DEPTH, cols // tile),
        in_specs=[
            pl.BlockSpec((rows, D_MODEL), lambda l, j: (0, 0)),
            pl.BlockSpec((1, D_MODEL, tile), lambda l, j: (l, 0, j)),
            pl.BlockSpec((1, 1, tile), lambda l, j: (l, 0, j)),
        ],
        out_specs=pl.BlockSpec((1, rows, tile), lambda l, j: (l, 0, j)),
        out_shape=jax.ShapeDtypeStruct((DEPTH, rows, cols), F32),
        compiler_params=pltpu.CompilerParams(vmem_limit_bytes=VMEM_LIMIT_BYTES),
        name="ada",
    )(c_all, w_ada, b_ada.reshape(DEPTH, 1, cols))


def _mlstm_chunk(q, k, v, ig_col, ig_row, b_col, b_row, causal, c_state, n_state, m_state):
    length = q.shape[0]
    dlog = jnp.where(causal, b_col - b_row + ig_row, -jnp.inf)
    inter = b_col + m_state
    m_t = jnp.maximum(inter, jnp.max(dlog, axis=1, keepdims=True))
    w_intra = jnp.exp(dlog - m_t)
    w_inter = jnp.exp(inter - m_t)
    qb = q.astype(BF16)
    kb = k.astype(BF16)
    scores = lax.dot_general(qb, kb, NT_DIMS, preferred_element_type=F32)
    a = w_intra * scores
    num = jnp.dot(a.astype(BF16), v.astype(BF16), preferred_element_type=F32)
    num = num + w_inter * lax.dot_general(qb, c_state.astype(BF16), NT_DIMS, preferred_element_type=F32)
    den = jnp.sum(a, axis=1, keepdims=True) + w_inter * jnp.sum(q * n_state, axis=1, keepdims=True)
    h = num / jnp.maximum(jnp.abs(den), jnp.exp(-m_t))
    b_end = b_col[length - 1:length, :]
    dend = b_end - b_col + ig_col
    m_new = jnp.maximum(b_end + m_state, jnp.max(dend, axis=0, keepdims=True))
    wc = jnp.exp(dend - m_new)
    dec = jnp.exp(b_end + m_state - m_new)
    vw = (v * wc).astype(BF16)
    c_new = dec * c_state + lax.dot_general(vw, kb, TN_DIMS, preferred_element_type=F32)
    n_new = dec * n_state + jnp.sum(k * wc, axis=0, keepdims=True)
    return h, c_new, n_new, m_new


def _mixer_kernel(x_ref, mod_ref, c_in_ref, n_in_ref, m_in_ref, hist_ref,
                  w_in_ref, wgate_ref, bgate_ref, mhg_ref, sgug_ref, sgub_ref, ws_ref, bs_ref,
                  wpool_ref, pscale_ref, wo_ref, ln1g_ref, ln1b_ref,
                  x1_ref, c_out_ref, n_out_ref, m_out_ref, tail_ref, vn_ref,
                  proj_sc, ycat_sc, c_sc, n_sc, m_sc, carry_sc, pbuf_sc,
                  *, nb, rows, chunk, n_valid, sgu_blk, pos0):
    ci = pl.program_id(1)
    n_chunks = pl.num_programs(1)
    m_rows = nb * rows

    @pl.when(ci == 0)
    def _():
        c_sc[...] = c_in_ref[...]
        n_sc[...] = n_in_ref[...]
        m_sc[...] = m_in_ref[...]
        carry_sc[...] = hist_ref[...]

    x3 = x_ref[...]
    mod = mod_ref[...]
    sh1, sc1, g1 = mod[:, 0:1, :], mod[:, 1:2, :], mod[:, 2:3, :]
    h3 = x3 * (1.0 + sc1) + sh1
    h2d = h3.reshape(m_rows, D_MODEL)
    proj_sc[...] = jnp.dot(h2d.astype(BF16), w_in_ref[...], preferred_element_type=F32)
    proj_sc[:, COL_GATE:COL_GATE + LANES] = jnp.dot(h2d, wgate_ref[...], precision=HIGHEST,
                                                    preferred_element_type=F32)

    r_io = lax.broadcasted_iota(jnp.int32, (chunk, chunk), 0)
    c_io = lax.broadcasted_iota(jnp.int32, (chunk, chunk), 1)
    causal = c_io <= r_io
    tri = jnp.where(causal, 1.0, 0.0).astype(F32)
    row_id = lax.broadcasted_iota(jnp.int32, (chunk, 1), 0)
    valid_col = row_id < n_valid
    bgate = bgate_ref[...]
    mhg = mhg_ref[...]

    def seq_body(s, carry):
        for j in range(rows // chunk):
            row0 = s * rows + j * chunk
            if not isinstance(row0, int):
                row0 = pl.multiple_of(row0, SUBLANES)
            rsl = pl.ds(row0, chunk)
            gates = proj_sc[rsl, COL_GATE:COL_GATE + LANES] + bgate
            lf = jnp.where(valid_col, jax.nn.log_sigmoid(gates), 0.0)
            bcum = jnp.dot(tri, lf, precision=HIGHEST, preferred_element_type=F32)
            ig_all = jnp.where(valid_col, gates, -jnp.inf)
            bcum_t = bcum.T
            ig_t = ig_all.T
            for h in range(MLSTM_HEADS):
                hs = slice(h * MLSTM_DH, (h + 1) * MLSTM_DH)
                q = proj_sc[rsl, COL_Q + h * MLSTM_DH:COL_Q + (h + 1) * MLSTM_DH] * (MLSTM_DH ** -0.5)
                k = proj_sc[rsl, COL_K + h * MLSTM_DH:COL_K + (h + 1) * MLSTM_DH]
                v = proj_sc[rsl, COL_V + h * MLSTM_DH:COL_V + (h + 1) * MLSTM_DH]
                o = proj_sc[rsl, COL_O + h * MLSTM_DH:COL_O + (h + 1) * MLSTM_DH]
                fcol = MLSTM_HEADS + h
                hh, c_new, n_new, m_new = _mlstm_chunk(
                    q, k, v,
                    ig_all[:, h:h + 1], ig_t[h:h + 1, 0:chunk],
                    bcum[:, fcol:fcol + 1], bcum_t[fcol:fcol + 1, 0:chunk],
                    causal, c_sc[s, h], n_sc[s, pl.ds(h, 1), :], m_sc[s, pl.ds(h, 1), 0:1])
                c_sc[s, h] = c_new
                n_sc[s, pl.ds(h, 1), :] = n_new
                m_sc[s, pl.ds(h, 1), :] = jnp.broadcast_to(m_new, (1, LANES))
                hn = _layer_norm(hh) * mhg[:, hs]
                ycat_sc[rsl, hs] = jax.nn.sigmoid(o) * hn
        return carry

    if nb == 1:
        seq_body(0, 0)
    else:
        lax.fori_loop(0, nb, seq_body, 0)

    u_s = proj_sc[:, COL_U:COL_U + SGU_WIDTH]
    v_s = proj_sc[:, COL_VS:COL_VS + SGU_WIDTH]
    gr = lax.broadcasted_iota(jnp.int32, (SGU_WIDTH, SGU_WIDTH), 0) // SGU_DH
    gc = lax.broadcasted_iota(jnp.int32, (SGU_WIDTH, SGU_WIDTH), 1) // SGU_DH
    avg = jnp.where(gr == gc, 1.0 / SGU_DH, 0.0).astype(F32)
    mu = jnp.dot(v_s, avg, precision=HIGHEST, preferred_element_type=F32)
    dv = v_s - mu
    var = jnp.dot(dv * dv, avg, precision=HIGHEST, preferred_element_type=F32)
    vn = dv * lax.rsqrt(var + LN_EPS) * sgug_ref[...] + sgub_ref[...]
    if vn_ref is not None:
        vn_ref[...] = vn.reshape(nb, rows, SGU_WIDTH)
    sr = lax.broadcasted_iota(jnp.int32, (SGU_CHUNK, SGU_CHUNK), 0)
    scol = lax.broadcasted_iota(jnp.int32, (SGU_CHUNK, SGU_CHUNK), 1)
    smask = (scol <= sr) & ((sr // sgu_blk) == (scol // sgu_blk))
    lane_grp = lax.broadcasted_iota(jnp.int32, (SGU_CHUNK, SGU_WIDTH), 1) // SGU_DH
    vnb = vn.astype(BF16)
    for r in range(m_rows // SGU_CHUNK):
        rs = slice(r * SGU_CHUNK, (r + 1) * SGU_CHUNK)
        mix = jnp.zeros((SGU_CHUNK, SGU_WIDTH), F32)
        for g in range(SGU_HEADS):
            wg = jnp.where(smask, ws_ref[g], 0.0).astype(BF16)
            mg = jnp.dot(wg, vnb[rs], preferred_element_type=F32)
            mix = jnp.where(lane_grp == g, mg, mix)
        ycat_sc[rs, MLSTM_WIDTH:MLSTM_WIDTH + SGU_WIDTH] = u_s[rs] * (mix + bs_ref[...])

    pbuf_sc[:, 0:POOL_HIST, :] = carry_sc[...]
    pbuf_sc[:, POOL_HIST:POOL_HIST + rows, :] = proj_sc[:, COL_P:COL_P + POOL_WIDTH].reshape(nb, rows, POOL_WIDTH)
    lane_w = lax.broadcasted_iota(jnp.int32, (1, 1, POOL_WIDTH), 2) // (POOL_WIDTH // len(POOL_WINDOWS))
    x0 = pbuf_sc[:, POOL_HIST:POOL_HIST + rows, :]
    acc = x0
    wsum = jnp.zeros_like(x0)
    for kk in range(1, max(POOL_WINDOWS)):
        acc = acc + pbuf_sc[:, POOL_HIST - kk:POOL_HIST - kk + rows, :]
        if (kk + 1) in POOL_WINDOWS:
            wsum = jnp.where(lane_w == POOL_WINDOWS.index(kk + 1), acc, wsum)
    t_io = lax.broadcasted_iota(jnp.int32, (1, rows, POOL_WIDTH), 1)
    win = jnp.left_shift(2, lane_w)
    cnt = jnp.minimum(pos0 + ci * rows + t_io + 1, win).astype(F32)
    pooled = wsum / cnt - x0
    y_c = jnp.dot(pooled.reshape(m_rows, POOL_WIDTH).astype(BF16), wpool_ref[...],
                  preferred_element_type=F32) * pscale_ref[...]
    ycat_sc[:, MLSTM_WIDTH + SGU_WIDTH:D_MODEL] = y_c
    carry_sc[...] = pbuf_sc[:, rows:rows + POOL_HIST, :]

    y = jnp.dot(ycat_sc[...].astype(BF16), wo_ref[...], preferred_element_type=F32)
    z = ALPHA * x3 + g1 * y.reshape(nb, rows, D_MODEL)
    x1_ref[...] = _layer_norm(z) * ln1g_ref[...] + ln1b_ref[...]

    @pl.when(ci == n_chunks - 1)
    def _():
        c_out_ref[...] = c_sc[...]
        n_out_ref[...] = n_sc[...]
        m_out_ref[...] = m_sc[...]
        tail_ref[...] = pbuf_sc[:, n_valid + rows - chunk:n_valid + rows - chunk + POOL_HIST, :]


def _mixer(x, mod, c0, n0, m0, hist, lw, *, nb, rows, chunk, n_valid, sgu_blk, pos0, want_vn):
    bsz, seq, _ = x.shape
    grid = (bsz // nb, seq // rows)
    m_rows = nb * rows
    kern = functools.partial(_mixer_kernel, nb=nb, rows=rows, chunk=chunk, n_valid=n_valid,
                             sgu_blk=sgu_blk, pos0=pos0)
    if not want_vn:
        def kern(*refs, _k=kern):
            return _k(*refs[:24], None, *refs[24:])

    def full(shape):
        return pl.BlockSpec(shape, lambda b, c: (0,) * len(shape))

    per_b3 = lambda s1, s2: pl.BlockSpec((nb, s1, s2), lambda b, c: (b, 0, 0))
    in_specs = [
        pl.BlockSpec((nb, rows, D_MODEL), lambda b, c: (b, c, 0)),
        per_b3(SUBLANES, D_MODEL),
        pl.BlockSpec((nb, MLSTM_HEADS, MLSTM_DH, MLSTM_DH), lambda b, c: (b, 0, 0, 0)),
        per_b3(MLSTM_HEADS, MLSTM_DH),
        per_b3(MLSTM_HEADS, LANES),
        per_b3(POOL_HIST, POOL_WIDTH),
        full((D_MODEL, IN_COLS_PADDED)),
        full((D_MODEL, LANES)),
        full((1, LANES)),
        full((1, MLSTM_WIDTH)),
        full((1, SGU_WIDTH)),
        full((1, SGU_WIDTH)),
        full((SGU_HEADS, SGU_CHUNK, SGU_CHUNK)),
        full((SGU_CHUNK, SGU_WIDTH)),
        full((POOL_WIDTH, POOL_WIDTH)),
        full((1, POOL_WIDTH)),
        full((D_MODEL, D_MODEL)),
        full((1, D_MODEL)),
        full((1, D_MODEL)),
    ]
    out_specs = [
        pl.BlockSpec((nb, rows, D_MODEL), lambda b, c: (b, c, 0)),
        pl.BlockSpec((nb, MLSTM_HEADS, MLSTM_DH, MLSTM_DH), lambda b, c: (b, 0, 0, 0)),
        per_b3(MLSTM_HEADS, MLSTM_DH),
        per_b3(MLSTM_HEADS, LANES),
        per_b3(POOL_HIST, POOL_WIDTH),
    ]
    out_shape = [
        jax.ShapeDtypeStruct((bsz, seq, D_MODEL), F32),
        jax.ShapeDtypeStruct((bsz, MLSTM_HEADS, MLSTM_DH, MLSTM_DH), F32),
        jax.ShapeDtypeStruct((bsz, MLSTM_HEADS, MLSTM_DH), F32),
        jax.ShapeDtypeStruct((bsz, MLSTM_HEADS, LANES), F32),
        jax.ShapeDtypeStruct((bsz, POOL_HIST, POOL_WIDTH), F32),
    ]
    if want_vn:
        out_specs.append(pl.BlockSpec((nb, rows, SGU_WIDTH), lambda b, c: (b, c, 0)))
        out_shape.append(jax.ShapeDtypeStruct((bsz, seq, SGU_WIDTH), F32))
    scratch = [
        pltpu.VMEM((m_rows, IN_COLS_PADDED), F32),
        pltpu.VMEM((m_rows, D_MODEL), F32),
        pltpu.VMEM((nb, MLSTM_HEADS, MLSTM_DH, MLSTM_DH), F32),
        pltpu.VMEM((nb, MLSTM_HEADS, MLSTM_DH), F32),
        pltpu.VMEM((nb, MLSTM_HEADS, LANES), F32),
        pltpu.VMEM((nb, POOL_HIST, POOL_WIDTH), F32),
        pltpu.VMEM((nb, POOL_HIST + rows, POOL_WIDTH), F32),
    ]
    return pl.pallas_call(
        kern,
        grid=grid,
        in_specs=in_specs,
        out_specs=out_specs,
        out_shape=out_shape,
        scratch_shapes=scratch,
        compiler_params=pltpu.CompilerParams(
            dimension_semantics=("arbitrary", "arbitrary"), vmem_limit_bytes=VMEM_LIMIT_BYTES),
        name="mixer_sample" if want_vn else "mixer_prompt",
    )(x, mod, c0, n0, m0, hist, lw["w_in"], lw["w_gate"], lw["b_gate"], lw["mh_g"], lw["sgu_g"], lw["sgu_b"],
      lw["w_s_sample"] if want_vn else lw["w_s"], lw["b_s_sample"] if want_vn else lw["b_s"],
      lw["w_pool"], lw["pool_scale"], lw["w_o"], lw["ln1_g"], lw["ln1_b"])


def _sort16_pairs():
    pairs = []

    def merge(lo, hi, r):
        step = r * 2
        if step < hi - lo:
            merge(lo, hi, step)
            merge(lo + r, hi, step)
            for i in range(lo + r, hi - r, step):
                pairs.append((i, i + r))
        else:
            pairs.append((lo, lo + r))

    def sort(lo, hi):
        if hi - lo >= 1:
            mid = lo + (hi - lo) // 2
            sort(lo, mid)
            sort(mid + 1, hi)
            merge(lo, hi, 1)

    sort(0, PEER_TOPK - 1)
    return tuple(pairs)


SORT16_PAIRS = _sort16_pairs()


def _sort16_desc(vals):
    vals = list(vals)
    for i, j in SORT16_PAIRS:
        hi = jnp.maximum(vals[i], vals[j])
        lo = jnp.minimum(vals[i], vals[j])
        vals[i], vals[j] = hi, lo
    return vals


def _merge_top16(xs, ys):
    vals = [jnp.maximum(xs[i], ys[PEER_TOPK - 1 - i]) for i in range(PEER_TOPK)]
    d = PEER_TOPK // 2
    while d >= 1:
        for i in range(PEER_TOPK):
            if (i & d) == 0:
                hi = jnp.maximum(vals[i], vals[i + d])
                lo = jnp.minimum(vals[i], vals[i + d])
                vals[i], vals[i + d] = hi, lo
        d //= 2
    return vals


def _top16_desc(tiles):
    tiles = list(tiles)
    neg = jnp.full_like(tiles[0], -jnp.inf)
    while len(tiles) % PEER_TOPK:
        tiles.append(neg)
    best = None
    for g in range(len(tiles) // PEER_TOPK):
        grp = _sort16_desc(tiles[g * PEER_TOPK:(g + 1) * PEER_TOPK])
        best = grp if best is None else _merge_top16(best, grp)
    return best


S_PITCH = PEER_NKEYS + SUBLANES


def _peer_select_kernel(x_ref, sh_ref, sc_ref, wq_ref, keys_ref,
                        thr_ref, c1_ref, s2_ref, e2_ref,
                        q_sc, sa_sc, sb_sc, *, tn):
    s_sc = (sa_sc, sb_sc)
    x = x_ref[...]
    sh = sh_ref[...].reshape(-1, D_MODEL)
    sc = sc_ref[...].reshape(-1, D_MODEL)
    h2 = (x * (1.0 + sc) + sh).astype(BF16)
    q_sc[...] = lax.dot_general(wq_ref[...], h2, NT_DIMS, preferred_element_type=F32).astype(BF16)
    for h in range(PEER_HEADS):
        for p in range(2):
            r0 = (h * 2 + p) * PEER_DK
            s_hp = jnp.dot(keys_ref[h * 2 + p], q_sc[r0:r0 + PEER_DK, :], preferred_element_type=F32)
            for lg in range(tn // LANES):
                s_sc[p][lg, h * S_PITCH:h * S_PITCH + PEER_NKEYS, :] = s_hp[:, lg * LANES:(lg + 1) * LANES]
    for lg in range(tn // LANES):
        lanes = pl.ds(lg * LANES, LANES)
        tops = []
        for p in range(2):
            s_lg = s_sc[p].at[lg]
            tiles = [s_lg[pl.ds(i, PEER_HEADS, stride=S_PITCH), :] for i in range(PEER_NKEYS)]
            tops.append(_top16_desc(tiles))
        a, b = tops
        cands = []
        for r1 in range(PEER_TOPK):
            for r2 in range(PEER_TOPK // (r1 + 1)):
                cands.append(b[r2] + a[r1])
        c = _top16_desc(cands)
        tau = c[PEER_TOPK - 1]
        z = jnp.zeros_like(tau)
        for kk in range(PEER_TOPK):
            z = z + jnp.exp(c[kk] - c[0])
        zinv = 1.0 / z
        for h in range(PEER_HEADS):
            rows_in = slice(h * S_PITCH, h * S_PITCH + PEER_NKEYS)
            rows_out = slice(h * PEER_NKEYS, (h + 1) * PEER_NKEYS)
            s1 = sa_sc[lg, rows_in, :]
            s2 = sb_sc[lg, rows_in, :]
            tau_h = tau[h:h + 1, :]
            thr = jnp.full_like(s1, jnp.inf)
            for r in range(PEER_TOPK):
                cand = b[r][h:h + 1, :]
                thr = jnp.where((s1 + cand) >= tau_h, cand, thr)
            thr_ref[lg, rows_out, :] = thr
            s2_ref[lg, rows_out, :] = s2
            c1_ref[lg, rows_out, :] = jnp.exp(s1 - a[0][h:h + 1, :]) * zinv[h:h + 1, :]
            e2_ref[lg, rows_out, :] = jnp.exp(s2 - b[0][h:h + 1, :])


def _mod_spec(arr, tn, tiles_per_row):
    if arr.ndim == 3:
        return pl.BlockSpec((1, 1, D_MODEL), lambda i, *_: (i // tiles_per_row, 0, 0))
    return pl.BlockSpec((tn, D_MODEL), lambda i, *_: (i, 0))


def _peer_select(x1, sh2, sc2, lw, *, tn, tiles_per_row):
    n_tok = x1.shape[0]
    assert n_tok % tn == 0
    sel_rows = PEER_HEADS * PEER_NKEYS
    n_lg = tn // LANES
    big = lambda: pl.BlockSpec((n_lg, sel_rows, LANES), lambda i: (i, 0, 0))
    return pl.pallas_call(
        functools.partial(_peer_select_kernel, tn=tn),
        grid=(n_tok // tn,),
        in_specs=[
            pl.BlockSpec((tn, D_MODEL), lambda i: (i, 0)),
            _mod_spec(sh2, tn, tiles_per_row),
            _mod_spec(sc2, tn, tiles_per_row),
            pl.BlockSpec((2 * PEER_HEADS * PEER_DK, D_MODEL), lambda i: (0, 0)),
            pl.BlockSpec((2 * PEER_HEADS, PEER_NKEYS, PEER_DK), lambda i: (0, 0, 0)),
        ],
        out_specs=[big(), big(), big(), big()],
        out_shape=[jax.ShapeDtypeStruct((n_tok // LANES, sel_rows, LANES), F32)] * 4,
        scratch_shapes=[
            pltpu.VMEM((2 * PEER_HEADS * PEER_DK, tn), BF16),
            pltpu.VMEM((tn // LANES, PEER_HEADS * S_PITCH, LANES), F32),
            pltpu.VMEM((tn // LANES, PEER_HEADS * S_PITCH, LANES), F32),
        ],
        compiler_params=pltpu.CompilerParams(
            dimension_semantics=("arbitrary",), vmem_limit_bytes=VMEM_LIMIT_BYTES),
        name="peer_select",
    )(x1, sh2, sc2, lw["w_pq_t"], lw["peer_keys"])


GATE_SUB = 32
PEER_STAGE = 2 * PEER_NKEYS
TILES_PER_STEP = 2


def _peer_dense_kernel(x_ref, sh_ref, sc_ref, g_ref, thr_ref, c1_ref, s2_ref, e2_ref,
                       ua_ref, ub_ref, vta_ref, vtb_prev_ref, lng_ref, lnb_ref, o_ref,
                       xb_sc, acc_sc, at_a, at_b, coef_a, coef_b, thrb_sc, c1b_sc, *, tn, te):
    j = pl.program_id(1)
    n_j = pl.num_programs(1) - 1
    stage = at_a.shape[0]
    n_st = te // stage
    keys_per_tile = te // PEER_NKEYS
    keys_per_step = TILES_PER_STEP * keys_per_tile
    out_chunk = D_MODEL // n_st
    assert n_st % 2 == 0

    @pl.when(j == 0)
    def _():
        sh = sh_ref[...].reshape(-1, D_MODEL)
        sc = sc_ref[...].reshape(-1, D_MODEL)
        xb_sc[...] = (x_ref[...] * (1.0 + sc) + sh).T.astype(BF16)
        acc_sc[...] = jnp.zeros_like(acc_sc)
        coef_b[...] = jnp.zeros_like(coef_b)

    def fill_key_tables():
        for lg in range(tn // LANES):
            for h in range(PEER_HEADS):
                krows = pl.ds(pl.multiple_of(h * PEER_NKEYS + j * keys_per_step, keys_per_step), keys_per_step)
                thr_t = thr_ref[lg, krows, :]
                c1_t = c1_ref[lg, krows, :]
                for k in range(keys_per_step):
                    thrb_sc[k, lg * PEER_HEADS + h] = jnp.broadcast_to(thr_t[k:k + 1, :], (SUBLANES, LANES))
                    c1b_sc[k, lg * PEER_HEADS + h] = jnp.broadcast_to(c1_t[k:k + 1, :], (SUBLANES, LANES))

    def scores(u_ref, st):
        rows = pl.ds(pl.multiple_of(st * stage, stage), stage)
        return jnp.dot(u_ref[rows, :], xb_sc[...], preferred_element_type=F32)

    def values_chunk(vt_ref, coef_ref, st):
        rows = pl.ds(pl.multiple_of(st * out_chunk, out_chunk), out_chunk)
        acc_sc[rows, :] += jnp.dot(vt_ref[rows, :], coef_ref[...], preferred_element_type=F32)

    def gate_stage(key_base, st, at_ref, coef_ref):
        n_keys = stage // PEER_NKEYS
        gshape = (GATE_SUB // SUBLANES, SUBLANES, LANES)
        zero_dep = jnp.zeros((SUBLANES, LANES), F32)
        for lg in range(tn // LANES):
            lanes = slice(lg * LANES, (lg + 1) * LANES)
            for sub in range(PEER_NKEYS // GATE_SUB):
                gates = [jnp.broadcast_to(zero_dep[None], gshape) for _ in range(n_keys)]
                for h in range(PEER_HEADS):
                    hrows = slice(h * PEER_NKEYS + sub * GATE_SUB, h * PEER_NKEYS + (sub + 1) * GATE_SUB)
                    s2 = s2_ref[lg, hrows, :].reshape(gshape)
                    e2 = e2_ref[lg, hrows, :].reshape(gshape)
                    for qq in range(n_keys):
                        thrb = thrb_sc[key_base + st * n_keys + qq, lg * PEER_HEADS + h]
                        c1b = c1b_sc[key_base + st * n_keys + qq, lg * PEER_HEADS + h]
                        gates[qq] = gates[qq] + jnp.where(s2 >= thrb, e2 * c1b, 0.0)
                for qq in range(n_keys):
                    r0 = qq * PEER_NKEYS + sub * GATE_SUB
                    a = at_ref[r0:r0 + GATE_SUB, lanes]
                    act = 0.5 * a * (1.0 + lax.erf(a * (2.0 ** -0.5)))
                    out_rows = pl.ds(pl.multiple_of(st * stage + r0, GATE_SUB), GATE_SUB)
                    coef = gates[qq].reshape(GATE_SUB, LANES) * act
                    coef_ref[out_rows, lanes] = coef.astype(BF16)
                bits = lax.bitcast_convert_type(coef[0:SUBLANES, :], jnp.uint32)
                bits = lax.shift_right_logical(lax.shift_right_logical(bits, jnp.uint32(16)), jnp.uint32(16))
                zero_dep = lax.bitcast_convert_type(bits, F32)

    def tile_phase(u_ref, key_base, coef_cur, vt_prev_ref, coef_prev):
        at_a[...] = scores(u_ref, 0)

        def stage_pair(k, carry):
            for half, (at_cur, at_nxt) in enumerate(((at_a, at_b), (at_b, at_a))):
                st = 2 * k + half
                gate_stage(key_base, st, at_cur, coef_cur)
                at_nxt[...] = scores(u_ref, jnp.minimum(st + 1, n_st - 1))
                values_chunk(vt_prev_ref, coef_prev, st)
            return carry

        lax.fori_loop(0, n_st // 2, stage_pair, 0)

    @pl.when(j < n_j)
    def _():
        fill_key_tables()
        tile_phase(ua_ref, 0, coef_a, vtb_prev_ref, coef_b)
        tile_phase(ub_ref, keys_per_tile, coef_b, vta_ref, coef_a)

    @pl.when(j == n_j)
    def _():
        for st in range(n_st):
            values_chunk(vtb_prev_ref, coef_b, st)
        g2 = g_ref[...].reshape(-1, D_MODEL)
        z = ALPHA * x_ref[...] + g2 * acc_sc[...].T
        o_ref[...] = _layer_norm(z) * lng_ref[...] + lnb_ref[...]


def _peer_dense(x1, sh2, sc2, g2, sel, lw, *, tn, te, tiles_per_row):
    n_tok = x1.shape[0]
    assert n_tok % tn == 0
    sel_rows = PEER_HEADS * PEER_NKEYS
    n_lg = tn // LANES
    thr, c1, s2, e2 = sel
    big = lambda: pl.BlockSpec((n_lg, sel_rows, LANES), lambda i, j: (i, 0, 0))
    n_j = PEER_EXPERTS // (TILES_PER_STEP * te)
    tile_a = lambda j: TILES_PER_STEP * jnp.minimum(j, n_j - 1)
    table = pltpu.VMEM((TILES_PER_STEP * te // PEER_NKEYS, n_lg * PEER_HEADS, SUBLANES, LANES), F32)
    return pl.pallas_call(
        functools.partial(_peer_dense_kernel, tn=tn, te=te),
        grid=(n_tok // tn, n_j + 1),
        in_specs=[
            pl.BlockSpec((tn, D_MODEL), lambda i, j: (i, 0)),
            _mod_spec(sh2, tn, tiles_per_row),
            _mod_spec(sc2, tn, tiles_per_row),
            _mod_spec(g2, tn, tiles_per_row),
            big(), big(), big(), big(),
            pl.BlockSpec((te, D_MODEL), lambda i, j: (tile_a(j), 0)),
            pl.BlockSpec((te, D_MODEL), lambda i, j: (tile_a(j) + 1, 0)),
            pl.BlockSpec((D_MODEL, te), lambda i, j: (0, tile_a(j))),
            pl.BlockSpec((D_MODEL, te), lambda i, j: (0, jnp.maximum(TILES_PER_STEP * j - 1, 0))),
            pl.BlockSpec((1, D_MODEL), lambda i, j: (0, 0)),
            pl.BlockSpec((1, D_MODEL), lambda i, j: (0, 0)),
        ],
        out_specs=pl.BlockSpec((tn, D_MODEL), lambda i, j: (i, 0)),
        out_shape=jax.ShapeDtypeStruct((n_tok, D_MODEL), F32),
        scratch_shapes=[
            pltpu.VMEM((D_MODEL, tn), BF16),
            pltpu.VMEM((D_MODEL, tn), F32),
            pltpu.VMEM((PEER_STAGE, tn), F32),
            pltpu.VMEM((PEER_STAGE, tn), F32),
            pltpu.VMEM((te, tn), BF16),
            pltpu.VMEM((te, tn), BF16),
            table, table,
        ],
        compiler_params=pltpu.CompilerParams(
            dimension_semantics=("arbitrary", "arbitrary"), vmem_limit_bytes=VMEM_LIMIT_BYTES),
        name="peer_dense",
    )(x1, sh2, sc2, g2, thr, c1, s2, e2, lw["peer_u"], lw["peer_u"], lw["peer_v_t"], lw["peer_v_t"],
      lw["ln2_g"], lw["ln2_b"])


def _prep_layer(l, w_in, b_gate, mh_g, sgu_g, sgu_b, w_s, b_s, w_pool, pool_scale, w_o, ln1_g, ln1_b,
                w_pq, peer_keys, peer_u, peer_v, ln2_g, ln2_b, sample_rows):
    n_gate = 2 * MLSTM_HEADS
    g0 = 4 * MLSTM_WIDTH
    w = w_in[l]
    w_gate = jnp.pad(w[:, g0:g0 + n_gate], ((0, 0), (0, LANES - n_gate)))
    w_in_p = jnp.concatenate([w[:, :g0], w_gate, w[:, g0 + n_gate:]], axis=1).astype(BF16)
    bs_full = jnp.repeat(jnp.swapaxes(b_s[l], 0, 1), SGU_DH, axis=1)
    reps = SGU_CHUNK // sample_rows
    eye = jnp.eye(len(POOL_WINDOWS), dtype=F32)
    dg = POOL_WIDTH // len(POOL_WINDOWS)
    w_pool_bd = (eye[:, None, :, None] * w_pool[l][:, :, None, :]).reshape(POOL_WIDTH, POOL_WIDTH)
    row = lambda a: a.reshape(1, -1)
    return {
        "w_in": w_in_p,
        "w_gate": w_gate,
        "b_gate": jnp.pad(b_gate[l], (0, LANES - n_gate)).reshape(1, LANES),
        "mh_g": row(mh_g[l]), "sgu_g": row(sgu_g[l]), "sgu_b": row(sgu_b[l]),
        "w_s": w_s[l],
        "b_s": bs_full,
        "w_s_sample": jnp.tile(w_s[l][:, :sample_rows, :sample_rows], (1, reps, reps)),
        "b_s_sample": jnp.tile(bs_full[:sample_rows], (reps, 1)),
        "w_pool": w_pool_bd.astype(BF16),
        "pool_scale": row(pool_scale[l]),
        "w_o": w_o[l].astype(BF16),
        "ln1_g": row(ln1_g[l]), "ln1_b": row(ln1_b[l]),
        "w_pq_t": w_pq[l].T.astype(BF16),
        "peer_keys": peer_keys[l].reshape(2 * PEER_HEADS, PEER_NKEYS, PEER_DK).astype(BF16),
        "peer_u": peer_u[l].astype(BF16),
        "peer_v_t": peer_v[l].T.astype(BF16),
        "ln2_g": row(ln2_g[l]), "ln2_b": row(ln2_b[l]),
    }


def kernel(x_prompt, x_sample, state_mlstm_C, state_mlstm_n, state_mlstm_m, state_pool, c_prompt, c_sample,
           w_ada, b_ada, w_in, b_gate, mh_g, sgu_g, sgu_b, w_s, b_s, w_pool, pool_scale, w_o, ln1_g, ln1_b,
           w_pq, peer_keys, peer_u, peer_v, ln2_g, ln2_b):
    bp, seq, _ = x_prompt.shape
    bs, dec_seq, _ = x_sample.shape
    sample_rows = SUBLANES
    prompt_rows = min(seq, 256)
    peer_tn = 512
    peer_te = 1024
    sel_tn = 256

    ada = _ada(jnp.concatenate([c_prompt, c_sample], axis=0), w_ada, b_ada)

    xp = x_prompt
    xs = jnp.pad(x_sample, ((0, 0), (0, sample_rows - dec_seq), (0, 0)))
    zeros_c = jnp.zeros((bp, MLSTM_HEADS, MLSTM_DH, MLSTM_DH), F32)
    zeros_n = jnp.zeros((bp, MLSTM_HEADS, MLSTM_DH), F32)
    zeros_m = jnp.zeros((bp, MLSTM_HEADS, LANES), F32)
    zeros_hist = jnp.zeros((bp, POOL_HIST, POOL_WIDTH), F32)
    outs = [[] for _ in range(9)]
    for l in range(DEPTH):
        lw = _prep_layer(l, w_in, b_gate, mh_g, sgu_g, sgu_b, w_s, b_s, w_pool, pool_scale, w_o, ln1_g, ln1_b,
                         w_pq, peer_keys, peer_u, peer_v, ln2_g, ln2_b, sample_rows)
        mods = jnp.pad(ada[l].reshape(bp + bs, N_ADA, D_MODEL), ((0, 0), (0, SUBLANES - N_ADA), (0, 0)))
        mod_p, mod_s = mods[:bp], mods[bp:]

        x1p, cp, np_, mp, tailp = _mixer(
            xp, mod_p, zeros_c, zeros_n, zeros_m, zeros_hist, lw,
            nb=1, rows=prompt_rows, chunk=SGU_CHUNK, n_valid=SGU_CHUNK, sgu_blk=SGU_CHUNK, pos0=0,
            want_vn=False)
        m_in = jnp.broadcast_to(state_mlstm_m[l][:, :, None], (bs, MLSTM_HEADS, LANES))
        hist_s = jnp.pad(state_pool[l], ((0, 0), (1, 0), (0, 0)))
        x1s, cs, ns, ms, tails, vns = _mixer(
            xs, mod_s, state_mlstm_C[l], state_mlstm_n[l], m_in, hist_s, lw,
            nb=16, rows=sample_rows, chunk=sample_rows, n_valid=dec_seq, sgu_blk=sample_rows, pos0=PAST_LEN,
            want_vn=True)

        x1p2 = x1p.reshape(bp * seq, D_MODEL)
        shp, scp, gp = mod_p[:, 3:4], mod_p[:, 4:5], mod_p[:, 5:6]
        sel = _peer_select(x1p2, shp, scp, lw, tn=sel_tn, tiles_per_row=seq // sel_tn)
        xp = _peer_dense(x1p2, shp, scp, gp, sel, lw, tn=peer_tn, te=peer_te,
                         tiles_per_row=seq // peer_tn).reshape(bp, seq, D_MODEL)
        x1s2 = x1s[:, :dec_seq].reshape(bs * dec_seq, D_MODEL)
        shs, scs, gs = (jnp.repeat(mod_s[:, i], dec_seq, axis=0) for i in (3, 4, 5))
        n_s = bs * dec_seq
        sel = _peer_select(x1s2, shs, scs, lw, tn=min(sel_tn, n_s), tiles_per_row=1)
        xs_new = _peer_dense(x1s2, shs, scs, gs, sel, lw, tn=min(peer_tn, n_s), te=peer_te, tiles_per_row=1)
        xs_new = xs_new.reshape(bs, dec_seq, D_MODEL)
        xs = jnp.pad(xs_new, ((0, 0), (0, sample_rows - dec_seq), (0, 0)))

        for lst, val in zip(outs, (cp, np_, mp[:, :, 0], tailp[:, 1:], cs, ns, ms[:, :, 0], tails[:, 1:],
                                   vns[:, :dec_seq])):
            lst.append(val)
    return (xp, xs_new) + tuple(jnp.stack(o) for o in outs)
```

```python
import functools

import jax
import jax.numpy as jnp
import numpy as np
from jax import lax
from jax.experimental import pallas as pl
from jax.experimental.pallas import tpu as pltpu

F32 = jnp.float32
BF16 = jnp.bfloat16
HIGHEST = lax.Precision.HIGHEST

D_MODEL = 1024
DEPTH = 2
N_ADA = 6
MLSTM_HEADS = 4
MLSTM_DH = 128
MLSTM_WIDTH = MLSTM_HEADS * MLSTM_DH
SGU_WIDTH = 256
SGU_HEADS = 4
SGU_DH = SGU_WIDTH // SGU_HEADS
SGU_CHUNK = 128
POOL_WIDTH = 256
POOL_WINDOWS = (2, 4, 8, 16)
POOL_HIST = 16
PEER_HEADS = 8
PEER_NKEYS = 128
PEER_TOPK = 16
PEER_DK = 128
PEER_EXPERTS = PEER_NKEYS * PEER_NKEYS
PAST_LEN = 16384
ALPHA = (2 * DEPTH) ** 0.25
LN_EPS = 1e-5

LANES = 128
SUBLANES = 8
VMEM_LIMIT_BYTES = 56 * 1024 * 1024

COL_Q = 0
COL_K = MLSTM_WIDTH
COL_V = 2 * MLSTM_WIDTH
COL_O = 3 * MLSTM_WIDTH
COL_GATE = 4 * MLSTM_WIDTH
COL_U = COL_GATE + LANES
COL_VS = COL_U + SGU_WIDTH
COL_P = COL_VS + SGU_WIDTH
COL_GATE_LO = COL_P + POOL_WIDTH
IN_COLS_PADDED = COL_GATE_LO + LANES

NT_DIMS = (((1,), (1,)), ((), ()))
TN_DIMS = (((0,), (0,)), ((), ()))


def _layer_norm(x):
    mu = jnp.mean(x, axis=-1, keepdims=True)
    d = x - mu
    var = jnp.mean(d * d, axis=-1, keepdims=True)
    return d * lax.rsqrt(var + LN_EPS)


def _ada_kernel(c_ref, w_ref, b_ref, o_ref):
    c = c_ref[...]
    s = c * jax.nn.sigmoid(c)
    o_ref[0] = jnp.dot(s, w_ref[0], precision=HIGHEST, preferred_element_type=F32) + b_ref[0]


def _ada(c_all, w_ada, b_ada):
    rows = c_all.shape[0]
    cols = w_ada.shape[-1]
    tile = 1536
    return pl.pallas_call(
        _ada_kernel,
        grid=(DEPTH, cols // tile),
        in_specs=[
            pl.BlockSpec((rows, D_MODEL), lambda l, j: (0, 0)),
            pl.BlockSpec((1, D_MODEL, tile), lambda l, j: (l, 0, j)),
            pl.BlockSpec((1, 1, tile), lambda l, j: (l, 0, j)),
        ],
        out_specs=pl.BlockSpec((1, rows, tile), lambda l, j: (l, 0, j)),
        out_shape=jax.ShapeDtypeStruct((DEPTH, rows, cols), F32),
        compiler_params=pltpu.CompilerParams(vmem_limit_bytes=VMEM_LIMIT_BYTES),
        name="ada",
    )(c_all, w_ada, b_ada.reshape(DEPTH, 1, cols))


def _mlstm_chunk(q, k, v, ig_col, ig_row, b_col, b_row, causal, c_state, n_state, m_state):
    length = q.shape[0]
    dlog = jnp.where(causal, b_col - b_row + ig_row, -jnp.inf)
    inter = b_col + m_state
    m_t = jnp.maximum(inter, jnp.max(dlog, axis=1, keepdims=True))
    w_intra = jnp.exp(dlog - m_t)
    w_inter = jnp.exp(inter - m_t)
    qb = q.astype(BF16)
    kb = k.astype(BF16)
    scores = lax.dot_general(qb, kb, NT_DIMS, preferred_element_type=F32)
    a = w_intra * scores
    num = jnp.dot(a.astype(BF16), v.astype(BF16), preferred_element_type=F32)
    num = num + w_inter * lax.dot_general(qb, c_state.astype(BF16), NT_DIMS, preferred_element_type=F32)
    den = jnp.sum(a, axis=1, keepdims=True) + w_inter * jnp.sum(q * n_state, axis=1, keepdims=True)
    h = num / jnp.maximum(jnp.abs(den), jnp.exp(-m_t))
    b_end = b_col[length - 1:length, :]
    dend = b_end - b_col + ig_col
    m_new = jnp.maximum(b_end + m_state, jnp.max(dend, axis=0, keepdims=True))
    wc = jnp.exp(dend - m_new)
    dec = jnp.exp(b_end + m_state - m_new)
    vw = (v * wc).astype(BF16)
    c_new = dec * c_state + lax.dot_general(vw, kb, TN_DIMS, preferred_element_type=F32)
    n_new = dec * n_state + jnp.sum(k * wc, axis=0, keepdims=True)
    return h, c_new, n_new, m_new


def _mixer_kernel(x_ref, mod_ref, c_in_ref, n_in_ref, m_in_ref, hist_ref,
                  w_in_ref, wgate_ref, bgate_ref, mhg_ref, sgug_ref, sgub_ref, ws_ref, bs_ref,
                  wpool_ref, pscale_ref, wo_ref, ln1g_ref, ln1b_ref,
                  x1_ref, c_out_ref, n_out_ref, m_out_ref, tail_ref, vn_ref,
                  proj_sc, ycat_sc, c_sc, n_sc, m_sc, carry_sc, pbuf_sc,
                  *, nb, rows, chunk, n_valid, sgu_blk, pos0):
    ci = pl.program_id(1)
    n_chunks = pl.num_programs(1)
    m_rows = nb * rows

    @pl.when(ci == 0)
    def _():
        c_sc[...] = c_in_ref[...]
        n_sc[...] = n_in_ref[...]
        m_sc[...] = m_in_ref[...]
        carry_sc[...] = hist_ref[...]

    x3 = x_ref[...]
    mod = mod_ref[...]
    sh1, sc1, g1 = mod[:, 0:1, :], mod[:, 1:2, :], mod[:, 2:3, :]
    h3 = x3 * (1.0 + sc1) + sh1
    h2d = h3.reshape(m_rows, D_MODEL)
    h_hi = h2d.astype(BF16)
    proj_sc[...] = jnp.dot(h_hi, w_in_ref[...], preferred_element_type=F32)
    h_lo = (h2d - h_hi.astype(F32)).astype(BF16)
    proj_sc[:, COL_GATE:COL_GATE + LANES] = (
        proj_sc[:, COL_GATE:COL_GATE + LANES] + proj_sc[:, COL_GATE_LO:COL_GATE_LO + LANES]
        + jnp.dot(h_lo, wgate_ref[...], preferred_element_type=F32))

    r_io = lax.broadcasted_iota(jnp.int32, (chunk, chunk), 0)
    c_io = lax.broadcasted_iota(jnp.int32, (chunk, chunk), 1)
    causal = c_io <= r_io
    tri = jnp.where(causal, 1.0, 0.0).astype(F32)
    row_id = lax.broadcasted_iota(jnp.int32, (chunk, 1), 0)
    valid_col = row_id < n_valid
    bgate = bgate_ref[...]
    mhg = mhg_ref[...]

    def seq_body(s, carry):
        for j in range(rows // chunk):
            row0 = s * rows + j * chunk
            if not isinstance(row0, int):
                row0 = pl.multiple_of(row0, SUBLANES)
            rsl = pl.ds(row0, chunk)
            gates = proj_sc[rsl, COL_GATE:COL_GATE + LANES] + bgate
            lf = jnp.where(valid_col, jax.nn.log_sigmoid(gates), 0.0)
            bcum = jnp.dot(tri, lf, precision=HIGHEST, preferred_element_type=F32)
            ig_all = jnp.where(valid_col, gates, -jnp.inf)
            bcum_t = bcum.T
            ig_t = ig_all.T
            for h in range(MLSTM_HEADS):
                hs = slice(h * MLSTM_DH, (h + 1) * MLSTM_DH)
                q = proj_sc[rsl, COL_Q + h * MLSTM_DH:COL_Q + (h + 1) * MLSTM_DH] * (MLSTM_DH ** -0.5)
                k = proj_sc[rsl, COL_K + h * MLSTM_DH:COL_K + (h + 1) * MLSTM_DH]
                v = proj_sc[rsl, COL_V + h * MLSTM_DH:COL_V + (h + 1) * MLSTM_DH]
                o = proj_sc[rsl, COL_O + h * MLSTM_DH:COL_O + (h + 1) * MLSTM_DH]
                fcol = MLSTM_HEADS + h
                hh, c_new, n_new, m_new = _mlstm_chunk(
                    q, k, v,
                    ig_all[:, h:h + 1], ig_t[h:h + 1, 0:chunk],
                    bcum[:, fcol:fcol + 1], bcum_t[fcol:fcol + 1, 0:chunk],
                    causal, c_sc[s, h], n_sc[s, pl.ds(h, 1), :], m_sc[s, pl.ds(h, 1), 0:1])
                c_sc[s, h] = c_new
                n_sc[s, pl.ds(h, 1), :] = n_new
                m_sc[s, pl.ds(h, 1), :] = jnp.broadcast_to(m_new, (1, LANES))
                hn = _layer_norm(hh) * mhg[:, hs]
                ycat_sc[rsl, hs] = jax.nn.sigmoid(o) * hn
        return carry

    if nb == 1:
        seq_body(0, 0)
    else:
        lax.fori_loop(0, nb, seq_body, 0)

    u_s = proj_sc[:, COL_U:COL_U + SGU_WIDTH]
    v_s = proj_sc[:, COL_VS:COL_VS + SGU_WIDTH]
    gr = lax.broadcasted_iota(jnp.int32, (SGU_WIDTH, SGU_WIDTH), 0) // SGU_DH
    gc = lax.broadcasted_iota(jnp.int32, (SGU_WIDTH, SGU_WIDTH), 1) // SGU_DH
    avg = jnp.where(gr == gc, 1.0 / SGU_DH, 0.0).astype(F32)
    mu = jnp.dot(v_s, avg, precision=HIGHEST, preferred_element_type=F32)
    dv = v_s - mu
    var = jnp.dot(dv * dv, avg, precision=HIGHEST, preferred_element_type=F32)
    vn = dv * lax.rsqrt(var + LN_EPS) * sgug_ref[...] + sgub_ref[...]
    if vn_ref is not None:
        vn_ref[...] = vn.reshape(nb, rows, SGU_WIDTH)
    sr = lax.broadcasted_iota(jnp.int32, (SGU_CHUNK, SGU_CHUNK), 0)
    scol = lax.broadcasted_iota(jnp.int32, (SGU_CHUNK, SGU_CHUNK), 1)
    smask = (scol <= sr) & ((sr // sgu_blk) == (scol // sgu_blk))
    lane_grp = lax.broadcasted_iota(jnp.int32, (SGU_CHUNK, SGU_WIDTH), 1) // SGU_DH
    vnb = vn.astype(BF16)
    for r in range(m_rows // SGU_CHUNK):
        rs = slice(r * SGU_CHUNK, (r + 1) * SGU_CHUNK)
        mix = jnp.zeros((SGU_CHUNK, SGU_WIDTH), F32)
        for g in range(SGU_HEADS):
            wg = jnp.where(smask, ws_ref[g], 0.0).astype(BF16)
            mg = jnp.dot(wg, vnb[rs], preferred_element_type=F32)
            mix = jnp.where(lane_grp == g, mg, mix)
        ycat_sc[rs, MLSTM_WIDTH:MLSTM_WIDTH + SGU_WIDTH] = u_s[rs] * (mix + bs_ref[...])

    pbuf_sc[:, 0:POOL_HIST, :] = carry_sc[...]
    pbuf_sc[:, POOL_HIST:POOL_HIST + rows, :] = proj_sc[:, COL_P:COL_P + POOL_WIDTH].reshape(nb, rows, POOL_WIDTH)
    lane_w = lax.broadcasted_iota(jnp.int32, (1, 1, POOL_WIDTH), 2) // (POOL_WIDTH // len(POOL_WINDOWS))
    x0 = pbuf_sc[:, POOL_HIST:POOL_HIST + rows, :]
    acc = x0
    wsum = jnp.zeros_like(x0)
    for kk in range(1, max(POOL_WINDOWS)):
        acc = acc + pbuf_sc[:, POOL_HIST - kk:POOL_HIST - kk + rows, :]
        if (kk + 1) in POOL_WINDOWS:
            wsum = jnp.where(lane_w == POOL_WINDOWS.index(kk + 1), acc, wsum)
    t_io = lax.broadcasted_iota(jnp.int32, (1, rows, POOL_WIDTH), 1)
    win = jnp.left_shift(2, lane_w)
    cnt = jnp.minimum(pos0 + ci * rows + t_io + 1, win).astype(F32)
    pooled = wsum / cnt - x0
    y_c = jnp.dot(pooled.reshape(m_rows, POOL_WIDTH).astype(BF16), wpool_ref[...],
                  preferred_element_type=F32) * pscale_ref[...]
    ycat_sc[:, MLSTM_WIDTH + SGU_WIDTH:D_MODEL] = y_c
    carry_sc[...] = pbuf_sc[:, rows:rows + POOL_HIST, :]

    y = jnp.dot(ycat_sc[...].astype(BF16), wo_ref[...], preferred_element_type=F32)
    z = ALPHA * x3 + g1 * y.reshape(nb, rows, D_MODEL)
    x1_ref[...] = _layer_norm(z) * ln1g_ref[...] + ln1b_ref[...]

    @pl.when(ci == n_chunks - 1)
    def _():
        c_out_ref[...] = c_sc[...]
        n_out_ref[...] = n_sc[...]
        m_out_ref[...] = m_sc[...]
        tail_ref[...] = pbuf_sc[:, n_valid + rows - chunk:n_valid + rows - chunk + POOL_HIST, :]


def _mixer(x, mod, c0, c0_layer, n0, m0, hist, lw, *, nb, rows, chunk, n_valid, sgu_blk, pos0, want_vn):
    bsz, seq, _ = x.shape
    grid = (bsz // nb, seq // rows)
    m_rows = nb * rows
    kern = functools.partial(_mixer_kernel, nb=nb, rows=rows, chunk=chunk, n_valid=n_valid,
                             sgu_blk=sgu_blk, pos0=pos0)
    if not want_vn:
        def kern(*refs, _k=kern):
            return _k(*refs[:24], None, *refs[24:])

    def full(shape):
        return pl.BlockSpec(shape, lambda b, c: (0,) * len(shape))

    per_b3 = lambda s1, s2: pl.BlockSpec((nb, s1, s2), lambda b, c: (b, 0, 0))
    in_specs = [
        pl.BlockSpec((nb, rows, D_MODEL), lambda b, c: (b, c, 0)),
        per_b3(SUBLANES, D_MODEL),
        pl.BlockSpec((None, nb, MLSTM_HEADS, MLSTM_DH, MLSTM_DH), lambda b, c: (c0_layer, b, 0, 0, 0)),
        per_b3(MLSTM_HEADS, MLSTM_DH),
        per_b3(MLSTM_HEADS, LANES),
        per_b3(POOL_HIST, POOL_WIDTH),
        full((D_MODEL, IN_COLS_PADDED)),
        full((D_MODEL, LANES)),
        full((1, LANES)),
        full((1, MLSTM_WIDTH)),
        full((1, SGU_WIDTH)),
        full((1, SGU_WIDTH)),
        full((SGU_HEADS, SGU_CHUNK, SGU_CHUNK)),
        full((SGU_CHUNK, SGU_WIDTH)),
        full((POOL_WIDTH, POOL_WIDTH)),
        full((1, POOL_WIDTH)),
        full((D_MODEL, D_MODEL)),
        full((1, D_MODEL)),
        full((1, D_MODEL)),
    ]
    out_specs = [
        pl.BlockSpec((nb, rows, D_MODEL), lambda b, c: (b, c, 0)),
        pl.BlockSpec((nb, MLSTM_HEADS, MLSTM_DH, MLSTM_DH), lambda b, c: (b, 0, 0, 0)),
        per_b3(MLSTM_HEADS, MLSTM_DH),
        per_b3(MLSTM_HEADS, LANES),
        per_b3(POOL_HIST, POOL_WIDTH),
    ]
    out_shape = [
        jax.ShapeDtypeStruct((bsz, seq, D_MODEL), F32),
        jax.ShapeDtypeStruct((bsz, MLSTM_HEADS, MLSTM_DH, MLSTM_DH), F32),
        jax.ShapeDtypeStruct((bsz, MLSTM_HEADS, MLSTM_DH), F32),
        jax.ShapeDtypeStruct((bsz, MLSTM_HEADS, LANES), F32),
        jax.ShapeDtypeStruct((bsz, POOL_HIST, POOL_WIDTH), F32),
    ]
    if want_vn:
        out_specs.append(pl.BlockSpec((nb, rows, SGU_WIDTH), lambda b, c: (b, c, 0)))
        out_shape.append(jax.ShapeDtypeStruct((bsz, seq, SGU_WIDTH), F32))
    scratch = [
        pltpu.VMEM((m_rows, IN_COLS_PADDED), F32),
        pltpu.VMEM((m_rows, D_MODEL), F32),
        pltpu.VMEM((nb, MLSTM_HEADS, MLSTM_DH, MLSTM_DH), F32),
        pltpu.VMEM((nb, MLSTM_HEADS, MLSTM_DH), F32),
        pltpu.VMEM((nb, MLSTM_HEADS, LANES), F32),
        pltpu.VMEM((nb, POOL_HIST, POOL_WIDTH), F32),
        pltpu.VMEM((nb, POOL_HIST + rows, POOL_WIDTH), F32),
    ]
    return pl.pallas_call(
        kern,
        grid=grid,
        in_specs=in_specs,
        out_specs=out_specs,
        out_shape=out_shape,
        scratch_shapes=scratch,
        compiler_params=pltpu.CompilerParams(
            dimension_semantics=("arbitrary", "arbitrary"), vmem_limit_bytes=VMEM_LIMIT_BYTES),
        name="mixer_sample" if want_vn else "mixer_prompt",
    )(x, mod, c0, n0, m0, hist, lw["w_in"], lw["w_gate"], lw["b_gate"], lw["mh_g"], lw["sgu_g"], lw["sgu_b"],
      lw["w_s_sample"] if want_vn else lw["w_s"], lw["b_s_sample"] if want_vn else lw["b_s"],
      lw["w_pool"], lw["pool_scale"], lw["w_o"], lw["ln1_g"], lw["ln1_b"])


def _sort16_pairs():
    pairs = []

    def merge(lo, hi, r):
        step = r * 2
        if step < hi - lo:
            merge(lo, hi, step)
            merge(lo + r, hi, step)
            for i in range(lo + r, hi - r, step):
                pairs.append((i, i + r))
        else:
            pairs.append((lo, lo + r))

    def sort(lo, hi):
        if hi - lo >= 1:
            mid = lo + (hi - lo) // 2
            sort(lo, mid)
            sort(mid + 1, hi)
            merge(lo, hi, 1)

    sort(0, PEER_TOPK - 1)
    return tuple(pairs)


SORT16_PAIRS = _sort16_pairs()


def _sort16_desc(vals):
    vals = list(vals)
    for i, j in SORT16_PAIRS:
        hi = jnp.maximum(vals[i], vals[j])
        lo = jnp.minimum(vals[i], vals[j])
        vals[i], vals[j] = hi, lo
    return vals


def _merge_top16(xs, ys):
    vals = [jnp.maximum(xs[i], ys[PEER_TOPK - 1 - i]) for i in range(PEER_TOPK)]
    d = PEER_TOPK // 2
    while d >= 1:
        for i in range(PEER_TOPK):
            if (i & d) == 0:
                hi = jnp.maximum(vals[i], vals[i + d])
                lo = jnp.minimum(vals[i], vals[i + d])
                vals[i], vals[i + d] = hi, lo
        d //= 2
    return vals


def _top16_desc(tiles):
    tiles = list(tiles)
    neg = jnp.full_like(tiles[0], -jnp.inf)
    while len(tiles) % PEER_TOPK:
        tiles.append(neg)
    best = None
    for g in range(len(tiles) // PEER_TOPK):
        grp = _sort16_desc(tiles[g * PEER_TOPK:(g + 1) * PEER_TOPK])
        best = grp if best is None else _merge_top16(best, grp)
    return best


S_PITCH = PEER_NKEYS + SUBLANES


def _peer_select_kernel(x_ref, sh_ref, sc_ref, wq_ref, keys_ref,
                        thr_ref, c1_ref, s2_ref, e2_ref,
                        q_sc, sa_sc, sb_sc, *, tn):
    s_sc = (sa_sc, sb_sc)
    x = x_ref[...]
    sh = sh_ref[...].reshape(-1, D_MODEL)
    sc = sc_ref[...].reshape(-1, D_MODEL)
    h2 = (x * (1.0 + sc) + sh).astype(BF16)
    q_sc[...] = lax.dot_general(wq_ref[...], h2, NT_DIMS, preferred_element_type=F32).astype(BF16)
    for h in range(PEER_HEADS):
        for p in range(2):
            r0 = (h * 2 + p) * PEER_DK
            s_hp = jnp.dot(keys_ref[h * 2 + p], q_sc[r0:r0 + PEER_DK, :], preferred_element_type=F32)
            for lg in range(tn // LANES):
                s_sc[p][lg, h * S_PITCH:h * S_PITCH + PEER_NKEYS, :] = s_hp[:, lg * LANES:(lg + 1) * LANES]
    for lg in range(tn // LANES):
        lanes = pl.ds(lg * LANES, LANES)
        tops = []
        for p in range(2):
            s_lg = s_sc[p].at[lg]
            tiles = [s_lg[pl.ds(i, PEER_HEADS, stride=S_PITCH), :] for i in range(PEER_NKEYS)]
            tops.append(_top16_desc(tiles))
        a, b = tops
        cands = []
        for r1 in range(PEER_TOPK):
            for r2 in range(PEER_TOPK // (r1 + 1)):
                cands.append(b[r2] + a[r1])
        c = _top16_desc(cands)
        tau = c[PEER_TOPK - 1]
        z = jnp.zeros_like(tau)
        for kk in range(PEER_TOPK):
            z = z + jnp.exp(c[kk] - c[0])
        zinv = 1.0 / z
        for h in range(PEER_HEADS):
            rows_in = slice(h * S_PITCH, h * S_PITCH + PEER_NKEYS)
            rows_out = slice(h * PEER_NKEYS, (h + 1) * PEER_NKEYS)
            s1 = sa_sc[lg, rows_in, :]
            s2 = sb_sc[lg, rows_in, :]
            tau_h = tau[h:h + 1, :]
            thr = jnp.full_like(s1, jnp.inf)
            for r in range(PEER_TOPK):
                cand = b[r][h:h + 1, :]
                thr = jnp.where((s1 + cand) >= tau_h, cand, thr)
            thr_ref[lg, rows_out, :] = thr
            s2_ref[lg, rows_out, :] = s2
            c1_ref[lg, rows_out, :] = jnp.exp(s1 - a[0][h:h + 1, :]) * zinv[h:h + 1, :]
            e2_ref[lg, rows_out, :] = jnp.exp(s2 - b[0][h:h + 1, :])


def _mod_spec(arr, tn, tiles_per_row):
    if arr.ndim == 3:
        return pl.BlockSpec((1, 1, D_MODEL), lambda i, *_: (i // tiles_per_row, 0, 0))
    return pl.BlockSpec((tn, D_MODEL), lambda i, *_: (i, 0))


def _peer_select(x1, sh2, sc2, lw, *, tn, tiles_per_row):
    n_tok = x1.shape[0]
    assert n_tok % tn == 0
    sel_rows = PEER_HEADS * PEER_NKEYS
    n_lg = tn // LANES
    big = lambda: pl.BlockSpec((n_lg, sel_rows, LANES), lambda i: (i, 0, 0))
    return pl.pallas_call(
        functools.partial(_peer_select_kernel, tn=tn),
        grid=(n_tok // tn,),
        in_specs=[
            pl.BlockSpec((tn, D_MODEL), lambda i: (i, 0)),
            _mod_spec(sh2, tn, tiles_per_row),
            _mod_spec(sc2, tn, tiles_per_row),
            pl.BlockSpec((2 * PEER_HEADS * PEER_DK, D_MODEL), lambda i: (0, 0)),
            pl.BlockSpec((2 * PEER_HEADS, PEER_NKEYS, PEER_DK), lambda i: (0, 0, 0)),
        ],
        out_specs=[big(), big(), big(), big()],
        out_shape=[jax.ShapeDtypeStruct((n_tok // LANES, sel_rows, LANES), F32)] * 4,
        scratch_shapes=[
            pltpu.VMEM((2 * PEER_HEADS * PEER_DK, tn), BF16),
            pltpu.VMEM((tn // LANES, PEER_HEADS * S_PITCH, LANES), F32),
            pltpu.VMEM((tn // LANES, PEER_HEADS * S_PITCH, LANES), F32),
        ],
        compiler_params=pltpu.CompilerParams(
            dimension_semantics=("arbitrary",), vmem_limit_bytes=VMEM_LIMIT_BYTES),
        name="peer_select",
    )(x1, sh2, sc2, lw["w_pq_t"], lw["peer_keys"])


GATE_SUB = 32
PEER_STAGE = 2 * PEER_NKEYS
TILES_PER_STEP = 2


def _peer_dense_kernel(x_ref, sh_ref, sc_ref, g_ref, thr_ref, c1_ref, s2_ref, e2_ref,
                       ua_ref, ub_ref, vta_ref, vtb_prev_ref, lng_ref, lnb_ref, o_ref,
                       xb_sc, acc_sc, at_a, at_b, coef_a, coef_b, thrb_sc, c1b_sc, *, tn, te):
    j = pl.program_id(1)
    n_j = pl.num_programs(1) - 1
    stage = at_a.shape[0]
    n_st = te // stage
    keys_per_tile = te // PEER_NKEYS
    keys_per_step = TILES_PER_STEP * keys_per_tile
    out_chunk = D_MODEL // n_st
    assert n_st % 2 == 0

    @pl.when(j == 0)
    def _():
        sh = sh_ref[...].reshape(-1, D_MODEL)
        sc = sc_ref[...].reshape(-1, D_MODEL)
        xb_sc[...] = (x_ref[...] * (1.0 + sc) + sh).T.astype(BF16)
        acc_sc[...] = jnp.zeros_like(acc_sc)
        coef_b[...] = jnp.zeros_like(coef_b)

    def fill_key_tables():
        for lg in range(tn // LANES):
            for h in range(PEER_HEADS):
                krows = pl.ds(pl.multiple_of(h * PEER_NKEYS + j * keys_per_step, keys_per_step), keys_per_step)
                thr_t = thr_ref[lg, krows, :]
                c1_t = c1_ref[lg, krows, :]
                for k in range(keys_per_step):
                    thrb_sc[k, lg * PEER_HEADS + h] = jnp.broadcast_to(thr_t[k:k + 1, :], (SUBLANES, LANES))
                    c1b_sc[k, lg * PEER_HEADS + h] = jnp.broadcast_to(c1_t[k:k + 1, :], (SUBLANES, LANES))

    def scores(u_ref, st):
        rows = pl.ds(pl.multiple_of(st * stage, stage), stage)
        return jnp.dot(u_ref[rows, :], xb_sc[...], preferred_element_type=F32)

    def values_chunk(vt_ref, coef_ref, st):
        rows = pl.ds(pl.multiple_of(st * out_chunk, out_chunk), out_chunk)
        acc_sc[rows, :] += jnp.dot(vt_ref[rows, :], coef_ref[...], preferred_element_type=F32)

    def gate_stage(key_base, st, at_ref, coef_ref):
        n_keys = stage // PEER_NKEYS
        gshape = (GATE_SUB // SUBLANES, SUBLANES, LANES)
        zero_dep = jnp.zeros((SUBLANES, LANES), F32)
        for lg in range(tn // LANES):
            lanes = slice(lg * LANES, (lg + 1) * LANES)
            for sub in range(PEER_NKEYS // GATE_SUB):
                gates = [jnp.broadcast_to(zero_dep[None], gshape) for _ in range(n_keys)]
                for h in range(PEER_HEADS):
                    hrows = slice(h * PEER_NKEYS + sub * GATE_SUB, h * PEER_NKEYS + (sub + 1) * GATE_SUB)
                    s2 = s2_ref[lg, hrows, :].reshape(gshape)
                    e2 = e2_ref[lg, hrows, :].reshape(gshape)
                    for qq in range(n_keys):
                        thrb = thrb_sc[key_base + st * n_keys + qq, lg * PEER_HEADS + h]
                        c1b = c1b_sc[key_base + st * n_keys + qq, lg * PEER_HEADS + h]
                        gates[qq] = gates[qq] + jnp.where(s2 >= thrb, e2 * c1b, 0.0)
                for qq in range(n_keys):
                    r0 = qq * PEER_NKEYS + sub * GATE_SUB
                    a = at_ref[r0:r0 + GATE_SUB, lanes]
                    act = 0.5 * a * (1.0 + lax.erf(a * (2.0 ** -0.5)))
                    out_rows = pl.ds(pl.multiple_of(st * stage + r0, GATE_SUB), GATE_SUB)
                    coef = gates[qq].reshape(GATE_SUB, LANES) * act
                    coef_ref[out_rows, lanes] = coef.astype(BF16)
                bits = lax.bitcast_convert_type(coef[0:SUBLANES, :], jnp.uint32)
                bits = lax.shift_right_logical(lax.shift_right_logical(bits, jnp.uint32(16)), jnp.uint32(16))
                zero_dep = lax.bitcast_convert_type(bits, F32)

    def tile_phase(u_ref, key_base, coef_cur, vt_prev_ref, coef_prev):
        at_a[...] = scores(u_ref, 0)

        def stage_pair(k, carry):
            for half, (at_cur, at_nxt) in enumerate(((at_a, at_b), (at_b, at_a))):
                st = 2 * k + half
                gate_stage(key_base, st, at_cur, coef_cur)
                at_nxt[...] = scores(u_ref, jnp.minimum(st + 1, n_st - 1))
                values_chunk(vt_prev_ref, coef_prev, st)
            return carry

        lax.fori_loop(0, n_st // 2, stage_pair, 0)

    @pl.when(j < n_j)
    def _():
        fill_key_tables()
        tile_phase(ua_ref, 0, coef_a, vtb_prev_ref, coef_b)
        tile_phase(ub_ref, keys_per_tile, coef_b, vta_ref, coef_a)

    @pl.when(j == n_j)
    def _():
        for st in range(n_st):
            values_chunk(vtb_prev_ref, coef_b, st)
        g2 = g_ref[...].reshape(-1, D_MODEL)
        z = ALPHA * x_ref[...] + g2 * acc_sc[...].T
        o_ref[...] = _layer_norm(z) * lng_ref[...] + lnb_ref[...]


def _peer_dense(x1, sh2, sc2, g2, sel, lw, *, tn, te, tiles_per_row):
    n_tok = x1.shape[0]
    assert n_tok % tn == 0
    sel_rows = PEER_HEADS * PEER_NKEYS
    n_lg = tn // LANES
    thr, c1, s2, e2 = sel
    big = lambda: pl.BlockSpec((n_lg, sel_rows, LANES), lambda i, j: (i, 0, 0))
    n_j = PEER_EXPERTS // (TILES_PER_STEP * te)
    layer = lw["layer"]
    tile_a = lambda j: TILES_PER_STEP * jnp.minimum(j, n_j - 1)
    table = pltpu.VMEM((TILES_PER_STEP * te // PEER_NKEYS, n_lg * PEER_HEADS, SUBLANES, LANES), F32)
    return pl.pallas_call(
        functools.partial(_peer_dense_kernel, tn=tn, te=te),
        grid=(n_tok // tn, n_j + 1),
        in_specs=[
            pl.BlockSpec((tn, D_MODEL), lambda i, j: (i, 0)),
            _mod_spec(sh2, tn, tiles_per_row),
            _mod_spec(sc2, tn, tiles_per_row),
            _mod_spec(g2, tn, tiles_per_row),
            big(), big(), big(), big(),
            pl.BlockSpec((None, te, D_MODEL), lambda i, j: (layer, tile_a(j), 0)),
            pl.BlockSpec((None, te, D_MODEL), lambda i, j: (layer, tile_a(j) + 1, 0)),
            pl.BlockSpec((None, D_MODEL, te), lambda i, j: (layer, 0, tile_a(j))),
            pl.BlockSpec((None, D_MODEL, te), lambda i, j: (layer, 0, jnp.maximum(TILES_PER_STEP * j - 1, 0))),
            pl.BlockSpec((1, D_MODEL), lambda i, j: (0, 0)),
            pl.BlockSpec((1, D_MODEL), lambda i, j: (0, 0)),
        ],
        out_specs=pl.BlockSpec((tn, D_MODEL), lambda i, j: (i, 0)),
        out_shape=jax.ShapeDtypeStruct((n_tok, D_MODEL), F32),
        scratch_shapes=[
            pltpu.VMEM((D_MODEL, tn), BF16),
            pltpu.VMEM((D_MODEL, tn), F32),
            pltpu.VMEM((PEER_STAGE, tn), F32),
            pltpu.VMEM((PEER_STAGE, tn), F32),
            pltpu.VMEM((te, tn), BF16),
            pltpu.VMEM((te, tn), BF16),
            table, table,
        ],
        compiler_params=pltpu.CompilerParams(
            dimension_semantics=("arbitrary", "arbitrary"), vmem_limit_bytes=VMEM_LIMIT_BYTES),
        name="peer_dense",
    )(x1, sh2, sc2, g2, thr, c1, s2, e2, lw["peer_u"], lw["peer_u"], lw["peer_v_t"], lw["peer_v_t"],
      lw["ln2_g"], lw["ln2_b"])


def _prep_layer(l, w_in, b_gate, mh_g, sgu_g, sgu_b, w_s, b_s, w_pool, pool_scale, w_o, ln1_g, ln1_b,
                w_pq, peer_keys, ln2_g, ln2_b, sample_rows):
    n_gate = 2 * MLSTM_HEADS
    g0 = 4 * MLSTM_WIDTH
    w = w_in[l]
    w_gate = jnp.pad(w[:, g0:g0 + n_gate], ((0, 0), (0, LANES - n_gate)))
    w_gate_hi = w_gate.astype(BF16)
    w_gate_lo = (w_gate - w_gate_hi.astype(F32)).astype(BF16)
    w_in_p = jnp.concatenate(
        [w[:, :g0].astype(BF16), w_gate_hi, w[:, g0 + n_gate:].astype(BF16), w_gate_lo], axis=1)
    bs_full = jnp.repeat(jnp.swapaxes(b_s[l], 0, 1), SGU_DH, axis=1)
    reps = SGU_CHUNK // sample_rows
    eye = jnp.eye(len(POOL_WINDOWS), dtype=F32)
    dg = POOL_WIDTH // len(POOL_WINDOWS)
    w_pool_bd = (eye[:, None, :, None] * w_pool[l][:, :, None, :]).reshape(POOL_WIDTH, POOL_WIDTH)
    row = lambda a: a.reshape(1, -1)
    return {
        "w_in": w_in_p,
        "w_gate": w_gate_hi,
        "b_gate": jnp.pad(b_gate[l], (0, LANES - n_gate)).reshape(1, LANES),
        "mh_g": row(mh_g[l]), "sgu_g": row(sgu_g[l]), "sgu_b": row(sgu_b[l]),
        "w_s": w_s[l],
        "b_s": bs_full,
        "w_s_sample": jnp.tile(w_s[l][:, :sample_rows, :sample_rows], (1, reps, reps)),
        "b_s_sample": jnp.tile(bs_full[:sample_rows], (reps, 1)),
        "w_pool": w_pool_bd.astype(BF16),
        "pool_scale": row(pool_scale[l]),
        "w_o": w_o[l].astype(BF16),
        "ln1_g": row(ln1_g[l]), "ln1_b": row(ln1_b[l]),
        "w_pq_t": w_pq[l].T.astype(BF16),
        "peer_keys": peer_keys[l].reshape(2 * PEER_HEADS, PEER_NKEYS, PEER_DK).astype(BF16),
        "ln2_g": row(ln2_g[l]), "ln2_b": row(ln2_b[l]),
    }


def kernel(x_prompt, x_sample, state_mlstm_C, state_mlstm_n, state_mlstm_m, state_pool, c_prompt, c_sample,
           w_ada, b_ada, w_in, b_gate, mh_g, sgu_g, sgu_b, w_s, b_s, w_pool, pool_scale, w_o, ln1_g, ln1_b,
           w_pq, peer_keys, peer_u, peer_v, ln2_g, ln2_b):
    bp, seq, _ = x_prompt.shape
    bs, dec_seq, _ = x_sample.shape
    sample_rows = SUBLANES
    prompt_rows = min(seq, 256)
    peer_tn = 512
    peer_te = 1024
    sel_tn = 256

    ada = _ada(jnp.concatenate([c_prompt, c_sample], axis=0), w_ada, b_ada)

    xp = x_prompt
    xs = jnp.pad(x_sample, ((0, 0), (0, sample_rows - dec_seq), (0, 0)))
    zeros_c = jnp.zeros((1, bp, MLSTM_HEADS, MLSTM_DH, MLSTM_DH), F32)
    zeros_n = jnp.zeros((bp, MLSTM_HEADS, MLSTM_DH), F32)
    zeros_m = jnp.zeros((bp, MLSTM_HEADS, LANES), F32)
    zeros_hist = jnp.zeros((bp, POOL_HIST, POOL_WIDTH), F32)
    peer_u_b = peer_u.astype(BF16)
    peer_v_t = jnp.swapaxes(peer_v, 1, 2).astype(BF16)
    outs = [[] for _ in range(9)]
    for l in range(DEPTH):
        lw = _prep_layer(l, w_in, b_gate, mh_g, sgu_g, sgu_b, w_s, b_s, w_pool, pool_scale, w_o, ln1_g, ln1_b,
                         w_pq, peer_keys, ln2_g, ln2_b, sample_rows)
        lw.update(peer_u=peer_u_b, peer_v_t=peer_v_t, layer=l)
        mods = jnp.pad(ada[l].reshape(bp + bs, N_ADA, D_MODEL), ((0, 0), (0, SUBLANES - N_ADA), (0, 0)))
        mod_p, mod_s = mods[:bp], mods[bp:]

        x1p, cp, np_, mp, tailp = _mixer(
            xp, mod_p, zeros_c, 0, zeros_n, zeros_m, zeros_hist, lw,
            nb=1, rows=prompt_rows, chunk=SGU_CHUNK, n_valid=SGU_CHUNK, sgu_blk=SGU_CHUNK, pos0=0,
            want_vn=False)
        m_in = jnp.broadcast_to(state_mlstm_m[l][:, :, None], (bs, MLSTM_HEADS, LANES))
        hist_s = jnp.pad(state_pool[l], ((0, 0), (1, 0), (0, 0)))
        x1s, cs, ns, ms, tails, vns = _mixer(
            xs, mod_s, state_mlstm_C, l, state_mlstm_n[l], m_in, hist_s, lw,
            nb=16, rows=sample_rows, chunk=sample_rows, n_valid=dec_seq, sgu_blk=sample_rows, pos0=PAST_LEN,
            want_vn=True)

        x1p2 = x1p.reshape(bp * seq, D_MODEL)
        shp, scp, gp = mod_p[:, 3:4], mod_p[:, 4:5], mod_p[:, 5:6]
        sel = _peer_select(x1p2, shp, scp, lw, tn=sel_tn, tiles_per_row=seq // sel_tn)
        xp = _peer_dense(x1p2, shp, scp, gp, sel, lw, tn=peer_tn, te=peer_te,
                         tiles_per_row=seq // peer_tn).reshape(bp, seq, D_MODEL)
        x1s2 = x1s[:, :dec_seq].reshape(bs * dec_seq, D_MODEL)
        shs, scs, gs = (jnp.repeat(mod_s[:, i], dec_seq, axis=0) for i in (3, 4, 5))
        n_s = bs * dec_seq
        sel = _peer_select(x1s2, shs, scs, lw, tn=min(sel_tn, n_s), tiles_per_row=1)
        xs_new = _peer_dense(x1s2, shs, scs, gs, sel, lw, tn=min(peer_tn, n_s), te=peer_te, tiles_per_row=1)
        xs_new = xs_new.reshape(bs, dec_seq, D_MODEL)
        xs = jnp.pad(xs_new, ((0, 0), (0, sample_rows - dec_seq), (0, 0)))

        for lst, val in zip(outs, (cp, np_, mp[:, :, 0], tailp[:, 1:], cs, ns, ms[:, :, 0], tails[:, 1:],
                                   vns[:, :dec_seq])):
            lst.append(val)
    return (xp, xs_new) + tuple(jnp.stack(o) for o in outs)
```

```python
import functools

import jax
import jax.numpy as jnp
import numpy as np
from jax import lax
from jax.experimental import pallas as pl
from jax.experimental.pallas import tpu as pltpu

F32 = jnp.float32
BF16 = jnp.bfloat16
HIGHEST = lax.Precision.HIGHEST

D_MODEL = 1024
DEPTH = 2
N_ADA = 6
MLSTM_HEADS = 4
MLSTM_DH = 128
MLSTM_WIDTH = MLSTM_HEADS * MLSTM_DH
SGU_WIDTH = 256
SGU_HEADS = 4
SGU_DH = SGU_WIDTH // SGU_HEADS
SGU_CHUNK = 128
POOL_WIDTH = 256
POOL_WINDOWS = (2, 4, 8, 16)
POOL_HIST = 16
PEER_HEADS = 8
PEER_NKEYS = 128
PEER_TOPK = 16
PEER_DK = 128
PEER_EXPERTS = PEER_NKEYS * PEER_NKEYS
PAST_LEN = 16384
ALPHA = (2 * DEPTH) ** 0.25
LN_EPS = 1e-5

LANES = 128
SUBLANES = 8
VMEM_LIMIT_BYTES = 56 * 1024 * 1024

COL_Q = 0
COL_K = MLSTM_WIDTH
COL_V = 2 * MLSTM_WIDTH
COL_O = 3 * MLSTM_WIDTH
COL_GATE = 4 * MLSTM_WIDTH
COL_U = COL_GATE + LANES
COL_VS = COL_U + SGU_WIDTH
COL_P = COL_VS + SGU_WIDTH
COL_GATE_LO = COL_P + POOL_WIDTH
IN_COLS_PADDED = COL_GATE_LO + LANES

NT_DIMS = (((1,), (1,)), ((), ()))
TN_DIMS = (((0,), (0,)), ((), ()))


def _layer_norm(x):
    mu = jnp.mean(x, axis=-1, keepdims=True)
    d = x - mu
    var = jnp.mean(d * d, axis=-1, keepdims=True)
    return d * lax.rsqrt(var + LN_EPS)


def _ada_kernel(c_ref, w_ref, b_ref, o_ref):
    c = c_ref[...]
    s = c * jax.nn.sigmoid(c)
    o_ref[0] = jnp.dot(s, w_ref[0], precision=HIGHEST, preferred_element_type=F32) + b_ref[0]


def _ada(c_all, w_ada, b_ada):
    rows = c_all.shape[0]
    cols = w_ada.shape[-1]
    tile = 1536
    return pl.pallas_call(
        _ada_kernel,
        grid=(DEPTH, cols // tile),
        in_specs=[
            pl.BlockSpec((rows, D_MODEL), lambda l, j: (0, 0)),
            pl.BlockSpec((1, D_MODEL, tile), lambda l, j: (l, 0, j)),
            pl.BlockSpec((1, 1, tile), lambda l, j: (l, 0, j)),
        ],
        out_specs=pl.BlockSpec((1, rows, tile), lambda l, j: (l, 0, j)),
        out_shape=jax.ShapeDtypeStruct((DEPTH, rows, cols), F32),
        compiler_params=pltpu.CompilerParams(vmem_limit_bytes=VMEM_LIMIT_BYTES),
        name="ada",
    )(c_all, w_ada, b_ada.reshape(DEPTH, 1, cols))


def _mlstm_chunk(q, k, v, ig_col, ig_row, b_col, b_row, causal, c_state, n_state, m_state):
    length = q.shape[0]
    dlog = jnp.where(causal, b_col - b_row + ig_row, -jnp.inf)
    inter = b_col + m_state
    m_t = jnp.maximum(inter, jnp.max(dlog, axis=1, keepdims=True))
    w_intra = jnp.exp(dlog - m_t)
    w_inter = jnp.exp(inter - m_t)
    qb = q.astype(BF16)
    kb = k.astype(BF16)
    scores = lax.dot_general(qb, kb, NT_DIMS, preferred_element_type=F32)
    a = w_intra * scores
    num = jnp.dot(a.astype(BF16), v.astype(BF16), preferred_element_type=F32)
    num = num + w_inter * lax.dot_general(qb, c_state.astype(BF16), NT_DIMS, preferred_element_type=F32)
    den = jnp.sum(a, axis=1, keepdims=True) + w_inter * jnp.sum(q * n_state, axis=1, keepdims=True)
    h = num / jnp.maximum(jnp.abs(den), jnp.exp(-m_t))
    b_end = b_col[length - 1:length, :]
    dend = b_end - b_col + ig_col
    m_new = jnp.maximum(b_end + m_state, jnp.max(dend, axis=0, keepdims=True))
    wc = jnp.exp(dend - m_new)
    dec = jnp.exp(b_end + m_state - m_new)
    vw = (v * wc).astype(BF16)
    c_new = dec * c_state + lax.dot_general(vw, kb, TN_DIMS, preferred_element_type=F32)
    n_new = dec * n_state + jnp.sum(k * wc, axis=0, keepdims=True)
    return h, c_new, n_new, m_new


def _mixer_kernel(x_ref, mod_ref, c_in_ref, n_in_ref, m_in_ref, hist_ref,
                  w_in_ref, wgate_ref, bgate_ref, mhg_ref, sgug_ref, sgub_ref, ws_ref, bs_ref,
                  wpool_ref, pscale_ref, wo_ref, ln1g_ref, ln1b_ref,
                  x1_ref, c_out_ref, n_out_ref, m_out_ref, tail_ref, vn_ref,
                  proj_sc, ycat_sc, c_sc, n_sc, m_sc, carry_sc, pbuf_sc,
                  *, nb, rows, chunk, n_valid, sgu_blk, pos0):
    ci = pl.program_id(1)
    n_chunks = pl.num_programs(1)
    m_rows = nb * rows

    @pl.when(ci == 0)
    def _():
        c_sc[...] = c_in_ref[...]
        n_sc[...] = n_in_ref[...]
        m_sc[...] = m_in_ref[...]
        carry_sc[...] = hist_ref[...]

    x3 = x_ref[...]
    mod = mod_ref[...]
    sh1, sc1, g1 = mod[:, 0:1, :], mod[:, 1:2, :], mod[:, 2:3, :]
    h3 = x3 * (1.0 + sc1) + sh1
    h2d = h3.reshape(m_rows, D_MODEL)
    h_hi = h2d.astype(BF16)
    proj_sc[...] = jnp.dot(h_hi, w_in_ref[...], preferred_element_type=F32)
    h_lo = (h2d - h_hi.astype(F32)).astype(BF16)
    proj_sc[:, COL_GATE:COL_GATE + LANES] = (
        proj_sc[:, COL_GATE:COL_GATE + LANES] + proj_sc[:, COL_GATE_LO:COL_GATE_LO + LANES]
        + jnp.dot(h_lo, wgate_ref[...], preferred_element_type=F32))

    r_io = lax.broadcasted_iota(jnp.int32, (chunk, chunk), 0)
    c_io = lax.broadcasted_iota(jnp.int32, (chunk, chunk), 1)
    causal = c_io <= r_io
    tri = jnp.where(causal, 1.0, 0.0).astype(F32)
    row_id = lax.broadcasted_iota(jnp.int32, (chunk, 1), 0)
    valid_col = row_id < n_valid
    bgate = bgate_ref[...]
    mhg = mhg_ref[...]

    def seq_body(s, carry):
        for j in range(rows // chunk):
            row0 = s * rows + j * chunk
            if not isinstance(row0, int):
                row0 = pl.multiple_of(row0, SUBLANES)
            rsl = pl.ds(row0, chunk)
            gates = proj_sc[rsl, COL_GATE:COL_GATE + LANES] + bgate
            lf = jnp.where(valid_col, jax.nn.log_sigmoid(gates), 0.0)
            bcum = jnp.dot(tri, lf, precision=HIGHEST, preferred_element_type=F32)
            ig_all = jnp.where(valid_col, gates, -jnp.inf)
            bcum_t = bcum.T
            ig_t = ig_all.T
            for h in range(MLSTM_HEADS):
                hs = slice(h * MLSTM_DH, (h + 1) * MLSTM_DH)
                q = proj_sc[rsl, COL_Q + h * MLSTM_DH:COL_Q + (h + 1) * MLSTM_DH] * (MLSTM_DH ** -0.5)
                k = proj_sc[rsl, COL_K + h * MLSTM_DH:COL_K + (h + 1) * MLSTM_DH]
                v = proj_sc[rsl, COL_V + h * MLSTM_DH:COL_V + (h + 1) * MLSTM_DH]
                o = proj_sc[rsl, COL_O + h * MLSTM_DH:COL_O + (h + 1) * MLSTM_DH]
                fcol = MLSTM_HEADS + h
                hh, c_new, n_new, m_new = _mlstm_chunk(
                    q, k, v,
                    ig_all[:, h:h + 1], ig_t[h:h + 1, 0:chunk],
                    bcum[:, fcol:fcol + 1], bcum_t[fcol:fcol + 1, 0:chunk],
                    causal, c_sc[s, h], n_sc[s, pl.ds(h, 1), :], m_sc[s, pl.ds(h, 1), 0:1])
                c_sc[s, h] = c_new
                n_sc[s, pl.ds(h, 1), :] = n_new
                m_sc[s, pl.ds(h, 1), :] = jnp.broadcast_to(m_new, (1, LANES))
                hn = _layer_norm(hh) * mhg[:, hs]
                ycat_sc[rsl, hs] = jax.nn.sigmoid(o) * hn
        return carry

    if nb == 1:
        seq_body(0, 0)
    else:
        lax.fori_loop(0, nb, seq_body, 0)

    u_s = proj_sc[:, COL_U:COL_U + SGU_WIDTH]
    v_s = proj_sc[:, COL_VS:COL_VS + SGU_WIDTH]
    gr = lax.broadcasted_iota(jnp.int32, (SGU_WIDTH, SGU_WIDTH), 0) // SGU_DH
    gc = lax.broadcasted_iota(jnp.int32, (SGU_WIDTH, SGU_WIDTH), 1) // SGU_DH
    avg = jnp.where(gr == gc, 1.0 / SGU_DH, 0.0).astype(F32)
    mu = jnp.dot(v_s, avg, precision=HIGHEST, preferred_element_type=F32)
    dv = v_s - mu
    var = jnp.dot(dv * dv, avg, precision=HIGHEST, preferred_element_type=F32)
    vn = dv * lax.rsqrt(var + LN_EPS) * sgug_ref[...] + sgub_ref[...]
    if vn_ref is not None:
        vn_ref[...] = vn.reshape(nb, rows, SGU_WIDTH)
    sr = lax.broadcasted_iota(jnp.int32, (SGU_CHUNK, SGU_CHUNK), 0)
    scol = lax.broadcasted_iota(jnp.int32, (SGU_CHUNK, SGU_CHUNK), 1)
    smask = (scol <= sr) & ((sr // sgu_blk) == (scol // sgu_blk))
    lane_grp = lax.broadcasted_iota(jnp.int32, (SGU_CHUNK, SGU_WIDTH), 1) // SGU_DH
    vnb = vn.astype(BF16)
    for r in range(m_rows // SGU_CHUNK):
        rs = slice(r * SGU_CHUNK, (r + 1) * SGU_CHUNK)
        mix = jnp.zeros((SGU_CHUNK, SGU_WIDTH), F32)
        for g in range(SGU_HEADS):
            wg = jnp.where(smask, ws_ref[g], 0.0).astype(BF16)
            mg = jnp.dot(wg, vnb[rs], preferred_element_type=F32)
            mix = jnp.where(lane_grp == g, mg, mix)
        ycat_sc[rs, MLSTM_WIDTH:MLSTM_WIDTH + SGU_WIDTH] = u_s[rs] * (mix + bs_ref[...])

    pbuf_sc[:, 0:POOL_HIST, :] = carry_sc[...]
    pbuf_sc[:, POOL_HIST:POOL_HIST + rows, :] = proj_sc[:, COL_P:COL_P + POOL_WIDTH].reshape(nb, rows, POOL_WIDTH)
    lane_w = lax.broadcasted_iota(jnp.int32, (1, 1, POOL_WIDTH), 2) // (POOL_WIDTH // len(POOL_WINDOWS))
    x0 = pbuf_sc[:, POOL_HIST:POOL_HIST + rows, :]
    acc = x0
    wsum = jnp.zeros_like(x0)
    for kk in range(1, max(POOL_WINDOWS)):
        acc = acc + pbuf_sc[:, POOL_HIST - kk:POOL_HIST - kk + rows, :]
        if (kk + 1) in POOL_WINDOWS:
            wsum = jnp.where(lane_w == POOL_WINDOWS.index(kk + 1), acc, wsum)
    t_io = lax.broadcasted_iota(jnp.int32, (1, rows, POOL_WIDTH), 1)
    win = jnp.left_shift(2, lane_w)
    cnt = jnp.minimum(pos0 + ci * rows + t_io + 1, win).astype(F32)
    pooled = wsum / cnt - x0
    y_c = jnp.dot(pooled.reshape(m_rows, POOL_WIDTH).astype(BF16), wpool_ref[...],
                  preferred_element_type=F32) * pscale_ref[...]
    ycat_sc[:, MLSTM_WIDTH + SGU_WIDTH:D_MODEL] = y_c
    carry_sc[...] = pbuf_sc[:, rows:rows + POOL_HIST, :]

    y = jnp.dot(ycat_sc[...].astype(BF16), wo_ref[...], preferred_element_type=F32)
    z = ALPHA * x3 + g1 * y.reshape(nb, rows, D_MODEL)
    x1_ref[...] = _layer_norm(z) * ln1g_ref[...] + ln1b_ref[...]

    @pl.when(ci == n_chunks - 1)
    def _():
        c_out_ref[...] = c_sc[...]
        n_out_ref[...] = n_sc[...]
        m_out_ref[...] = m_sc[...]
        tail_ref[...] = pbuf_sc[:, n_valid + rows - chunk:n_valid + rows - chunk + POOL_HIST, :]


def _mixer(x, mod, c0, c0_layer, n0, m0, hist, lw, *, nb, rows, chunk, n_valid, sgu_blk, pos0, want_vn):
    bsz, seq, _ = x.shape
    grid = (bsz // nb, seq // rows)
    m_rows = nb * rows
    kern = functools.partial(_mixer_kernel, nb=nb, rows=rows, chunk=chunk, n_valid=n_valid,
                             sgu_blk=sgu_blk, pos0=pos0)
    if not want_vn:
        def kern(*refs, _k=kern):
            return _k(*refs[:24], None, *refs[24:])

    def full(shape):
        return pl.BlockSpec(shape, lambda b, c: (0,) * len(shape))

    per_b3 = lambda s1, s2: pl.BlockSpec((nb, s1, s2), lambda b, c: (b, 0, 0))
    in_specs = [
        pl.BlockSpec((nb, rows, D_MODEL), lambda b, c: (b, c, 0)),
        per_b3(SUBLANES, D_MODEL),
        pl.BlockSpec((None, nb, MLSTM_HEADS, MLSTM_DH, MLSTM_DH), lambda b, c: (c0_layer, b, 0, 0, 0)),
        per_b3(MLSTM_HEADS, MLSTM_DH),
        per_b3(MLSTM_HEADS, LANES),
        per_b3(POOL_HIST, POOL_WIDTH),
        full((D_MODEL, IN_COLS_PADDED)),
        full((D_MODEL, LANES)),
        full((1, LANES)),
        full((1, MLSTM_WIDTH)),
        full((1, SGU_WIDTH)),
        full((1, SGU_WIDTH)),
        full((SGU_HEADS, SGU_CHUNK, SGU_CHUNK)),
        full((SGU_CHUNK, SGU_WIDTH)),
        full((POOL_WIDTH, POOL_WIDTH)),
        full((1, POOL_WIDTH)),
        full((D_MODEL, D_MODEL)),
        full((1, D_MODEL)),
        full((1, D_MODEL)),
    ]
    out_specs = [
        pl.BlockSpec((nb, rows, D_MODEL), lambda b, c: (b, c, 0)),
        pl.BlockSpec((nb, MLSTM_HEADS, MLSTM_DH, MLSTM_DH), lambda b, c: (b, 0, 0, 0)),
        per_b3(MLSTM_HEADS, MLSTM_DH),
        per_b3(MLSTM_HEADS, LANES),
        per_b3(POOL_HIST, POOL_WIDTH),
    ]
    out_shape = [
        jax.ShapeDtypeStruct((bsz, seq, D_MODEL), F32),
        jax.ShapeDtypeStruct((bsz, MLSTM_HEADS, MLSTM_DH, MLSTM_DH), F32),
        jax.ShapeDtypeStruct((bsz, MLSTM_HEADS, MLSTM_DH), F32),
        jax.ShapeDtypeStruct((bsz, MLSTM_HEADS, LANES), F32),
        jax.ShapeDtypeStruct((bsz, POOL_HIST, POOL_WIDTH), F32),
    ]
    if want_vn:
        out_specs.append(pl.BlockSpec((nb, rows, SGU_WIDTH), lambda b, c: (b, c, 0)))
        out_shape.append(jax.ShapeDtypeStruct((bsz, seq, SGU_WIDTH), F32))
    scratch = [
        pltpu.VMEM((m_rows, IN_COLS_PADDED), F32),
        pltpu.VMEM((m_rows, D_MODEL), F32),
        pltpu.VMEM((nb, MLSTM_HEADS, MLSTM_DH, MLSTM_DH), F32),
        pltpu.VMEM((nb, MLSTM_HEADS, MLSTM_DH), F32),
        pltpu.VMEM((nb, MLSTM_HEADS, LANES), F32),
        pltpu.VMEM((nb, POOL_HIST, POOL_WIDTH), F32),
        pltpu.VMEM((nb, POOL_HIST + rows, POOL_WIDTH), F32),
    ]
    return pl.pallas_call(
        kern,
        grid=grid,
        in_specs=in_specs,
        out_specs=out_specs,
        out_shape=out_shape,
        scratch_shapes=scratch,
        compiler_params=pltpu.CompilerParams(
            dimension_semantics=("arbitrary", "arbitrary"), vmem_limit_bytes=VMEM_LIMIT_BYTES),
        name="mixer_sample" if want_vn else "mixer_prompt",
    )(x, mod, c0, n0, m0, hist, lw["w_in"], lw["w_gate"], lw["b_gate"], lw["mh_g"], lw["sgu_g"], lw["sgu_b"],
      lw["w_s_sample"] if want_vn else lw["w_s"], lw["b_s_sample"] if want_vn else lw["b_s"],
      lw["w_pool"], lw["pool_scale"], lw["w_o"], lw["ln1_g"], lw["ln1_b"])


def _sort16_pairs():
    pairs = []

    def merge(lo, hi, r):
        step = r * 2
        if step < hi - lo:
            merge(lo, hi, step)
            merge(lo + r, hi, step)
            for i in range(lo + r, hi - r, step):
                pairs.append((i, i + r))
        else:
            pairs.append((lo, lo + r))

    def sort(lo, hi):
        if hi - lo >= 1:
            mid = lo + (hi - lo) // 2
            sort(lo, mid)
            sort(mid + 1, hi)
            merge(lo, hi, 1)

    sort(0, PEER_TOPK - 1)
    return tuple(pairs)


SORT16_PAIRS = _sort16_pairs()


def _sort16_desc(vals):
    vals = list(vals)
    for i, j in SORT16_PAIRS:
        hi = jnp.maximum(vals[i], vals[j])
        lo = jnp.minimum(vals[i], vals[j])
        vals[i], vals[j] = hi, lo
    return vals


def _merge_top16(xs, ys):
    vals = [jnp.maximum(xs[i], ys[PEER_TOPK - 1 - i]) for i in range(PEER_TOPK)]
    d = PEER_TOPK // 2
    while d >= 1:
        for i in range(PEER_TOPK):
            if (i & d) == 0:
                hi = jnp.maximum(vals[i], vals[i + d])
                lo = jnp.minimum(vals[i], vals[i + d])
                vals[i], vals[i + d] = hi, lo
        d //= 2
    return vals


def _top16_desc(tiles):
    tiles = list(tiles)
    neg = jnp.full_like(tiles[0], -jnp.inf)
    while len(tiles) % PEER_TOPK:
        tiles.append(neg)
    best = None
    for g in range(len(tiles) // PEER_TOPK):
        grp = _sort16_desc(tiles[g * PEER_TOPK:(g + 1) * PEER_TOPK])
        best = grp if best is None else _merge_top16(best, grp)
    return best


S_PITCH = PEER_NKEYS + SUBLANES


def _peer_select_kernel(x_ref, sh_ref, sc_ref, wq_ref, keys_ref,
                        thr_ref, c1_ref, s2_ref, e2_ref,
                        q_sc, sa_sc, sb_sc, *, tn):
    s_sc = (sa_sc, sb_sc)
    x = x_ref[...]
    sh = sh_ref[...].reshape(-1, D_MODEL)
    sc = sc_ref[...].reshape(-1, D_MODEL)
    h2 = (x * (1.0 + sc) + sh).astype(BF16)
    q_sc[...] = lax.dot_general(wq_ref[...], h2, NT_DIMS, preferred_element_type=F32).astype(BF16)
    for h in range(PEER_HEADS):
        for p in range(2):
            r0 = (h * 2 + p) * PEER_DK
            s_hp = jnp.dot(keys_ref[h * 2 + p], q_sc[r0:r0 + PEER_DK, :], preferred_element_type=F32)
            for lg in range(tn // LANES):
                s_sc[p][lg, h * S_PITCH:h * S_PITCH + PEER_NKEYS, :] = s_hp[:, lg * LANES:(lg + 1) * LANES]
    for lg in range(tn // LANES):
        lanes = pl.ds(lg * LANES, LANES)
        tops = []
        for p in range(2):
            s_lg = s_sc[p].at[lg]
            tiles = [s_lg[pl.ds(i, PEER_HEADS, stride=S_PITCH), :] for i in range(PEER_NKEYS)]
            tops.append(_top16_desc(tiles))
        a, b = tops
        cands = []
        for r1 in range(PEER_TOPK):
            for r2 in range(PEER_TOPK // (r1 + 1)):
                cands.append(b[r2] + a[r1])
        c = _top16_desc(cands)
        tau = c[PEER_TOPK - 1]
        z = jnp.zeros_like(tau)
        for kk in range(PEER_TOPK):
            z = z + jnp.exp(c[kk] - c[0])
        zinv = 1.0 / z
        for h in range(PEER_HEADS):
            rows_in = slice(h * S_PITCH, h * S_PITCH + PEER_NKEYS)
            rows_out = slice(h * PEER_NKEYS, (h + 1) * PEER_NKEYS)
            s1 = sa_sc[lg, rows_in, :]
            s2 = sb_sc[lg, rows_in, :]
            tau_h = tau[h:h + 1, :]
            thr = jnp.full_like(s1, jnp.inf)
            for r in range(PEER_TOPK):
                cand = b[r][h:h + 1, :]
                thr = jnp.where((s1 + cand) >= tau_h, cand, thr)
            thr_ref[lg, rows_out, :] = thr
            s2_ref[lg, rows_out, :] = s2
            c1_ref[lg, rows_out, :] = jnp.exp(s1 - a[0][h:h + 1, :]) * zinv[h:h + 1, :]
            e2_ref[lg, rows_out, :] = jnp.exp(s2 - b[0][h:h + 1, :])


def _mod_spec(arr, tn, tiles_per_row):
    if arr.ndim == 3:
        return pl.BlockSpec((1, 1, D_MODEL), lambda i, *_: (i // tiles_per_row, 0, 0))
    return pl.BlockSpec((tn, D_MODEL), lambda i, *_: (i, 0))


def _peer_select(x1, sh2, sc2, lw, *, tn, tiles_per_row):
    n_tok = x1.shape[0]
    assert n_tok % tn == 0
    sel_rows = PEER_HEADS * PEER_NKEYS
    n_lg = tn // LANES
    big = lambda: pl.BlockSpec((n_lg, sel_rows, LANES), lambda i: (i, 0, 0))
    return pl.pallas_call(
        functools.partial(_peer_select_kernel, tn=tn),
        grid=(n_tok // tn,),
        in_specs=[
            pl.BlockSpec((tn, D_MODEL), lambda i: (i, 0)),
            _mod_spec(sh2, tn, tiles_per_row),
            _mod_spec(sc2, tn, tiles_per_row),
            pl.BlockSpec((2 * PEER_HEADS * PEER_DK, D_MODEL), lambda i: (0, 0)),
            pl.BlockSpec((2 * PEER_HEADS, PEER_NKEYS, PEER_DK), lambda i: (0, 0, 0)),
        ],
        out_specs=[big(), big(), big(), big()],
        out_shape=[jax.ShapeDtypeStruct((n_tok // LANES, sel_rows, LANES), F32)] * 4,
        scratch_shapes=[
            pltpu.VMEM((2 * PEER_HEADS * PEER_DK, tn), BF16),
            pltpu.VMEM((tn // LANES, PEER_HEADS * S_PITCH, LANES), F32),
            pltpu.VMEM((tn // LANES, PEER_HEADS * S_PITCH, LANES), F32),
        ],
        compiler_params=pltpu.CompilerParams(
            dimension_semantics=("arbitrary",), vmem_limit_bytes=VMEM_LIMIT_BYTES),
        name="peer_select",
    )(x1, sh2, sc2, lw["w_pq_t"], lw["peer_keys"])


GATE_SUB = 32
PEER_STAGE = 2 * PEER_NKEYS
TILES_PER_STEP = 2


def _peer_dense_kernel(x_ref, sh_ref, sc_ref, g_ref, thr_ref, c1_ref, s2_ref, e2_ref,
                       ua_ref, ub_ref, vta_ref, vtb_prev_ref, lng_ref, lnb_ref, o_ref,
                       xb_sc, acc_sc, at_a, at_b, coef_a, coef_b, thrb_sc, c1b_sc, *, tn, te):
    j = pl.program_id(1)
    n_j = pl.num_programs(1) - 1
    stage = at_a.shape[0]
    n_st = te // stage
    keys_per_tile = te // PEER_NKEYS
    keys_per_step = TILES_PER_STEP * keys_per_tile
    out_chunk = D_MODEL // n_st
    assert n_st % 2 == 0

    @pl.when(j == 0)
    def _():
        sh = sh_ref[...].reshape(-1, D_MODEL)
        sc = sc_ref[...].reshape(-1, D_MODEL)
        xb_sc[...] = (x_ref[...] * (1.0 + sc) + sh).T.astype(BF16)
        acc_sc[...] = jnp.zeros_like(acc_sc)
        coef_b[...] = jnp.zeros_like(coef_b)

    def fill_key_tables():
        for lg in range(tn // LANES):
            for h in range(PEER_HEADS):
                krows = pl.ds(pl.multiple_of(h * PEER_NKEYS + j * keys_per_step, keys_per_step), keys_per_step)
                thr_t = thr_ref[lg, krows, :]
                c1_t = c1_ref[lg, krows, :]
                for k in range(keys_per_step):
                    thrb_sc[k, lg * PEER_HEADS + h] = jnp.broadcast_to(thr_t[k:k + 1, :], (SUBLANES, LANES))
                    c1b_sc[k, lg * PEER_HEADS + h] = jnp.broadcast_to(c1_t[k:k + 1, :], (SUBLANES, LANES))

    def scores(u_ref, st):
        rows = pl.ds(st * stage, stage)
        return jnp.dot(u_ref[rows, :], xb_sc[...], preferred_element_type=F32)

    def values_chunk(vt_ref, coef_ref, st):
        rows = pl.ds(st * out_chunk, out_chunk)
        acc_sc[rows, :] += jnp.dot(vt_ref[rows, :], coef_ref[...], preferred_element_type=F32)

    def gate_stage(key_base, st, at_ref, coef_ref):
        n_keys = stage // PEER_NKEYS
        gshape = (GATE_SUB // SUBLANES, SUBLANES, LANES)
        zero_dep = jnp.zeros((SUBLANES, LANES), F32)
        for lg in range(tn // LANES):
            lanes = slice(lg * LANES, (lg + 1) * LANES)
            for sub in range(PEER_NKEYS // GATE_SUB):
                gates = [jnp.broadcast_to(zero_dep[None], gshape) for _ in range(n_keys)]
                for h in range(PEER_HEADS):
                    hrows = slice(h * PEER_NKEYS + sub * GATE_SUB, h * PEER_NKEYS + (sub + 1) * GATE_SUB)
                    s2 = s2_ref[lg, hrows, :].reshape(gshape)
                    e2 = e2_ref[lg, hrows, :].reshape(gshape)
                    for qq in range(n_keys):
                        thrb = thrb_sc[key_base + st * n_keys + qq, lg * PEER_HEADS + h]
                        c1b = c1b_sc[key_base + st * n_keys + qq, lg * PEER_HEADS + h]
                        gates[qq] = gates[qq] + jnp.where(s2 >= thrb, e2 * c1b, 0.0)
                for qq in range(n_keys):
                    r0 = qq * PEER_NKEYS + sub * GATE_SUB
                    a = at_ref[r0:r0 + GATE_SUB, lanes]
                    act = 0.5 * a * (1.0 + lax.erf(a * (2.0 ** -0.5)))
                    out_rows = pl.ds(st * stage + r0, GATE_SUB)
                    coef = gates[qq].reshape(GATE_SUB, LANES) * act
                    coef_ref[out_rows, lanes] = coef.astype(BF16)
                bits = lax.bitcast_convert_type(coef[0:SUBLANES, :], jnp.uint32)
                bits = lax.shift_right_logical(lax.shift_right_logical(bits, jnp.uint32(16)), jnp.uint32(16))
                zero_dep = lax.bitcast_convert_type(bits, F32)

    def tile_phase(u_ref, u_next_ref, key_base, coef_cur, vt_prev_ref, coef_prev):
        for st in range(n_st):
            at_cur, at_nxt = (at_a, at_b) if st % 2 == 0 else (at_b, at_a)
            gate_stage(key_base, st, at_cur, coef_cur)
            if st + 1 < n_st:
                at_nxt[...] = scores(u_ref, st + 1)
            elif u_next_ref is not None:
                at_nxt[...] = scores(u_next_ref, 0)
            values_chunk(vt_prev_ref, coef_prev, st)

    @pl.when(j < n_j)
    def _():
        at_a[...] = scores(ua_ref, 0)
        fill_key_tables()
        tile_phase(ua_ref, ub_ref, 0, coef_a, vtb_prev_ref, coef_b)
        tile_phase(ub_ref, None, keys_per_tile, coef_b, vta_ref, coef_a)

    @pl.when(j == n_j)
    def _():
        for st in range(n_st):
            values_chunk(vtb_prev_ref, coef_b, st)
        g2 = g_ref[...].reshape(-1, D_MODEL)
        z = ALPHA * x_ref[...] + g2 * acc_sc[...].T
        o_ref[...] = _layer_norm(z) * lng_ref[...] + lnb_ref[...]


def _peer_dense(x1, sh2, sc2, g2, sel, lw, *, tn, te, tiles_per_row):
    n_tok = x1.shape[0]
    assert n_tok % tn == 0
    sel_rows = PEER_HEADS * PEER_NKEYS
    n_lg = tn // LANES
    thr, c1, s2, e2 = sel
    big = lambda: pl.BlockSpec((n_lg, sel_rows, LANES), lambda i, j: (i, 0, 0))
    n_j = PEER_EXPERTS // (TILES_PER_STEP * te)
    layer = lw["layer"]
    tile_a = lambda j: TILES_PER_STEP * jnp.minimum(j, n_j - 1)
    table = pltpu.VMEM((TILES_PER_STEP * te // PEER_NKEYS, n_lg * PEER_HEADS, SUBLANES, LANES), F32)
    return pl.pallas_call(
        functools.partial(_peer_dense_kernel, tn=tn, te=te),
        grid=(n_tok // tn, n_j + 1),
        in_specs=[
            pl.BlockSpec((tn, D_MODEL), lambda i, j: (i, 0)),
            _mod_spec(sh2, tn, tiles_per_row),
            _mod_spec(sc2, tn, tiles_per_row),
            _mod_spec(g2, tn, tiles_per_row),
            big(), big(), big(), big(),
            pl.BlockSpec((None, te, D_MODEL), lambda i, j: (layer, tile_a(j), 0)),
            pl.BlockSpec((None, te, D_MODEL), lambda i, j: (layer, tile_a(j) + 1, 0)),
            pl.BlockSpec((None, D_MODEL, te), lambda i, j: (layer, 0, tile_a(j))),
            pl.BlockSpec((None, D_MODEL, te), lambda i, j: (layer, 0, jnp.maximum(TILES_PER_STEP * j - 1, 0))),
            pl.BlockSpec((1, D_MODEL), lambda i, j: (0, 0)),
            pl.BlockSpec((1, D_MODEL), lambda i, j: (0, 0)),
        ],
        out_specs=pl.BlockSpec((tn, D_MODEL), lambda i, j: (i, 0)),
        out_shape=jax.ShapeDtypeStruct((n_tok, D_MODEL), F32),
        scratch_shapes=[
            pltpu.VMEM((D_MODEL, tn), BF16),
            pltpu.VMEM((D_MODEL, tn), F32),
            pltpu.VMEM((PEER_STAGE, tn), F32),
            pltpu.VMEM((PEER_STAGE, tn), F32),
            pltpu.VMEM((te, tn), BF16),
            pltpu.VMEM((te, tn), BF16),
            table, table,
        ],
        compiler_params=pltpu.CompilerParams(
            dimension_semantics=("arbitrary", "arbitrary"), vmem_limit_bytes=VMEM_LIMIT_BYTES),
        name="peer_dense",
    )(x1, sh2, sc2, g2, thr, c1, s2, e2, lw["peer_u"], lw["peer_u"], lw["peer_v_t"], lw["peer_v_t"],
      lw["ln2_g"], lw["ln2_b"])


def _prep_layer(l, w_in, b_gate, mh_g, sgu_g, sgu_b, w_s, b_s, w_pool, pool_scale, w_o, ln1_g, ln1_b,
                w_pq, peer_keys, ln2_g, ln2_b, sample_rows):
    n_gate = 2 * MLSTM_HEADS
    g0 = 4 * MLSTM_WIDTH
    w = w_in[l]
    w_gate = jnp.pad(w[:, g0:g0 + n_gate], ((0, 0), (0, LANES - n_gate)))
    w_gate_hi = w_gate.astype(BF16)
    w_gate_lo = (w_gate - w_gate_hi.astype(F32)).astype(BF16)
    w_in_p = jnp.concatenate(
        [w[:, :g0].astype(BF16), w_gate_hi, w[:, g0 + n_gate:].astype(BF16), w_gate_lo], axis=1)
    bs_full = jnp.repeat(jnp.swapaxes(b_s[l], 0, 1), SGU_DH, axis=1)
    reps = SGU_CHUNK // sample_rows
    eye = jnp.eye(len(POOL_WINDOWS), dtype=F32)
    dg = POOL_WIDTH // len(POOL_WINDOWS)
    w_pool_bd = (eye[:, None, :, None] * w_pool[l][:, :, None, :]).reshape(POOL_WIDTH, POOL_WIDTH)
    row = lambda a: a.reshape(1, -1)
    return {
        "w_in": w_in_p,
        "w_gate": w_gate_hi,
        "b_gate": jnp.pad(b_gate[l], (0, LANES - n_gate)).reshape(1, LANES),
        "mh_g": row(mh_g[l]), "sgu_g": row(sgu_g[l]), "sgu_b": row(sgu_b[l]),
        "w_s": w_s[l],
        "b_s": bs_full,
        "w_s_sample": jnp.tile(w_s[l][:, :sample_rows, :sample_rows], (1, reps, reps)),
        "b_s_sample": jnp.tile(bs_full[:sample_rows], (reps, 1)),
        "w_pool": w_pool_bd.astype(BF16),
        "pool_scale": row(pool_scale[l]),
        "w_o": w_o[l].astype(BF16),
        "ln1_g": row(ln1_g[l]), "ln1_b": row(ln1_b[l]),
        "w_pq_t": w_pq[l].T.astype(BF16),
        "peer_keys": peer_keys[l].reshape(2 * PEER_HEADS, PEER_NKEYS, PEER_DK).astype(BF16),
        "ln2_g": row(ln2_g[l]), "ln2_b": row(ln2_b[l]),
    }


def kernel(x_prompt, x_sample, state_mlstm_C, state_mlstm_n, state_mlstm_m, state_pool, c_prompt, c_sample,
           w_ada, b_ada, w_in, b_gate, mh_g, sgu_g, sgu_b, w_s, b_s, w_pool, pool_scale, w_o, ln1_g, ln1_b,
           w_pq, peer_keys, peer_u, peer_v, ln2_g, ln2_b):
    bp, seq, _ = x_prompt.shape
    bs, dec_seq, _ = x_sample.shape
    sample_rows = SUBLANES
    prompt_rows = min(seq, 256)
    peer_tn = 512
    peer_te = 1024
    sel_tn = 256

    ada = _ada(jnp.concatenate([c_prompt, c_sample], axis=0), w_ada, b_ada)

    xp = x_prompt
    xs = jnp.pad(x_sample, ((0, 0), (0, sample_rows - dec_seq), (0, 0)))
    zeros_c = jnp.zeros((1, bp, MLSTM_HEADS, MLSTM_DH, MLSTM_DH), F32)
    zeros_n = jnp.zeros((bp, MLSTM_HEADS, MLSTM_DH), F32)
    zeros_m = jnp.zeros((bp, MLSTM_HEADS, LANES), F32)
    zeros_hist = jnp.zeros((bp, POOL_HIST, POOL_WIDTH), F32)
    peer_u_b = peer_u.astype(BF16)
    peer_v_t = jnp.swapaxes(peer_v, 1, 2).astype(BF16)
    outs = [[] for _ in range(9)]
    for l in range(DEPTH):
        lw = _prep_layer(l, w_in, b_gate, mh_g, sgu_g, sgu_b, w_s, b_s, w_pool, pool_scale, w_o, ln1_g, ln1_b,
                         w_pq, peer_keys, ln2_g, ln2_b, sample_rows)
        lw.update(peer_u=peer_u_b, peer_v_t=peer_v_t, layer=l)
        mods = jnp.pad(ada[l].reshape(bp + bs, N_ADA, D_MODEL), ((0, 0), (0, SUBLANES - N_ADA), (0, 0)))
        mod_p, mod_s = mods[:bp], mods[bp:]

        x1p, cp, np_, mp, tailp = _mixer(
            xp, mod_p, zeros_c, 0, zeros_n, zeros_m, zeros_hist, lw,
            nb=1, rows=prompt_rows, chunk=SGU_CHUNK, n_valid=SGU_CHUNK, sgu_blk=SGU_CHUNK, pos0=0,
            want_vn=False)
        m_in = jnp.broadcast_to(state_mlstm_m[l][:, :, None], (bs, MLSTM_HEADS, LANES))
        hist_s = jnp.pad(state_pool[l], ((0, 0), (1, 0), (0, 0)))
        x1s, cs, ns, ms, tails, vns = _mixer(
            xs, mod_s, state_mlstm_C, l, state_mlstm_n[l], m_in, hist_s, lw,
            nb=16, rows=sample_rows, chunk=sample_rows, n_valid=dec_seq, sgu_blk=sample_rows, pos0=PAST_LEN,
            want_vn=True)

        x1p2 = x1p.reshape(bp * seq, D_MODEL)
        shp, scp, gp = mod_p[:, 3:4], mod_p[:, 4:5], mod_p[:, 5:6]
        sel = _peer_select(x1p2, shp, scp, lw, tn=sel_tn, tiles_per_row=seq // sel_tn)
        xp = _peer_dense(x1p2, shp, scp, gp, sel, lw, tn=peer_tn, te=peer_te,
                         tiles_per_row=seq // peer_tn).reshape(bp, seq, D_MODEL)
        x1s2 = x1s[:, :dec_seq].reshape(bs * dec_seq, D_MODEL)
        shs, scs, gs = (jnp.repeat(mod_s[:, i], dec_seq, axis=0) for i in (3, 4, 5))
        n_s = bs * dec_seq
        sel = _peer_select(x1s2, shs, scs, lw, tn=min(sel_tn, n_s), tiles_per_row=1)
        xs_new = _peer_dense(x1s2, shs, scs, gs, sel, lw, tn=min(peer_tn, n_s), te=peer_te, tiles_per_row=1)
        xs_new = xs_new.reshape(bs, dec_seq, D_MODEL)
        xs = jnp.pad(xs_new, ((0, 0), (0, sample_rows - dec_seq), (0, 0)))

        for lst, val in zip(outs, (cp, np_, mp[:, :, 0], tailp[:, 1:], cs, ns, ms[:, :, 0], tails[:, 1:],
                                   vns[:, :dec_seq])):
            lst.append(val)
    return (xp, xs_new) + tuple(jnp.stack(o) for o in outs)
```

```python
import functools

import jax
import jax.numpy as jnp
import numpy as np
from jax import lax
from jax.experimental import pallas as pl
from jax.experimental.pallas import tpu as pltpu

F32 = jnp.float32
BF16 = jnp.bfloat16
HIGHEST = lax.Precision.HIGHEST

D_MODEL = 1024
DEPTH = 2
N_ADA = 6
MLSTM_HEADS = 4
MLSTM_DH = 128
MLSTM_WIDTH = MLSTM_HEADS * MLSTM_DH
SGU_WIDTH = 256
SGU_HEADS = 4
SGU_DH = SGU_WIDTH // SGU_HEADS
SGU_CHUNK = 128
POOL_WIDTH = 256
POOL_WINDOWS = (2, 4, 8, 16)
POOL_HIST = 16
PEER_HEADS = 8
PEER_NKEYS = 128
PEER_TOPK = 16
PEER_DK = 128
PEER_EXPERTS = PEER_NKEYS * PEER_NKEYS
PAST_LEN = 16384
ALPHA = (2 * DEPTH) ** 0.25
LN_EPS = 1e-5

LANES = 128
SUBLANES = 8
VMEM_LIMIT_BYTES = 56 * 1024 * 1024

COL_Q = 0
COL_K = MLSTM_WIDTH
COL_V = 2 * MLSTM_WIDTH
COL_O = 3 * MLSTM_WIDTH
COL_GATE = 4 * MLSTM_WIDTH
COL_U = COL_GATE + LANES
COL_VS = COL_U + SGU_WIDTH
COL_P = COL_VS + SGU_WIDTH
COL_GATE_LO = COL_P + POOL_WIDTH
IN_COLS_PADDED = COL_GATE_LO + LANES

NT_DIMS = (((1,), (1,)), ((), ()))
TN_DIMS = (((0,), (0,)), ((), ()))


def _layer_norm(x):
    mu = jnp.mean(x, axis=-1, keepdims=True)
    d = x - mu
    var = jnp.mean(d * d, axis=-1, keepdims=True)
    return d * lax.rsqrt(var + LN_EPS)


def _ada_kernel(c_ref, w_ref, b_ref, o_ref):
    c = c_ref[...]
    s = c * jax.nn.sigmoid(c)
    o_ref[0] = jnp.dot(s, w_ref[0], precision=HIGHEST, preferred_element_type=F32) + b_ref[0]


def _ada(c_all, w_ada, b_ada):
    rows = c_all.shape[0]
    cols = w_ada.shape[-1]
    tile = 1536
    return pl.pallas_call(
        _ada_kernel,
        grid=(DEPTH, cols // tile),
        in_specs=[
            pl.BlockSpec((rows, D_MODEL), lambda l, j: (0, 0)),
            pl.BlockSpec((1, D_MODEL, tile), lambda l, j: (l, 0, j)),
            pl.BlockSpec((1, 1, tile), lambda l, j: (l, 0, j)),
        ],
        out_specs=pl.BlockSpec((1, rows, tile), lambda l, j: (l, 0, j)),
        out_shape=jax.ShapeDtypeStruct((DEPTH, rows, cols), F32),
        compiler_params=pltpu.CompilerParams(vmem_limit_bytes=VMEM_LIMIT_BYTES),
        name="ada",
    )(c_all, w_ada, b_ada.reshape(DEPTH, 1, cols))


def _mlstm_chunk(q, k, v, ig_col, ig_row, b_col, b_row, causal, c_state, n_state, m_state):
    length = q.shape[0]
    dlog = jnp.where(causal, b_col - b_row + ig_row, -jnp.inf)
    inter = b_col + m_state
    m_t = jnp.maximum(inter, jnp.max(dlog, axis=1, keepdims=True))
    w_intra = jnp.exp(dlog - m_t)
    w_inter = jnp.exp(inter - m_t)
    qb = q.astype(BF16)
    kb = k.astype(BF16)
    scores = lax.dot_general(qb, kb, NT_DIMS, preferred_element_type=F32)
    a = w_intra * scores
    num = jnp.dot(a.astype(BF16), v.astype(BF16), preferred_element_type=F32)
    num = num + w_inter * lax.dot_general(qb, c_state.astype(BF16), NT_DIMS, preferred_element_type=F32)
    den = jnp.sum(a, axis=1, keepdims=True) + w_inter * jnp.sum(q * n_state, axis=1, keepdims=True)
    h = num / jnp.maximum(jnp.abs(den), jnp.exp(-m_t))
    b_end = b_col[length - 1:length, :]
    dend = b_end - b_col + ig_col
    m_new = jnp.maximum(b_end + m_state, jnp.max(dend, axis=0, keepdims=True))
    wc = jnp.exp(dend - m_new)
    dec = jnp.exp(b_end + m_state - m_new)
    vw = (v * wc).astype(BF16)
    c_new = dec * c_state + lax.dot_general(vw, kb, TN_DIMS, preferred_element_type=F32)
    n_new = dec * n_state + jnp.sum(k * wc, axis=0, keepdims=True)
    return h, c_new, n_new, m_new


def _mixer_kernel(x_ref, mod_ref, c_in_ref, n_in_ref, m_in_ref, hist_ref,
                  w_in_ref, wgate_ref, bgate_ref, mhg_ref, sgug_ref, sgub_ref, ws_ref, bs_ref,
                  wpool_ref, pscale_ref, wo_ref, ln1g_ref, ln1b_ref,
                  x1_ref, c_out_ref, n_out_ref, m_out_ref, tail_ref, vn_ref,
                  proj_sc, ycat_sc, c_sc, n_sc, m_sc, carry_sc, pbuf_sc,
                  *, nb, rows, chunk, n_valid, sgu_blk, pos0):
    ci = pl.program_id(1)
    n_chunks = pl.num_programs(1)
    m_rows = nb * rows

    @pl.when(ci == 0)
    def _():
        c_sc[...] = c_in_ref[...]
        n_sc[...] = n_in_ref[...]
        m_sc[...] = m_in_ref[...]
        carry_sc[...] = hist_ref[...]

    x3 = x_ref[...]
    mod = mod_ref[...]
    sh1, sc1, g1 = mod[:, 0:1, :], mod[:, 1:2, :], mod[:, 2:3, :]
    h3 = x3 * (1.0 + sc1) + sh1
    h2d = h3.reshape(m_rows, D_MODEL)
    h_hi = h2d.astype(BF16)
    proj_sc[...] = jnp.dot(h_hi, w_in_ref[...], preferred_element_type=F32)
    h_lo = (h2d - h_hi.astype(F32)).astype(BF16)
    proj_sc[:, COL_GATE:COL_GATE + LANES] = (
        proj_sc[:, COL_GATE:COL_GATE + LANES] + proj_sc[:, COL_GATE_LO:COL_GATE_LO + LANES]
        + jnp.dot(h_lo, wgate_ref[...], preferred_element_type=F32))

    r_io = lax.broadcasted_iota(jnp.int32, (chunk, chunk), 0)
    c_io = lax.broadcasted_iota(jnp.int32, (chunk, chunk), 1)
    causal = c_io <= r_io
    tri = jnp.where(causal, 1.0, 0.0).astype(F32)
    row_id = lax.broadcasted_iota(jnp.int32, (chunk, 1), 0)
    valid_col = row_id < n_valid
    bgate = bgate_ref[...]
    mhg = mhg_ref[...]

    def seq_body(s, carry):
        for j in range(rows // chunk):
            row0 = s * rows + j * chunk
            if not isinstance(row0, int):
                row0 = pl.multiple_of(row0, SUBLANES)
            rsl = pl.ds(row0, chunk)
            gates = proj_sc[rsl, COL_GATE:COL_GATE + LANES] + bgate
            lf = jnp.where(valid_col, jax.nn.log_sigmoid(gates), 0.0)
            bcum = jnp.dot(tri, lf, precision=HIGHEST, preferred_element_type=F32)
            ig_all = jnp.where(valid_col, gates, -jnp.inf)
            bcum_t = bcum.T
            ig_t = ig_all.T
            for h in range(MLSTM_HEADS):
                hs = slice(h * MLSTM_DH, (h + 1) * MLSTM_DH)
                q = proj_sc[rsl, COL_Q + h * MLSTM_DH:COL_Q + (h + 1) * MLSTM_DH] * (MLSTM_DH ** -0.5)
                k = proj_sc[rsl, COL_K + h * MLSTM_DH:COL_K + (h + 1) * MLSTM_DH]
                v = proj_sc[rsl, COL_V + h * MLSTM_DH:COL_V + (h + 1) * MLSTM_DH]
                o = proj_sc[rsl, COL_O + h * MLSTM_DH:COL_O + (h + 1) * MLSTM_DH]
                fcol = MLSTM_HEADS + h
                hh, c_new, n_new, m_new = _mlstm_chunk(
                    q, k, v,
                    ig_all[:, h:h + 1], ig_t[h:h + 1, 0:chunk],
                    bcum[:, fcol:fcol + 1], bcum_t[fcol:fcol + 1, 0:chunk],
                    causal, c_sc[s, h], n_sc[s, pl.ds(h, 1), :], m_sc[s, pl.ds(h, 1), 0:1])
                c_sc[s, h] = c_new
                n_sc[s, pl.ds(h, 1), :] = n_new
                m_sc[s, pl.ds(h, 1), :] = jnp.broadcast_to(m_new, (1, LANES))
                hn = _layer_norm(hh) * mhg[:, hs]
                ycat_sc[rsl, hs] = jax.nn.sigmoid(o) * hn
        return carry

    if nb == 1:
        seq_body(0, 0)
    else:
        lax.fori_loop(0, nb, seq_body, 0)

    u_s = proj_sc[:, COL_U:COL_U + SGU_WIDTH]
    v_s = proj_sc[:, COL_VS:COL_VS + SGU_WIDTH]
    gr = lax.broadcasted_iota(jnp.int32, (SGU_WIDTH, SGU_WIDTH), 0) // SGU_DH
    gc = lax.broadcasted_iota(jnp.int32, (SGU_WIDTH, SGU_WIDTH), 1) // SGU_DH
    avg = jnp.where(gr == gc, 1.0 / SGU_DH, 0.0).astype(F32)
    mu = jnp.dot(v_s, avg, precision=HIGHEST, preferred_element_type=F32)
    dv = v_s - mu
    var = jnp.dot(dv * dv, avg, precision=HIGHEST, preferred_element_type=F32)
    vn = dv * lax.rsqrt(var + LN_EPS) * sgug_ref[...] + sgub_ref[...]
    if vn_ref is not None:
        vn_ref[...] = vn.reshape(nb, rows, SGU_WIDTH)
    sr = lax.broadcasted_iota(jnp.int32, (SGU_CHUNK, SGU_CHUNK), 0)
    scol = lax.broadcasted_iota(jnp.int32, (SGU_CHUNK, SGU_CHUNK), 1)
    smask = (scol <= sr) & ((sr // sgu_blk) == (scol // sgu_blk))
    lane_grp = lax.broadcasted_iota(jnp.int32, (SGU_CHUNK, SGU_WIDTH), 1) // SGU_DH
    vnb = vn.astype(BF16)
    for r in range(m_rows // SGU_CHUNK):
        rs = slice(r * SGU_CHUNK, (r + 1) * SGU_CHUNK)
        mix = jnp.zeros((SGU_CHUNK, SGU_WIDTH), F32)
        for g in range(SGU_HEADS):
            wg = jnp.where(smask, ws_ref[g], 0.0).astype(BF16)
            mg = jnp.dot(wg, vnb[rs], preferred_element_type=F32)
            mix = jnp.where(lane_grp == g, mg, mix)
        ycat_sc[rs, MLSTM_WIDTH:MLSTM_WIDTH + SGU_WIDTH] = u_s[rs] * (mix + bs_ref[...])

    pbuf_sc[:, 0:POOL_HIST, :] = carry_sc[...]
    pbuf_sc[:, POOL_HIST:POOL_HIST + rows, :] = proj_sc[:, COL_P:COL_P + POOL_WIDTH].reshape(nb, rows, POOL_WIDTH)
    lane_w = lax.broadcasted_iota(jnp.int32, (1, 1, POOL_WIDTH), 2) // (POOL_WIDTH // len(POOL_WINDOWS))
    x0 = pbuf_sc[:, POOL_HIST:POOL_HIST + rows, :]
    acc = x0
    wsum = jnp.zeros_like(x0)
    for kk in range(1, max(POOL_WINDOWS)):
        acc = acc + pbuf_sc[:, POOL_HIST - kk:POOL_HIST - kk + rows, :]
        if (kk + 1) in POOL_WINDOWS:
            wsum = jnp.where(lane_w == POOL_WINDOWS.index(kk + 1), acc, wsum)
    t_io = lax.broadcasted_iota(jnp.int32, (1, rows, POOL_WIDTH), 1)
    win = jnp.left_shift(2, lane_w)
    cnt = jnp.minimum(pos0 + ci * rows + t_io + 1, win).astype(F32)
    pooled = wsum / cnt - x0
    y_c = jnp.dot(pooled.reshape(m_rows, POOL_WIDTH).astype(BF16), wpool_ref[...],
                  preferred_element_type=F32) * pscale_ref[...]
    ycat_sc[:, MLSTM_WIDTH + SGU_WIDTH:D_MODEL] = y_c
    carry_sc[...] = pbuf_sc[:, rows:rows + POOL_HIST, :]

    y = jnp.dot(ycat_sc[...].astype(BF16), wo_ref[...], preferred_element_type=F32)
    z = ALPHA * x3 + g1 * y.reshape(nb, rows, D_MODEL)
    x1_ref[...] = _layer_norm(z) * ln1g_ref[...] + ln1b_ref[...]

    @pl.when(ci == n_chunks - 1)
    def _():
        c_out_ref[...] = c_sc[...]
        n_out_ref[...] = n_sc[...]
        m_out_ref[...] = m_sc[...]
        tail_ref[...] = pbuf_sc[:, n_valid + rows - chunk:n_valid + rows - chunk + POOL_HIST, :]


def _mixer(x, mod, c0, c0_layer, n0, m0, hist, lw, *, nb, rows, chunk, n_valid, sgu_blk, pos0, want_vn):
    bsz, seq, _ = x.shape
    grid = (bsz // nb, seq // rows)
    m_rows = nb * rows
    kern = functools.partial(_mixer_kernel, nb=nb, rows=rows, chunk=chunk, n_valid=n_valid,
                             sgu_blk=sgu_blk, pos0=pos0)
    if not want_vn:
        def kern(*refs, _k=kern):
            return _k(*refs[:24], None, *refs[24:])

    def full(shape):
        return pl.BlockSpec(shape, lambda b, c: (0,) * len(shape))

    per_b3 = lambda s1, s2: pl.BlockSpec((nb, s1, s2), lambda b, c: (b, 0, 0))
    in_specs = [
        pl.BlockSpec((nb, rows, D_MODEL), lambda b, c: (b, c, 0)),
        per_b3(SUBLANES, D_MODEL),
        pl.BlockSpec((None, nb, MLSTM_HEADS, MLSTM_DH, MLSTM_DH), lambda b, c: (c0_layer, b, 0, 0, 0)),
        per_b3(MLSTM_HEADS, MLSTM_DH),
        per_b3(MLSTM_HEADS, LANES),
        per_b3(POOL_HIST, POOL_WIDTH),
        full((D_MODEL, IN_COLS_PADDED)),
        full((D_MODEL, LANES)),
        full((1, LANES)),
        full((1, MLSTM_WIDTH)),
        full((1, SGU_WIDTH)),
        full((1, SGU_WIDTH)),
        full((SGU_HEADS, SGU_CHUNK, SGU_CHUNK)),
        full((SGU_CHUNK, SGU_WIDTH)),
        full((POOL_WIDTH, POOL_WIDTH)),
        full((1, POOL_WIDTH)),
        full((D_MODEL, D_MODEL)),
        full((1, D_MODEL)),
        full((1, D_MODEL)),
    ]
    out_specs = [
        pl.BlockSpec((nb, rows, D_MODEL), lambda b, c: (b, c, 0)),
        pl.BlockSpec((nb, MLSTM_HEADS, MLSTM_DH, MLSTM_DH), lambda b, c: (b, 0, 0, 0)),
        per_b3(MLSTM_HEADS, MLSTM_DH),
        per_b3(MLSTM_HEADS, LANES),
        per_b3(POOL_HIST, POOL_WIDTH),
    ]
    out_shape = [
        jax.ShapeDtypeStruct((bsz, seq, D_MODEL), F32),
        jax.ShapeDtypeStruct((bsz, MLSTM_HEADS, MLSTM_DH, MLSTM_DH), F32),
        jax.ShapeDtypeStruct((bsz, MLSTM_HEADS, MLSTM_DH), F32),
        jax.ShapeDtypeStruct((bsz, MLSTM_HEADS, LANES), F32),
        jax.ShapeDtypeStruct((bsz, POOL_HIST, POOL_WIDTH), F32),
    ]
    if want_vn:
        out_specs.append(pl.BlockSpec((nb, rows, SGU_WIDTH), lambda b, c: (b, c, 0)))
        out_shape.append(jax.ShapeDtypeStruct((bsz, seq, SGU_WIDTH), F32))
    scratch = [
        pltpu.VMEM((m_rows, IN_COLS_PADDED), F32),
        pltpu.VMEM((m_rows, D_MODEL), F32),
        pltpu.VMEM((nb, MLSTM_HEADS, MLSTM_DH, MLSTM_DH), F32),
        pltpu.VMEM((nb, MLSTM_HEADS, MLSTM_DH), F32),
        pltpu.VMEM((nb, MLSTM_HEADS, LANES), F32),
        pltpu.VMEM((nb, POOL_HIST, POOL_WIDTH), F32),
        pltpu.VMEM((nb, POOL_HIST + rows, POOL_WIDTH), F32),
    ]
    return pl.pallas_call(
        kern,
        grid=grid,
        in_specs=in_specs,
        out_specs=out_specs,
        out_shape=out_shape,
        scratch_shapes=scratch,
        compiler_params=pltpu.CompilerParams(
            dimension_semantics=("arbitrary", "arbitrary"), vmem_limit_bytes=VMEM_LIMIT_BYTES),
        name="mixer_sample" if want_vn else "mixer_prompt",
    )(x, mod, c0, n0, m0, hist, lw["w_in"], lw["w_gate"], lw["b_gate"], lw["mh_g"], lw["sgu_g"], lw["sgu_b"],
      lw["w_s_sample"] if want_vn else lw["w_s"], lw["b_s_sample"] if want_vn else lw["b_s"],
      lw["w_pool"], lw["pool_scale"], lw["w_o"], lw["ln1_g"], lw["ln1_b"])


def _sort16_pairs():
    pairs = []

    def merge(lo, hi, r):
        step = r * 2
        if step < hi - lo:
            merge(lo, hi, step)
            merge(lo + r, hi, step)
            for i in range(lo + r, hi - r, step):
                pairs.append((i, i + r))
        else:
            pairs.append((lo, lo + r))

    def sort(lo, hi):
        if hi - lo >= 1:
            mid = lo + (hi - lo) // 2
            sort(lo, mid)
            sort(mid + 1, hi)
            merge(lo, hi, 1)

    sort(0, PEER_TOPK - 1)
    return tuple(pairs)


SORT16_PAIRS = _sort16_pairs()


def _sort16_desc(vals):
    vals = list(vals)
    for i, j in SORT16_PAIRS:
        hi = jnp.maximum(vals[i], vals[j])
        lo = jnp.minimum(vals[i], vals[j])
        vals[i], vals[j] = hi, lo
    return vals


def _merge_top16(xs, ys):
    vals = [jnp.maximum(xs[i], ys[PEER_TOPK - 1 - i]) for i in range(PEER_TOPK)]
    d = PEER_TOPK // 2
    while d >= 1:
        for i in range(PEER_TOPK):
            if (i & d) == 0:
                hi = jnp.maximum(vals[i], vals[i + d])
                lo = jnp.minimum(vals[i], vals[i + d])
                vals[i], vals[i + d] = hi, lo
        d //= 2
    return vals


def _top16_desc(tiles):
    tiles = list(tiles)
    neg = jnp.full_like(tiles[0], -jnp.inf)
    while len(tiles) % PEER_TOPK:
        tiles.append(neg)
    best = None
    for g in range(len(tiles) // PEER_TOPK):
        grp = _sort16_desc(tiles[g * PEER_TOPK:(g + 1) * PEER_TOPK])
        best = grp if best is None else _merge_top16(best, grp)
    return best


S_PITCH = PEER_NKEYS + SUBLANES


def _peer_select_kernel(x_ref, sh_ref, sc_ref, wq_ref, keys_ref,
                        thr_ref, c1_ref, s2_ref, e2_ref,
                        q_sc, sa_sc, sb_sc, *, tn):
    s_sc = (sa_sc, sb_sc)
    x = x_ref[...]
    sh = sh_ref[...].reshape(-1, D_MODEL)
    sc = sc_ref[...].reshape(-1, D_MODEL)
    h2 = (x * (1.0 + sc) + sh).astype(BF16)
    q_sc[...] = lax.dot_general(wq_ref[...], h2, NT_DIMS, preferred_element_type=F32).astype(BF16)
    for h in range(PEER_HEADS):
        for p in range(2):
            r0 = (h * 2 + p) * PEER_DK
            s_hp = jnp.dot(keys_ref[h * 2 + p], q_sc[r0:r0 + PEER_DK, :], preferred_element_type=F32)
            for lg in range(tn // LANES):
                s_sc[p][lg, h * S_PITCH:h * S_PITCH + PEER_NKEYS, :] = s_hp[:, lg * LANES:(lg + 1) * LANES]
    for lg in range(tn // LANES):
        lanes = pl.ds(lg * LANES, LANES)
        tops = []
        for p in range(2):
            s_lg = s_sc[p].at[lg]
            tiles = [s_lg[pl.ds(i, PEER_HEADS, stride=S_PITCH), :] for i in range(PEER_NKEYS)]
            tops.append(_top16_desc(tiles))
        a, b = tops
        cands = []
        for r1 in range(PEER_TOPK):
            for r2 in range(PEER_TOPK // (r1 + 1)):
                cands.append(b[r2] + a[r1])
        c = _top16_desc(cands)
        tau = c[PEER_TOPK - 1]
        z = jnp.zeros_like(tau)
        for kk in range(PEER_TOPK):
            z = z + jnp.exp(c[kk] - c[0])
        zinv = 1.0 / z
        for h in range(PEER_HEADS):
            rows_in = slice(h * S_PITCH, h * S_PITCH + PEER_NKEYS)
            rows_out = slice(h * PEER_NKEYS, (h + 1) * PEER_NKEYS)
            s1 = sa_sc[lg, rows_in, :]
            s2 = sb_sc[lg, rows_in, :]
            tau_h = tau[h:h + 1, :]
            thr = jnp.full_like(s1, jnp.inf)
            for r in range(PEER_TOPK):
                cand = b[r][h:h + 1, :]
                thr = jnp.where((s1 + cand) >= tau_h, cand, thr)
            thr_ref[lg, rows_out, :] = thr
            s2_ref[lg, rows_out, :] = s2
            c1_ref[lg, rows_out, :] = jnp.exp(s1 - a[0][h:h + 1, :]) * zinv[h:h + 1, :]
            e2_ref[lg, rows_out, :] = jnp.exp(s2 - b[0][h:h + 1, :])


def _mod_spec(arr, tn, tiles_per_row):
    if arr.ndim == 3:
        return pl.BlockSpec((1, 1, D_MODEL), lambda i, *_: (i // tiles_per_row, 0, 0))
    return pl.BlockSpec((tn, D_MODEL), lambda i, *_: (i, 0))


def _peer_select(x1, sh2, sc2, lw, *, tn, tiles_per_row):
    n_tok = x1.shape[0]
    assert n_tok % tn == 0
    sel_rows = PEER_HEADS * PEER_NKEYS
    n_lg = tn // LANES
    big = lambda: pl.BlockSpec((n_lg, sel_rows, LANES), lambda i: (i, 0, 0))
    return pl.pallas_call(
        functools.partial(_peer_select_kernel, tn=tn),
        grid=(n_tok // tn,),
        in_specs=[
            pl.BlockSpec((tn, D_MODEL), lambda i: (i, 0)),
            _mod_spec(sh2, tn, tiles_per_row),
            _mod_spec(sc2, tn, tiles_per_row),
            pl.BlockSpec((2 * PEER_HEADS * PEER_DK, D_MODEL), lambda i: (0, 0)),
            pl.BlockSpec((2 * PEER_HEADS, PEER_NKEYS, PEER_DK), lambda i: (0, 0, 0)),
        ],
        out_specs=[big(), big(), big(), big()],
        out_shape=[jax.ShapeDtypeStruct((n_tok // LANES, sel_rows, LANES), F32)] * 4,
        scratch_shapes=[
            pltpu.VMEM((2 * PEER_HEADS * PEER_DK, tn), BF16),
            pltpu.VMEM((tn // LANES, PEER_HEADS * S_PITCH, LANES), F32),
            pltpu.VMEM((tn // LANES, PEER_HEADS * S_PITCH, LANES), F32),
        ],
        compiler_params=pltpu.CompilerParams(
            dimension_semantics=("arbitrary",), vmem_limit_bytes=VMEM_LIMIT_BYTES),
        name="peer_select",
    )(x1, sh2, sc2, lw["w_pq_t"], lw["peer_keys"])


GATE_SUB = 32
PEER_STAGE = 2 * PEER_NKEYS
TILES_PER_STEP = 2


def _peer_dense_kernel(x_ref, sh_ref, sc_ref, g_ref, thr_ref, c1_ref, s2_ref, e2_ref,
                       ua_ref, ub_ref, vta_ref, vtb_prev_ref, lng_ref, lnb_ref, o_ref,
                       xb_sc, acc_sc, at_a, at_b, coef_a, coef_b, thrb_sc, c1b_sc, *, tn, te):
    j = pl.program_id(1)
    n_j = pl.num_programs(1) - 1
    stage = at_a.shape[0]
    n_st = te // stage
    keys_per_tile = te // PEER_NKEYS
    keys_per_step = TILES_PER_STEP * keys_per_tile
    out_chunk = D_MODEL // n_st
    assert n_st % 2 == 0

    @pl.when(j == 0)
    def _():
        sh = sh_ref[...].reshape(-1, D_MODEL)
        sc = sc_ref[...].reshape(-1, D_MODEL)
        xb_sc[...] = (x_ref[...] * (1.0 + sc) + sh).T.astype(BF16)
        acc_sc[...] = jnp.zeros_like(acc_sc)
        coef_b[...] = jnp.zeros_like(coef_b)

    def fill_key_tables():
        for lg in range(tn // LANES):
            for h in range(PEER_HEADS):
                krows = pl.ds(pl.multiple_of(h * PEER_NKEYS + j * keys_per_step, keys_per_step), keys_per_step)
                thr_t = thr_ref[lg, krows, :]
                c1_t = c1_ref[lg, krows, :]
                for k in range(keys_per_step):
                    thrb_sc[k, lg * PEER_HEADS + h] = jnp.broadcast_to(thr_t[k:k + 1, :], (SUBLANES, LANES))
                    c1b_sc[k, lg * PEER_HEADS + h] = jnp.broadcast_to(c1_t[k:k + 1, :], (SUBLANES, LANES))

    def scores(u_ref, st, zero_dep=None):
        rows = pl.ds(st * stage, stage)
        lhs = u_ref[rows, :]
        if zero_dep is not None:
            lhs = lhs + zero_dep[0:1, 0:1].astype(BF16)
        return jnp.dot(lhs, xb_sc[...], preferred_element_type=F32)

    def values_chunk(vt_ref, coef_ref, st, zero_dep=None):
        rows = pl.ds(st * out_chunk, out_chunk)
        lhs = vt_ref[rows, :]
        if zero_dep is not None:
            lhs = lhs + zero_dep[0:1, 0:1].astype(BF16)
        acc_sc[rows, :] += jnp.dot(lhs, coef_ref[...], preferred_element_type=F32)

    def zero_of(word_tile):
        bits = lax.bitcast_convert_type(word_tile, jnp.uint32)
        bits = lax.shift_right_logical(lax.shift_right_logical(bits, jnp.uint32(16)), jnp.uint32(16))
        return lax.bitcast_convert_type(bits, F32)

    def gate_stage(key_base, st, at_ref, coef_ref, zero_dep, mid_hook=None, end_hook=None):
        n_keys = stage // PEER_NKEYS
        gshape = (GATE_SUB // SUBLANES, SUBLANES, LANES)
        n_sub = PEER_NKEYS // GATE_SUB
        n_slabs = (tn // LANES) * n_sub
        for lg in range(tn // LANES):
            lanes = slice(lg * LANES, (lg + 1) * LANES)
            for sub in range(n_sub):
                if mid_hook is not None and lg * n_sub + sub == n_slabs // 2:
                    zero_dep = zero_dep + mid_hook()
                if end_hook is not None and lg * n_sub + sub == n_slabs - 1:
                    zero_dep = zero_dep + end_hook()
                gates = [jnp.broadcast_to(zero_dep[None], gshape) for _ in range(n_keys)]
                for h in range(PEER_HEADS):
                    hrows = slice(h * PEER_NKEYS + sub * GATE_SUB, h * PEER_NKEYS + (sub + 1) * GATE_SUB)
                    s2 = s2_ref[lg, hrows, :].reshape(gshape)
                    e2 = e2_ref[lg, hrows, :].reshape(gshape)
                    for qq in range(n_keys):
                        thrb = thrb_sc[key_base + st * n_keys + qq, lg * PEER_HEADS + h]
                        c1b = c1b_sc[key_base + st * n_keys + qq, lg * PEER_HEADS + h]
                        gates[qq] = gates[qq] + jnp.where(s2 >= thrb, e2 * c1b, 0.0)
                for qq in range(n_keys):
                    r0 = qq * PEER_NKEYS + sub * GATE_SUB
                    a = at_ref[r0:r0 + GATE_SUB, lanes]
                    act = 0.5 * a * (1.0 + lax.erf(a * (2.0 ** -0.5)))
                    out_rows = pl.ds(st * stage + r0, GATE_SUB)
                    coef = gates[qq].reshape(GATE_SUB, LANES) * act
                    coef_ref[out_rows, lanes] = coef.astype(BF16)
                zero_dep = zero_of(coef[0:SUBLANES, :])
        return zero_dep

    def tile_phase(u_ref, u_next_ref, key_base, coef_cur, vt_prev_ref, coef_prev, zero_dep):
        for st in range(n_st):
            at_cur, at_nxt = (at_a, at_b) if st % 2 == 0 else (at_b, at_a)
            mid_hook = None
            if st + 1 < n_st:
                at_nxt[...] = scores(u_ref, st + 1, zero_dep)
                mid_hook = functools.partial(lambda r: zero_of(r[0:SUBLANES, 0:LANES]), at_nxt)
            elif u_next_ref is not None:
                at_nxt[...] = scores(u_next_ref, 0, zero_dep)
                mid_hook = functools.partial(lambda r: zero_of(r[0:SUBLANES, 0:LANES]), at_nxt)
            values_chunk(vt_prev_ref, coef_prev, st, zero_dep)
            end_hook = functools.partial(
                lambda r0: zero_of(acc_sc[r0:r0 + SUBLANES, 0:LANES]), st * out_chunk)
            zero_dep = gate_stage(key_base, st, at_cur, coef_cur, zero_dep, mid_hook, end_hook)
        return zero_dep

    @pl.when(j < n_j)
    def _():
        at_a[...] = scores(ua_ref, 0)
        fill_key_tables()
        dep = tile_phase(ua_ref, ub_ref, 0, coef_a, vtb_prev_ref, coef_b, jnp.zeros((SUBLANES, LANES), F32))
        tile_phase(ub_ref, None, keys_per_tile, coef_b, vta_ref, coef_a, dep)

    @pl.when(j == n_j)
    def _():
        for st in range(n_st):
            values_chunk(vtb_prev_ref, coef_b, st)
        g2 = g_ref[...].reshape(-1, D_MODEL)
        z = ALPHA * x_ref[...] + g2 * acc_sc[...].T
        o_ref[...] = _layer_norm(z) * lng_ref[...] + lnb_ref[...]


def _peer_dense(x1, sh2, sc2, g2, sel, lw, *, tn, te, tiles_per_row):
    n_tok = x1.shape[0]
    assert n_tok % tn == 0
    sel_rows = PEER_HEADS * PEER_NKEYS
    n_lg = tn // LANES
    thr, c1, s2, e2 = sel
    big = lambda: pl.BlockSpec((n_lg, sel_rows, LANES), lambda i, j: (i, 0, 0))
    n_j = PEER_EXPERTS // (TILES_PER_STEP * te)
    layer = lw["layer"]
    tile_a = lambda j: TILES_PER_STEP * jnp.minimum(j, n_j - 1)
    table = pltpu.VMEM((TILES_PER_STEP * te // PEER_NKEYS, n_lg * PEER_HEADS, SUBLANES, LANES), F32)
    return pl.pallas_call(
        functools.partial(_peer_dense_kernel, tn=tn, te=te),
        grid=(n_tok // tn, n_j + 1),
        in_specs=[
            pl.BlockSpec((tn, D_MODEL), lambda i, j: (i, 0)),
            _mod_spec(sh2, tn, tiles_per_row),
            _mod_spec(sc2, tn, tiles_per_row),
            _mod_spec(g2, tn, tiles_per_row),
            big(), big(), big(), big(),
            pl.BlockSpec((None, te, D_MODEL), lambda i, j: (layer, tile_a(j), 0)),
            pl.BlockSpec((None, te, D_MODEL), lambda i, j: (layer, tile_a(j) + 1, 0)),
            pl.BlockSpec((None, D_MODEL, te), lambda i, j: (layer, 0, tile_a(j))),
            pl.BlockSpec((None, D_MODEL, te), lambda i, j: (layer, 0, jnp.maximum(TILES_PER_STEP * j - 1, 0))),
            pl.BlockSpec((1, D_MODEL), lambda i, j: (0, 0)),
            pl.BlockSpec((1, D_MODEL), lambda i, j: (0, 0)),
        ],
        out_specs=pl.BlockSpec((tn, D_MODEL), lambda i, j: (i, 0)),
        out_shape=jax.ShapeDtypeStruct((n_tok, D_MODEL), F32),
        scratch_shapes=[
            pltpu.VMEM((D_MODEL, tn), BF16),
            pltpu.VMEM((D_MODEL, tn), F32),
            pltpu.VMEM((PEER_STAGE, tn), F32),
            pltpu.VMEM((PEER_STAGE, tn), F32),
            pltpu.VMEM((te, tn), BF16),
            pltpu.VMEM((te, tn), BF16),
            table, table,
        ],
        compiler_params=pltpu.CompilerParams(
            dimension_semantics=("arbitrary", "arbitrary"), vmem_limit_bytes=VMEM_LIMIT_BYTES),
        name="peer_dense",
    )(x1, sh2, sc2, g2, thr, c1, s2, e2, lw["peer_u"], lw["peer_u"], lw["peer_v_t"], lw["peer_v_t"],
      lw["ln2_g"], lw["ln2_b"])


def _prep_layer(l, w_in, b_gate, mh_g, sgu_g, sgu_b, w_s, b_s, w_pool, pool_scale, w_o, ln1_g, ln1_b,
                w_pq, peer_keys, ln2_g, ln2_b, sample_rows):
    n_gate = 2 * MLSTM_HEADS
    g0 = 4 * MLSTM_WIDTH
    w = w_in[l]
    w_gate = jnp.pad(w[:, g0:g0 + n_gate], ((0, 0), (0, LANES - n_gate)))
    w_gate_hi = w_gate.astype(BF16)
    w_gate_lo = (w_gate - w_gate_hi.astype(F32)).astype(BF16)
    w_in_p = jnp.concatenate(
        [w[:, :g0].astype(BF16), w_gate_hi, w[:, g0 + n_gate:].astype(BF16), w_gate_lo], axis=1)
    bs_full = jnp.repeat(jnp.swapaxes(b_s[l], 0, 1), SGU_DH, axis=1)
    reps = SGU_CHUNK // sample_rows
    eye = jnp.eye(len(POOL_WINDOWS), dtype=F32)
    dg = POOL_WIDTH // len(POOL_WINDOWS)
    w_pool_bd = (eye[:, None, :, None] * w_pool[l][:, :, None, :]).reshape(POOL_WIDTH, POOL_WIDTH)
    row = lambda a: a.reshape(1, -1)
    return {
        "w_in": w_in_p,
        "w_gate": w_gate_hi,
        "b_gate": jnp.pad(b_gate[l], (0, LANES - n_gate)).reshape(1, LANES),
        "mh_g": row(mh_g[l]), "sgu_g": row(sgu_g[l]), "sgu_b": row(sgu_b[l]),
        "w_s": w_s[l],
        "b_s": bs_full,
        "w_s_sample": jnp.tile(w_s[l][:, :sample_rows, :sample_rows], (1, reps, reps)),
        "b_s_sample": jnp.tile(bs_full[:sample_rows], (reps, 1)),
        "w_pool": w_pool_bd.astype(BF16),
        "pool_scale": row(pool_scale[l]),
        "w_o": w_o[l].astype(BF16),
        "ln1_g": row(ln1_g[l]), "ln1_b": row(ln1_b[l]),
        "w_pq_t": w_pq[l].T.astype(BF16),
        "peer_keys": peer_keys[l].reshape(2 * PEER_HEADS, PEER_NKEYS, PEER_DK).astype(BF16),
        "ln2_g": row(ln2_g[l]), "ln2_b": row(ln2_b[l]),
    }


def kernel(x_prompt, x_sample, state_mlstm_C, state_mlstm_n, state_mlstm_m, state_pool, c_prompt, c_sample,
           w_ada, b_ada, w_in, b_gate, mh_g, sgu_g, sgu_b, w_s, b_s, w_pool, pool_scale, w_o, ln1_g, ln1_b,
           w_pq, peer_keys, peer_u, peer_v, ln2_g, ln2_b):
    bp, seq, _ = x_prompt.shape
    bs, dec_seq, _ = x_sample.shape
    sample_rows = SUBLANES
    prompt_rows = min(seq, 256)
    peer_tn = 512
    peer_te = 1024
    sel_tn = 256

    ada = _ada(jnp.concatenate([c_prompt, c_sample], axis=0), w_ada, b_ada)

    xp = x_prompt
    xs = jnp.pad(x_sample, ((0, 0), (0, sample_rows - dec_seq), (0, 0)))
    zeros_c = jnp.zeros((1, bp, MLSTM_HEADS, MLSTM_DH, MLSTM_DH), F32)
    zeros_n = jnp.zeros((bp, MLSTM_HEADS, MLSTM_DH), F32)
    zeros_m = jnp.zeros((bp, MLSTM_HEADS, LANES), F32)
    zeros_hist = jnp.zeros((bp, POOL_HIST, POOL_WIDTH), F32)
    peer_u_b = peer_u.astype(BF16)
    peer_v_t = jnp.swapaxes(peer_v, 1, 2).astype(BF16)
    outs = [[] for _ in range(9)]
    for l in range(DEPTH):
        lw = _prep_layer(l, w_in, b_gate, mh_g, sgu_g, sgu_b, w_s, b_s, w_pool, pool_scale, w_o, ln1_g, ln1_b,
                         w_pq, peer_keys, ln2_g, ln2_b, sample_rows)
        lw.update(peer_u=peer_u_b, peer_v_t=peer_v_t, layer=l)
        mods = jnp.pad(ada[l].reshape(bp + bs, N_ADA, D_MODEL), ((0, 0), (0, SUBLANES - N_ADA), (0, 0)))
        mod_p, mod_s = mods[:bp], mods[bp:]

        x1p, cp, np_, mp, tailp = _mixer(
            xp, mod_p, zeros_c, 0, zeros_n, zeros_m, zeros_hist, lw,
            nb=1, rows=prompt_rows, chunk=SGU_CHUNK, n_valid=SGU_CHUNK, sgu_blk=SGU_CHUNK, pos0=0,
            want_vn=False)
        m_in = jnp.broadcast_to(state_mlstm_m[l][:, :, None], (bs, MLSTM_HEADS, LANES))
        hist_s = jnp.pad(state_pool[l], ((0, 0), (1, 0), (0, 0)))
        x1s, cs, ns, ms, tails, vns = _mixer(
            xs, mod_s, state_mlstm_C, l, state_mlstm_n[l], m_in, hist_s, lw,
            nb=16, rows=sample_rows, chunk=sample_rows, n_valid=dec_seq, sgu_blk=sample_rows, pos0=PAST_LEN,
            want_vn=True)

        x1p2 = x1p.reshape(bp * seq, D_MODEL)
        shp, scp, gp = mod_p[:, 3:4], mod_p[:, 4:5], mod_p[:, 5:6]
        sel = _peer_select(x1p2, shp, scp, lw, tn=sel_tn, tiles_per_row=seq // sel_tn)
        xp = _peer_dense(x1p2, shp, scp, gp, sel, lw, tn=peer_tn, te=peer_te,
                         tiles_per_row=seq // peer_tn).reshape(bp, seq, D_MODEL)
        x1s2 = x1s[:, :dec_seq].reshape(bs * dec_seq, D_MODEL)
        shs, scs, gs = (jnp.repeat(mod_s[:, i], dec_seq, axis=0) for i in (3, 4, 5))
        n_s = bs * dec_seq
        sel = _peer_select(x1s2, shs, scs, lw, tn=min(sel_tn, n_s), tiles_per_row=1)
        xs_new = _peer_dense(x1s2, shs, scs, gs, sel, lw, tn=min(peer_tn, n_s), te=peer_te, tiles_per_row=1)
        xs_new = xs_new.reshape(bs, dec_seq, D_MODEL)
        xs = jnp.pad(xs_new, ((0, 0), (0, sample_rows - dec_seq), (0, 0)))

        for lst, val in zip(outs, (cp, np_, mp[:, :, 0], tailp[:, 1:], cs, ns, ms[:, :, 0], tails[:, 1:],
                                   vns[:, :dec_seq])):
            lst.append(val)
    return (xp, xs_new) + tuple(jnp.stack(o) for o in outs)
```

```python
import functools

import jax
import jax.numpy as jnp
import numpy as np
from jax import lax
from jax.experimental import pallas as pl
from jax.experimental.pallas import tpu as pltpu

F32 = jnp.float32
BF16 = jnp.bfloat16
HIGHEST = lax.Precision.HIGHEST

D_MODEL = 1024
DEPTH = 2
N_ADA = 6
MLSTM_HEADS = 4
MLSTM_DH = 128
MLSTM_WIDTH = MLSTM_HEADS * MLSTM_DH
SGU_WIDTH = 256
SGU_HEADS = 4
SGU_DH = SGU_WIDTH // SGU_HEADS
SGU_CHUNK = 128
POOL_WIDTH = 256
POOL_WINDOWS = (2, 4, 8, 16)
POOL_HIST = 16
PEER_HEADS = 8
PEER_NKEYS = 128
PEER_TOPK = 16
PEER_DK = 128
PEER_EXPERTS = PEER_NKEYS * PEER_NKEYS
PAST_LEN = 16384
ALPHA = (2 * DEPTH) ** 0.25
LN_EPS = 1e-5

LANES = 128
SUBLANES = 8
VMEM_LIMIT_BYTES = 56 * 1024 * 1024

COL_Q = 0
COL_K = MLSTM_WIDTH
COL_V = 2 * MLSTM_WIDTH
COL_O = 3 * MLSTM_WIDTH
COL_GATE = 4 * MLSTM_WIDTH
COL_U = COL_GATE + LANES
COL_VS = COL_U + SGU_WIDTH
COL_P = COL_VS + SGU_WIDTH
COL_GATE_LO = COL_P + POOL_WIDTH
IN_COLS_PADDED = COL_GATE_LO + LANES

NT_DIMS = (((1,), (1,)), ((), ()))
TN_DIMS = (((0,), (0,)), ((), ()))


def _layer_norm(x):
    mu = jnp.mean(x, axis=-1, keepdims=True)
    d = x - mu
    var = jnp.mean(d * d, axis=-1, keepdims=True)
    return d * lax.rsqrt(var + LN_EPS)


def _ada_kernel(c_ref, w_ref, b_ref, o_ref):
    c = c_ref[...]
    s = c * jax.nn.sigmoid(c)
    o_ref[0] = jnp.dot(s, w_ref[0], precision=HIGHEST, preferred_element_type=F32) + b_ref[0]


def _ada(c_all, w_ada, b_ada):
    rows = c_all.shape[0]
    cols = w_ada.shape[-1]
    tile = 1536
    return pl.pallas_call(
        _ada_kernel,
        grid=(DEPTH, cols // tile),
        in_specs=[
            pl.BlockSpec((rows, D_MODEL), lambda l, j: (0, 0)),
            pl.BlockSpec((1, D_MODEL, tile), lambda l, j: (l, 0, j)),
            pl.BlockSpec((1, 1, tile), lambda l, j: (l, 0, j)),
        ],
        out_specs=pl.BlockSpec((1, rows, tile), lambda l, j: (l, 0, j)),
        out_shape=jax.ShapeDtypeStruct((DEPTH, rows, cols), F32),
        compiler_params=pltpu.CompilerParams(vmem_limit_bytes=VMEM_LIMIT_BYTES),
        name="ada",
    )(c_all, w_ada, b_ada.reshape(DEPTH, 1, cols))


def _mlstm_chunk(q, k, v, ig_col, ig_row, b_col, b_row, causal, c_state, n_state, m_state):
    length = q.shape[0]
    dlog = jnp.where(causal, b_col - b_row + ig_row, -jnp.inf)
    inter = b_col + m_state
    m_t = jnp.maximum(inter, jnp.max(dlog, axis=1, keepdims=True))
    w_intra = jnp.exp(dlog - m_t)
    w_inter = jnp.exp(inter - m_t)
    qb = q.astype(BF16)
    kb = k.astype(BF16)
    scores = lax.dot_general(qb, kb, NT_DIMS, preferred_element_type=F32)
    a = w_intra * scores
    num = jnp.dot(a.astype(BF16), v.astype(BF16), preferred_element_type=F32)
    num = num + w_inter * lax.dot_general(qb, c_state.astype(BF16), NT_DIMS, preferred_element_type=F32)
    den = jnp.sum(a, axis=1, keepdims=True) + w_inter * jnp.sum(q * n_state, axis=1, keepdims=True)
    h = num / jnp.maximum(jnp.abs(den), jnp.exp(-m_t))
    b_end = b_col[length - 1:length, :]
    dend = b_end - b_col + ig_col
    m_new = jnp.maximum(b_end + m_state, jnp.max(dend, axis=0, keepdims=True))
    wc = jnp.exp(dend - m_new)
    dec = jnp.exp(b_end + m_state - m_new)
    vw = (v * wc).astype(BF16)
    c_new = dec * c_state + lax.dot_general(vw, kb, TN_DIMS, preferred_element_type=F32)
    n_new = dec * n_state + jnp.sum(k * wc, axis=0, keepdims=True)
    return h, c_new, n_new, m_new


def _mixer_kernel(x_ref, mod_ref, c_in_ref, n_in_ref, m_in_ref, hist_ref,
                  w_in_ref, wgate_ref, bgate_ref, mhg_ref, sgug_ref, sgub_ref, ws_ref, bs_ref,
                  wpool_ref, pscale_ref, wo_ref, ln1g_ref, ln1b_ref,
                  x1_ref, c_out_ref, n_out_ref, m_out_ref, tail_ref, vn_ref,
                  proj_sc, ycat_sc, c_sc, n_sc, m_sc, carry_sc, pbuf_sc,
                  *, nb, rows, chunk, n_valid, sgu_blk, pos0):
    ci = pl.program_id(1)
    n_chunks = pl.num_programs(1)
    m_rows = nb * rows

    @pl.when(ci == 0)
    def _():
        c_sc[...] = c_in_ref[...]
        n_sc[...] = n_in_ref[...]
        m_sc[...] = m_in_ref[...]
        carry_sc[...] = hist_ref[...]

    x3 = x_ref[...]
    mod = mod_ref[...]
    sh1, sc1, g1 = mod[:, 0:1, :], mod[:, 1:2, :], mod[:, 2:3, :]
    h3 = x3 * (1.0 + sc1) + sh1
    h2d = h3.reshape(m_rows, D_MODEL)
    h_hi = h2d.astype(BF16)
    proj_sc[...] = jnp.dot(h_hi, w_in_ref[...], preferred_element_type=F32)
    h_lo = (h2d - h_hi.astype(F32)).astype(BF16)
    proj_sc[:, COL_GATE:COL_GATE + LANES] = (
        proj_sc[:, COL_GATE:COL_GATE + LANES] + proj_sc[:, COL_GATE_LO:COL_GATE_LO + LANES]
        + jnp.dot(h_lo, wgate_ref[...], preferred_element_type=F32))

    r_io = lax.broadcasted_iota(jnp.int32, (chunk, chunk), 0)
    c_io = lax.broadcasted_iota(jnp.int32, (chunk, chunk), 1)
    causal = c_io <= r_io
    tri = jnp.where(causal, 1.0, 0.0).astype(F32)
    row_id = lax.broadcasted_iota(jnp.int32, (chunk, 1), 0)
    valid_col = row_id < n_valid
    bgate = bgate_ref[...]
    mhg = mhg_ref[...]

    def seq_body(s, carry):
        for j in range(rows // chunk):
            row0 = s * rows + j * chunk
            if not isinstance(row0, int):
                row0 = pl.multiple_of(row0, SUBLANES)
            rsl = pl.ds(row0, chunk)
            gates = proj_sc[rsl, COL_GATE:COL_GATE + LANES] + bgate
            lf = jnp.where(valid_col, jax.nn.log_sigmoid(gates), 0.0)
            bcum = jnp.dot(tri, lf, precision=HIGHEST, preferred_element_type=F32)
            ig_all = jnp.where(valid_col, gates, -jnp.inf)
            bcum_t = bcum.T
            ig_t = ig_all.T
            for h in range(MLSTM_HEADS):
                hs = slice(h * MLSTM_DH, (h + 1) * MLSTM_DH)
                q = proj_sc[rsl, COL_Q + h * MLSTM_DH:COL_Q + (h + 1) * MLSTM_DH] * (MLSTM_DH ** -0.5)
                k = proj_sc[rsl, COL_K + h * MLSTM_DH:COL_K + (h + 1) * MLSTM_DH]
                v = proj_sc[rsl, COL_V + h * MLSTM_DH:COL_V + (h + 1) * MLSTM_DH]
                o = proj_sc[rsl, COL_O + h * MLSTM_DH:COL_O + (h + 1) * MLSTM_DH]
                fcol = MLSTM_HEADS + h
                hh, c_new, n_new, m_new = _mlstm_chunk(
                    q, k, v,
                    ig_all[:, h:h + 1], ig_t[h:h + 1, 0:chunk],
                    bcum[:, fcol:fcol + 1], bcum_t[fcol:fcol + 1, 0:chunk],
                    causal, c_sc[s, h], n_sc[s, pl.ds(h, 1), :], m_sc[s, pl.ds(h, 1), 0:1])
                c_sc[s, h] = c_new
                n_sc[s, pl.ds(h, 1), :] = n_new
                m_sc[s, pl.ds(h, 1), :] = jnp.broadcast_to(m_new, (1, LANES))
                hn = _layer_norm(hh) * mhg[:, hs]
                ycat_sc[rsl, hs] = jax.nn.sigmoid(o) * hn
        return carry

    if nb == 1:
        seq_body(0, 0)
    else:
        lax.fori_loop(0, nb, seq_body, 0)

    u_s = proj_sc[:, COL_U:COL_U + SGU_WIDTH]
    v_s = proj_sc[:, COL_VS:COL_VS + SGU_WIDTH]
    gr = lax.broadcasted_iota(jnp.int32, (SGU_WIDTH, SGU_WIDTH), 0) // SGU_DH
    gc = lax.broadcasted_iota(jnp.int32, (SGU_WIDTH, SGU_WIDTH), 1) // SGU_DH
    avg = jnp.where(gr == gc, 1.0 / SGU_DH, 0.0).astype(F32)
    mu = jnp.dot(v_s, avg, precision=HIGHEST, preferred_element_type=F32)
    dv = v_s - mu
    var = jnp.dot(dv * dv, avg, precision=HIGHEST, preferred_element_type=F32)
    vn = dv * lax.rsqrt(var + LN_EPS) * sgug_ref[...] + sgub_ref[...]
    if vn_ref is not None:
        vn_ref[...] = vn.reshape(nb, rows, SGU_WIDTH)
    sr = lax.broadcasted_iota(jnp.int32, (SGU_CHUNK, SGU_CHUNK), 0)
    scol = lax.broadcasted_iota(jnp.int32, (SGU_CHUNK, SGU_CHUNK), 1)
    smask = (scol <= sr) & ((sr // sgu_blk) == (scol // sgu_blk))
    lane_grp = lax.broadcasted_iota(jnp.int32, (SGU_CHUNK, SGU_WIDTH), 1) // SGU_DH
    vnb = vn.astype(BF16)
    for r in range(m_rows // SGU_CHUNK):
        rs = slice(r * SGU_CHUNK, (r + 1) * SGU_CHUNK)
        mix = jnp.zeros((SGU_CHUNK, SGU_WIDTH), F32)
        for g in range(SGU_HEADS):
            wg = jnp.where(smask, ws_ref[g], 0.0).astype(BF16)
            mg = jnp.dot(wg, vnb[rs], preferred_element_type=F32)
            mix = jnp.where(lane_grp == g, mg, mix)
        ycat_sc[rs, MLSTM_WIDTH:MLSTM_WIDTH + SGU_WIDTH] = u_s[rs] * (mix + bs_ref[...])

    pbuf_sc[:, 0:POOL_HIST, :] = carry_sc[...]
    pbuf_sc[:, POOL_HIST:POOL_HIST + rows, :] = proj_sc[:, COL_P:COL_P + POOL_WIDTH].reshape(nb, rows, POOL_WIDTH)
    lane_w = lax.broadcasted_iota(jnp.int32, (1, 1, POOL_WIDTH), 2) // (POOL_WIDTH // len(POOL_WINDOWS))
    x0 = pbuf_sc[:, POOL_HIST:POOL_HIST + rows, :]
    acc = x0
    wsum = jnp.zeros_like(x0)
    for kk in range(1, max(POOL_WINDOWS)):
        acc = acc + pbuf_sc[:, POOL_HIST - kk:POOL_HIST - kk + rows, :]
        if (kk + 1) in POOL_WINDOWS:
            wsum = jnp.where(lane_w == POOL_WINDOWS.index(kk + 1), acc, wsum)
    t_io = lax.broadcasted_iota(jnp.int32, (1, rows, POOL_WIDTH), 1)
    win = jnp.left_shift(2, lane_w)
    cnt = jnp.minimum(pos0 + ci * rows + t_io + 1, win).astype(F32)
    pooled = wsum / cnt - x0
    y_c = jnp.dot(pooled.reshape(m_rows, POOL_WIDTH).astype(BF16), wpool_ref[...],
                  preferred_element_type=F32) * pscale_ref[...]
    ycat_sc[:, MLSTM_WIDTH + SGU_WIDTH:D_MODEL] = y_c
    carry_sc[...] = pbuf_sc[:, rows:rows + POOL_HIST, :]

    y = jnp.dot(ycat_sc[...].astype(BF16), wo_ref[...], preferred_element_type=F32)
    z = ALPHA * x3 + g1 * y.reshape(nb, rows, D_MODEL)
    x1_ref[...] = _layer_norm(z) * ln1g_ref[...] + ln1b_ref[...]

    @pl.when(ci == n_chunks - 1)
    def _():
        c_out_ref[...] = c_sc[...]
        n_out_ref[...] = n_sc[...]
        m_out_ref[...] = m_sc[...]
        tail_ref[...] = pbuf_sc[:, n_valid + rows - chunk:n_valid + rows - chunk + POOL_HIST, :]


def _mixer(x, mod, c0, c0_layer, n0, m0, hist, lw, *, nb, rows, chunk, n_valid, sgu_blk, pos0, want_vn):
    bsz, seq, _ = x.shape
    grid = (bsz // nb, seq // rows)
    m_rows = nb * rows
    kern = functools.partial(_mixer_kernel, nb=nb, rows=rows, chunk=chunk, n_valid=n_valid,
                             sgu_blk=sgu_blk, pos0=pos0)
    if not want_vn:
        def kern(*refs, _k=kern):
            return _k(*refs[:24], None, *refs[24:])

    def full(shape):
        return pl.BlockSpec(shape, lambda b, c: (0,) * len(shape))

    per_b3 = lambda s1, s2: pl.BlockSpec((nb, s1, s2), lambda b, c: (b, 0, 0))
    in_specs = [
        pl.BlockSpec((nb, rows, D_MODEL), lambda b, c: (b, c, 0)),
        per_b3(SUBLANES, D_MODEL),
        pl.BlockSpec((None, nb, MLSTM_HEADS, MLSTM_DH, MLSTM_DH), lambda b, c: (c0_layer, b, 0, 0, 0)),
        per_b3(MLSTM_HEADS, MLSTM_DH),
        per_b3(MLSTM_HEADS, LANES),
        per_b3(POOL_HIST, POOL_WIDTH),
        full((D_MODEL, IN_COLS_PADDED)),
        full((D_MODEL, LANES)),
        full((1, LANES)),
        full((1, MLSTM_WIDTH)),
        full((1, SGU_WIDTH)),
        full((1, SGU_WIDTH)),
        full((SGU_HEADS, SGU_CHUNK, SGU_CHUNK)),
        full((SGU_CHUNK, SGU_WIDTH)),
        full((POOL_WIDTH, POOL_WIDTH)),
        full((1, POOL_WIDTH)),
        full((D_MODEL, D_MODEL)),
        full((1, D_MODEL)),
        full((1, D_MODEL)),
    ]
    out_specs = [
        pl.BlockSpec((nb, rows, D_MODEL), lambda b, c: (b, c, 0)),
        pl.BlockSpec((nb, MLSTM_HEADS, MLSTM_DH, MLSTM_DH), lambda b, c: (b, 0, 0, 0)),
        per_b3(MLSTM_HEADS, MLSTM_DH),
        per_b3(MLSTM_HEADS, LANES),
        per_b3(POOL_HIST, POOL_WIDTH),
    ]
    out_shape = [
        jax.ShapeDtypeStruct((bsz, seq, D_MODEL), F32),
        jax.ShapeDtypeStruct((bsz, MLSTM_HEADS, MLSTM_DH, MLSTM_DH), F32),
        jax.ShapeDtypeStruct((bsz, MLSTM_HEADS, MLSTM_DH), F32),
        jax.ShapeDtypeStruct((bsz, MLSTM_HEADS, LANES), F32),
        jax.ShapeDtypeStruct((bsz, POOL_HIST, POOL_WIDTH), F32),
    ]
    if want_vn:
        out_specs.append(pl.BlockSpec((nb, rows, SGU_WIDTH), lambda b, c: (b, c, 0)))
        out_shape.append(jax.ShapeDtypeStruct((bsz, seq, SGU_WIDTH), F32))
    scratch = [
        pltpu.VMEM((m_rows, IN_COLS_PADDED), F32),
        pltpu.VMEM((m_rows, D_MODEL), F32),
        pltpu.VMEM((nb, MLSTM_HEADS, MLSTM_DH, MLSTM_DH), F32),
        pltpu.VMEM((nb, MLSTM_HEADS, MLSTM_DH), F32),
        pltpu.VMEM((nb, MLSTM_HEADS, LANES), F32),
        pltpu.VMEM((nb, POOL_HIST, POOL_WIDTH), F32),
        pltpu.VMEM((nb, POOL_HIST + rows, POOL_WIDTH), F32),
    ]
    return pl.pallas_call(
        kern,
        grid=grid,
        in_specs=in_specs,
        out_specs=out_specs,
        out_shape=out_shape,
        scratch_shapes=scratch,
        compiler_params=pltpu.CompilerParams(
            dimension_semantics=("arbitrary", "arbitrary"), vmem_limit_bytes=VMEM_LIMIT_BYTES),
        name="mixer_sample" if want_vn else "mixer_prompt",
    )(x, mod, c0, n0, m0, hist, lw["w_in"], lw["w_gate"], lw["b_gate"], lw["mh_g"], lw["sgu_g"], lw["sgu_b"],
      lw["w_s_sample"] if want_vn else lw["w_s"], lw["b_s_sample"] if want_vn else lw["b_s"],
      lw["w_pool"], lw["pool_scale"], lw["w_o"], lw["ln1_g"], lw["ln1_b"])


def _sort16_pairs():
    pairs = []

    def merge(lo, hi, r):
        step = r * 2
        if step < hi - lo:
            merge(lo, hi, step)
            merge(lo + r, hi, step)
            for i in range(lo + r, hi - r, step):
                pairs.append((i, i + r))
        else:
            pairs.append((lo, lo + r))

    def sort(lo, hi):
        if hi - lo >= 1:
            mid = lo + (hi - lo) // 2
            sort(lo, mid)
            sort(mid + 1, hi)
            merge(lo, hi, 1)

    sort(0, PEER_TOPK - 1)
    return tuple(pairs)


SORT16_PAIRS = _sort16_pairs()


def _sort16_desc(vals):
    vals = list(vals)
    for i, j in SORT16_PAIRS:
        hi = jnp.maximum(vals[i], vals[j])
        lo = jnp.minimum(vals[i], vals[j])
        vals[i], vals[j] = hi, lo
    return vals


def _merge_top16(xs, ys):
    vals = [jnp.maximum(xs[i], ys[PEER_TOPK - 1 - i]) for i in range(PEER_TOPK)]
    d = PEER_TOPK // 2
    while d >= 1:
        for i in range(PEER_TOPK):
            if (i & d) == 0:
                hi = jnp.maximum(vals[i], vals[i + d])
                lo = jnp.minimum(vals[i], vals[i + d])
                vals[i], vals[i + d] = hi, lo
        d //= 2
    return vals


def _zero_of(words):
    bits = lax.bitcast_convert_type(words, jnp.uint32)
    bits = lax.shift_right_logical(lax.shift_right_logical(bits, jnp.uint32(16)), jnp.uint32(16))
    return lax.bitcast_convert_type(bits, F32)


def _top16_desc(tiles, zero_dep=None):
    tiles = list(tiles)
    neg = jnp.full_like(tiles[0], -jnp.inf)
    while len(tiles) % PEER_TOPK:
        tiles.append(neg)
    best = None
    for g in range(len(tiles) // PEER_TOPK):
        grp = tiles[g * PEER_TOPK:(g + 1) * PEER_TOPK]
        if zero_dep is not None:
            grp = [grp[0] + zero_dep] + grp[1:]
        grp = _sort16_desc(grp)
        best = grp if best is None else _merge_top16(best, grp)
        zero_dep = _zero_of(best[PEER_TOPK - 1])
    return best


S_PITCH = PEER_NKEYS + SUBLANES


def _peer_select_kernel(x_ref, sh_ref, sc_ref, wq_ref, keys_ref,
                        thr_ref, c1_ref, s2_ref, e2_ref,
                        q_sc, sa_sc, sb_sc, *, tn):
    s_sc = (sa_sc, sb_sc)
    x = x_ref[...]
    sh = sh_ref[...].reshape(-1, D_MODEL)
    sc = sc_ref[...].reshape(-1, D_MODEL)
    h2 = (x * (1.0 + sc) + sh).astype(BF16)
    q_sc[...] = lax.dot_general(wq_ref[...], h2, NT_DIMS, preferred_element_type=F32).astype(BF16)
    for h in range(PEER_HEADS):
        for p in range(2):
            r0 = (h * 2 + p) * PEER_DK
            s_hp = jnp.dot(keys_ref[h * 2 + p], q_sc[r0:r0 + PEER_DK, :], preferred_element_type=F32)
            for lg in range(tn // LANES):
                s_sc[p][lg, h * S_PITCH:h * S_PITCH + PEER_NKEYS, :] = s_hp[:, lg * LANES:(lg + 1) * LANES]
    dep = None
    for lg in range(tn // LANES):
        lanes = pl.ds(lg * LANES, LANES)
        tops = []
        for p in range(2):
            s_lg = s_sc[p].at[lg]
            tiles = [s_lg[pl.ds(i, PEER_HEADS, stride=S_PITCH), :] for i in range(PEER_NKEYS)]
            tops.append(_top16_desc(tiles, dep))
            dep = _zero_of(tops[-1][PEER_TOPK - 1])
        a, b = tops
        cands = []
        for r1 in range(PEER_TOPK):
            for r2 in range(PEER_TOPK // (r1 + 1)):
                cands.append(b[r2] + a[r1])
        c = _top16_desc(cands, dep)
        tau = c[PEER_TOPK - 1]
        z = jnp.zeros_like(tau)
        for kk in range(PEER_TOPK):
            z = z + jnp.exp(c[kk] - c[0])
        zinv = 1.0 / z
        dep = _zero_of(zinv)
        for h in range(PEER_HEADS):
            rows_in = slice(h * S_PITCH, h * S_PITCH + PEER_NKEYS)
            rows_out = slice(h * PEER_NKEYS, (h + 1) * PEER_NKEYS)
            s1 = sa_sc[lg, rows_in, :] + dep[0:1, :]
            s2 = sb_sc[lg, rows_in, :]
            tau_h = tau[h:h + 1, :]
            thr = jnp.full_like(s1, jnp.inf)
            for r in range(PEER_TOPK):
                cand = b[r][h:h + 1, :]
                thr = jnp.where((s1 + cand) >= tau_h, cand, thr)
            thr_ref[lg, rows_out, :] = thr
            dep = _zero_of(thr[0:SUBLANES, :])
            s2_ref[lg, rows_out, :] = s2
            c1_ref[lg, rows_out, :] = jnp.exp(s1 - a[0][h:h + 1, :]) * zinv[h:h + 1, :]
            e2_ref[lg, rows_out, :] = jnp.exp(s2 - b[0][h:h + 1, :])


def _mod_spec(arr, tn, tiles_per_row):
    if arr.ndim == 3:
        return pl.BlockSpec((1, 1, D_MODEL), lambda i, *_: (i // tiles_per_row, 0, 0))
    return pl.BlockSpec((tn, D_MODEL), lambda i, *_: (i, 0))


def _peer_select(x1, sh2, sc2, lw, *, tn, tiles_per_row):
    n_tok = x1.shape[0]
    assert n_tok % tn == 0
    sel_rows = PEER_HEADS * PEER_NKEYS
    n_lg = tn // LANES
    big = lambda: pl.BlockSpec((n_lg, sel_rows, LANES), lambda i: (i, 0, 0))
    return pl.pallas_call(
        functools.partial(_peer_select_kernel, tn=tn),
        grid=(n_tok // tn,),
        in_specs=[
            pl.BlockSpec((tn, D_MODEL), lambda i: (i, 0)),
            _mod_spec(sh2, tn, tiles_per_row),
            _mod_spec(sc2, tn, tiles_per_row),
            pl.BlockSpec((2 * PEER_HEADS * PEER_DK, D_MODEL), lambda i: (0, 0)),
            pl.BlockSpec((2 * PEER_HEADS, PEER_NKEYS, PEER_DK), lambda i: (0, 0, 0)),
        ],
        out_specs=[big(), big(), big(), big()],
        out_shape=[jax.ShapeDtypeStruct((n_tok // LANES, sel_rows, LANES), F32)] * 4,
        scratch_shapes=[
            pltpu.VMEM((2 * PEER_HEADS * PEER_DK, tn), BF16),
            pltpu.VMEM((tn // LANES, PEER_HEADS * S_PITCH, LANES), F32),
            pltpu.VMEM((tn // LANES, PEER_HEADS * S_PITCH, LANES), F32),
        ],
        compiler_params=pltpu.CompilerParams(
            dimension_semantics=("arbitrary",), vmem_limit_bytes=VMEM_LIMIT_BYTES),
        name="peer_select",
    )(x1, sh2, sc2, lw["w_pq_t"], lw["peer_keys"])


GATE_SUB = 32
PEER_STAGE = 2 * PEER_NKEYS
TILES_PER_STEP = 2


def _peer_dense_kernel(x_ref, sh_ref, sc_ref, g_ref, thr_ref, c1_ref, s2_ref, e2_ref,
                       ua_ref, ub_ref, vta_ref, vtb_prev_ref, lng_ref, lnb_ref, o_ref,
                       xb_sc, acc_sc, at_a, at_b, coef_a, coef_b, thrb_sc, c1b_sc, *, tn, te):
    j = pl.program_id(1)
    n_j = pl.num_programs(1) - 1
    stage = at_a.shape[0]
    n_st = te // stage
    keys_per_tile = te // PEER_NKEYS
    keys_per_step = TILES_PER_STEP * keys_per_tile
    out_chunk = D_MODEL // n_st
    assert n_st % 2 == 0

    @pl.when(j == 0)
    def _():
        sh = sh_ref[...].reshape(-1, D_MODEL)
        sc = sc_ref[...].reshape(-1, D_MODEL)
        xb_sc[...] = (x_ref[...] * (1.0 + sc) + sh).T.astype(BF16)
        acc_sc[...] = jnp.zeros_like(acc_sc)
        coef_b[...] = jnp.zeros_like(coef_b)

    def fill_key_tables():
        for lg in range(tn // LANES):
            for h in range(PEER_HEADS):
                krows = pl.ds(pl.multiple_of(h * PEER_NKEYS + j * keys_per_step, keys_per_step), keys_per_step)
                thr_t = thr_ref[lg, krows, :]
                c1_t = c1_ref[lg, krows, :]
                for k in range(keys_per_step):
                    thrb_sc[k, lg * PEER_HEADS + h] = jnp.broadcast_to(thr_t[k:k + 1, :], (SUBLANES, LANES))
                    c1b_sc[k, lg * PEER_HEADS + h] = jnp.broadcast_to(c1_t[k:k + 1, :], (SUBLANES, LANES))

    def scores(u_ref, st):
        rows = pl.ds(st * stage, stage)
        return jnp.dot(u_ref[rows, :], xb_sc[...], preferred_element_type=F32)

    def values_chunk(vt_ref, coef_ref, st):
        rows = pl.ds(st * out_chunk, out_chunk)
        acc_sc[rows, :] += jnp.dot(vt_ref[rows, :], coef_ref[...], preferred_element_type=F32)

    def gate_stage(key_base, st, at_ref, coef_ref):
        n_keys = stage // PEER_NKEYS
        gshape = (GATE_SUB // SUBLANES, SUBLANES, LANES)
        zero_dep = jnp.zeros((SUBLANES, LANES), F32)
        for lg in range(tn // LANES):
            lanes = slice(lg * LANES, (lg + 1) * LANES)
            for sub in range(PEER_NKEYS // GATE_SUB):
                gates = [jnp.broadcast_to(zero_dep[None], gshape) for _ in range(n_keys)]
                for h in range(PEER_HEADS):
                    hrows = slice(h * PEER_NKEYS + sub * GATE_SUB, h * PEER_NKEYS + (sub + 1) * GATE_SUB)
                    s2 = s2_ref[lg, hrows, :].reshape(gshape)
                    e2 = e2_ref[lg, hrows, :].reshape(gshape)
                    for qq in range(n_keys):
                        thrb = thrb_sc[key_base + st * n_keys + qq, lg * PEER_HEADS + h]
                        c1b = c1b_sc[key_base + st * n_keys + qq, lg * PEER_HEADS + h]
                        gates[qq] = gates[qq] + jnp.where(s2 >= thrb, e2 * c1b, 0.0)
                for qq in range(n_keys):
                    r0 = qq * PEER_NKEYS + sub * GATE_SUB
                    a = at_ref[r0:r0 + GATE_SUB, lanes]
                    act = 0.5 * a * (1.0 + lax.erf(a * (2.0 ** -0.5)))
                    out_rows = pl.ds(st * stage + r0, GATE_SUB)
                    coef = gates[qq].reshape(GATE_SUB, LANES) * act
                    coef_ref[out_rows, lanes] = coef.astype(BF16)
                bits = lax.bitcast_convert_type(coef[0:SUBLANES, :], jnp.uint32)
                bits = lax.shift_right_logical(lax.shift_right_logical(bits, jnp.uint32(16)), jnp.uint32(16))
                zero_dep = lax.bitcast_convert_type(bits, F32)

    def tile_phase(u_ref, u_next_ref, key_base, coef_cur, vt_prev_ref, coef_prev):
        for st in range(n_st):
            at_cur, at_nxt = (at_a, at_b) if st % 2 == 0 else (at_b, at_a)
            gate_stage(key_base, st, at_cur, coef_cur)
            if st + 1 < n_st:
                at_nxt[...] = scores(u_ref, st + 1)
            elif u_next_ref is not None:
                at_nxt[...] = scores(u_next_ref, 0)
            values_chunk(vt_prev_ref, coef_prev, st)

    @pl.when(j < n_j)
    def _():
        at_a[...] = scores(ua_ref, 0)
        fill_key_tables()
        tile_phase(ua_ref, ub_ref, 0, coef_a, vtb_prev_ref, coef_b)
        tile_phase(ub_ref, None, keys_per_tile, coef_b, vta_ref, coef_a)

    @pl.when(j == n_j)
    def _():
        for st in range(n_st):
            values_chunk(vtb_prev_ref, coef_b, st)
        g2 = g_ref[...].reshape(-1, D_MODEL)
        z = ALPHA * x_ref[...] + g2 * acc_sc[...].T
        o_ref[...] = _layer_norm(z) * lng_ref[...] + lnb_ref[...]


def _peer_dense(x1, sh2, sc2, g2, sel, lw, *, tn, te, tiles_per_row):
    n_tok = x1.shape[0]
    assert n_tok % tn == 0
    sel_rows = PEER_HEADS * PEER_NKEYS
    n_lg = tn // LANES
    thr, c1, s2, e2 = sel
    big = lambda: pl.BlockSpec((n_lg, sel_rows, LANES), lambda i, j: (i, 0, 0))
    n_j = PEER_EXPERTS // (TILES_PER_STEP * te)
    layer = lw["layer"]
    tile_a = lambda j: TILES_PER_STEP * jnp.minimum(j, n_j - 1)
    table = pltpu.VMEM((TILES_PER_STEP * te // PEER_NKEYS, n_lg * PEER_HEADS, SUBLANES, LANES), F32)
    return pl.pallas_call(
        functools.partial(_peer_dense_kernel, tn=tn, te=te),
        grid=(n_tok // tn, n_j + 1),
        in_specs=[
            pl.BlockSpec((tn, D_MODEL), lambda i, j: (i, 0)),
            _mod_spec(sh2, tn, tiles_per_row),
            _mod_spec(sc2, tn, tiles_per_row),
            _mod_spec(g2, tn, tiles_per_row),
            big(), big(), big(), big(),
            pl.BlockSpec((None, te, D_MODEL), lambda i, j: (layer, tile_a(j), 0)),
            pl.BlockSpec((None, te, D_MODEL), lambda i, j: (layer, tile_a(j) + 1, 0)),
            pl.BlockSpec((None, D_MODEL, te), lambda i, j: (layer, 0, tile_a(j))),
            pl.BlockSpec((None, D_MODEL, te), lambda i, j: (layer, 0, jnp.maximum(TILES_PER_STEP * j - 1, 0))),
            pl.BlockSpec((1, D_MODEL), lambda i, j: (0, 0)),
            pl.BlockSpec((1, D_MODEL), lambda i, j: (0, 0)),
        ],
        out_specs=pl.BlockSpec((tn, D_MODEL), lambda i, j: (i, 0)),
        out_shape=jax.ShapeDtypeStruct((n_tok, D_MODEL), F32),
        scratch_shapes=[
            pltpu.VMEM((D_MODEL, tn), BF16),
            pltpu.VMEM((D_MODEL, tn), F32),
            pltpu.VMEM((PEER_STAGE, tn), F32),
            pltpu.VMEM((PEER_STAGE, tn), F32),
            pltpu.VMEM((te, tn), BF16),
            pltpu.VMEM((te, tn), BF16),
            table, table,
        ],
        compiler_params=pltpu.CompilerParams(
            dimension_semantics=("arbitrary", "arbitrary"), vmem_limit_bytes=VMEM_LIMIT_BYTES),
        name="peer_dense",
    )(x1, sh2, sc2, g2, thr, c1, s2, e2, lw["peer_u"], lw["peer_u"], lw["peer_v_t"], lw["peer_v_t"],
      lw["ln2_g"], lw["ln2_b"])


def _prep_layer(l, w_in, b_gate, mh_g, sgu_g, sgu_b, w_s, b_s, w_pool, pool_scale, w_o, ln1_g, ln1_b,
                w_pq, peer_keys, ln2_g, ln2_b, sample_rows):
    n_gate = 2 * MLSTM_HEADS
    g0 = 4 * MLSTM_WIDTH
    w = w_in[l]
    w_gate = jnp.pad(w[:, g0:g0 + n_gate], ((0, 0), (0, LANES - n_gate)))
    w_gate_hi = w_gate.astype(BF16)
    w_gate_lo = (w_gate - w_gate_hi.astype(F32)).astype(BF16)
    w_in_p = jnp.concatenate(
        [w[:, :g0].astype(BF16), w_gate_hi, w[:, g0 + n_gate:].astype(BF16), w_gate_lo], axis=1)
    bs_full = jnp.repeat(jnp.swapaxes(b_s[l], 0, 1), SGU_DH, axis=1)
    reps = SGU_CHUNK // sample_rows
    eye = jnp.eye(len(POOL_WINDOWS), dtype=F32)
    dg = POOL_WIDTH // len(POOL_WINDOWS)
    w_pool_bd = (eye[:, None, :, None] * w_pool[l][:, :, None, :]).reshape(POOL_WIDTH, POOL_WIDTH)
    row = lambda a: a.reshape(1, -1)
    return {
        "w_in": w_in_p,
        "w_gate": w_gate_hi,
        "b_gate": jnp.pad(b_gate[l], (0, LANES - n_gate)).reshape(1, LANES),
        "mh_g": row(mh_g[l]), "sgu_g": row(sgu_g[l]), "sgu_b": row(sgu_b[l]),
        "w_s": w_s[l],
        "b_s": bs_full,
        "w_s_sample": jnp.tile(w_s[l][:, :sample_rows, :sample_rows], (1, reps, reps)),
        "b_s_sample": jnp.tile(bs_full[:sample_rows], (reps, 1)),
        "w_pool": w_pool_bd.astype(BF16),
        "pool_scale": row(pool_scale[l]),
        "w_o": w_o[l].astype(BF16),
        "ln1_g": row(ln1_g[l]), "ln1_b": row(ln1_b[l]),
        "w_pq_t": w_pq[l].T.astype(BF16),
        "peer_keys": peer_keys[l].reshape(2 * PEER_HEADS, PEER_NKEYS, PEER_DK).astype(BF16),
        "ln2_g": row(ln2_g[l]), "ln2_b": row(ln2_b[l]),
    }


def kernel(x_prompt, x_sample, state_mlstm_C, state_mlstm_n, state_mlstm_m, state_pool, c_prompt, c_sample,
           w_ada, b_ada, w_in, b_gate, mh_g, sgu_g, sgu_b, w_s, b_s, w_pool, pool_scale, w_o, ln1_g, ln1_b,
           w_pq, peer_keys, peer_u, peer_v, ln2_g, ln2_b):
    bp, seq, _ = x_prompt.shape
    bs, dec_seq, _ = x_sample.shape
    sample_rows = SUBLANES
    prompt_rows = min(seq, 256)
    peer_tn = 512
    peer_te = 1024
    sel_tn = 256

    ada = _ada(jnp.concatenate([c_prompt, c_sample], axis=0), w_ada, b_ada)

    xp = x_prompt
    xs = jnp.pad(x_sample, ((0, 0), (0, sample_rows - dec_seq), (0, 0)))
    zeros_c = jnp.zeros((1, bp, MLSTM_HEADS, MLSTM_DH, MLSTM_DH), F32)
    zeros_n = jnp.zeros((bp, MLSTM_HEADS, MLSTM_DH), F32)
    zeros_m = jnp.zeros((bp, MLSTM_HEADS, LANES), F32)
    zeros_hist = jnp.zeros((bp, POOL_HIST, POOL_WIDTH), F32)
    peer_u_b = peer_u.astype(BF16)
    peer_v_t = jnp.swapaxes(peer_v, 1, 2).astype(BF16)
    outs = [[] for _ in range(9)]
    for l in range(DEPTH):
        lw = _prep_layer(l, w_in, b_gate, mh_g, sgu_g, sgu_b, w_s, b_s, w_pool, pool_scale, w_o, ln1_g, ln1_b,
                         w_pq, peer_keys, ln2_g, ln2_b, sample_rows)
        lw.update(peer_u=peer_u_b, peer_v_t=peer_v_t, layer=l)
        mods = jnp.pad(ada[l].reshape(bp + bs, N_ADA, D_MODEL), ((0, 0), (0, SUBLANES - N_ADA), (0, 0)))
        mod_p, mod_s = mods[:bp], mods[bp:]

        x1p, cp, np_, mp, tailp = _mixer(
            xp, mod_p, zeros_c, 0, zeros_n, zeros_m, zeros_hist, lw,
            nb=1, rows=prompt_rows, chunk=SGU_CHUNK, n_valid=SGU_CHUNK, sgu_blk=SGU_CHUNK, pos0=0,
            want_vn=False)
        m_in = jnp.broadcast_to(state_mlstm_m[l][:, :, None], (bs, MLSTM_HEADS, LANES))
        hist_s = jnp.pad(state_pool[l], ((0, 0), (1, 0), (0, 0)))
        x1s, cs, ns, ms, tails, vns = _mixer(
            xs, mod_s, state_mlstm_C, l, state_mlstm_n[l], m_in, hist_s, lw,
            nb=16, rows=sample_rows, chunk=sample_rows, n_valid=dec_seq, sgu_blk=sample_rows, pos0=PAST_LEN,
            want_vn=True)

        x1p2 = x1p.reshape(bp * seq, D_MODEL)
        shp, scp, gp = mod_p[:, 3:4], mod_p[:, 4:5], mod_p[:, 5:6]
        sel = _peer_select(x1p2, shp, scp, lw, tn=sel_tn, tiles_per_row=seq // sel_tn)
        xp = _peer_dense(x1p2, shp, scp, gp, sel, lw, tn=peer_tn, te=peer_te,
                         tiles_per_row=seq // peer_tn).reshape(bp, seq, D_MODEL)
        x1s2 = x1s[:, :dec_seq].reshape(bs * dec_seq, D_MODEL)
        shs, scs, gs = (jnp.repeat(mod_s[:, i], dec_seq, axis=0) for i in (3, 4, 5))
        n_s = bs * dec_seq
        sel = _peer_select(x1s2, shs, scs, lw, tn=min(sel_tn, n_s), tiles_per_row=1)
        xs_new = _peer_dense(x1s2, shs, scs, gs, sel, lw, tn=min(peer_tn, n_s), te=peer_te, tiles_per_row=1)
        xs_new = xs_new.reshape(bs, dec_seq, D_MODEL)
        xs = jnp.pad(xs_new, ((0, 0), (0, sample_rows - dec_seq), (0, 0)))

        for lst, val in zip(outs, (cp, np_, mp[:, :, 0], tailp[:, 1:], cs, ns, ms[:, :, 0], tails[:, 1:],
                                   vns[:, :dec_seq])):
            lst.append(val)
    return (xp, xs_new) + tuple(jnp.stack(o) for o in outs)
```

```python
import functools

import jax
import jax.numpy as jnp
from jax import lax
from jax.experimental import pallas as pl
from jax.experimental.pallas import tpu as pltpu

F32 = jnp.float32
BF16 = jnp.bfloat16
HIGHEST = lax.Precision.HIGHEST

D_MODEL = 1024
DEPTH = 2
N_ADA = 6
MLSTM_HEADS = 4
MLSTM_DH = 128
MLSTM_WIDTH = MLSTM_HEADS * MLSTM_DH
SGU_WIDTH = 256
SGU_HEADS = 4
SGU_DH = SGU_WIDTH // SGU_HEADS
SGU_CHUNK = 128
POOL_WIDTH = 256
POOL_WINDOWS = (2, 4, 8, 16)
POOL_HIST = 16
PEER_HEADS = 8
PEER_NKEYS = 128
PEER_TOPK = 16
PEER_DK = 128
PEER_EXPERTS = PEER_NKEYS * PEER_NKEYS
PAST_LEN = 16384
ALPHA = (2 * DEPTH) ** 0.25
LN_EPS = 1e-5

LANES = 128
SUBLANES = 8
VMEM_LIMIT_BYTES = 56 * 1024 * 1024

MIXER_ROWS = 256
SELECT_TOKEN_TILE = 256
PEER_TOKEN_TILE = 512
PEER_EXPERT_TILE = D_MODEL

COL_Q = 0
COL_K = MLSTM_WIDTH
COL_V = 2 * MLSTM_WIDTH
COL_O = 3 * MLSTM_WIDTH
COL_GATE = 4 * MLSTM_WIDTH
COL_U = COL_GATE + LANES
COL_VS = COL_U + SGU_WIDTH
COL_P = COL_VS + SGU_WIDTH
COL_GATE_LO = COL_P + POOL_WIDTH
IN_COLS_PADDED = COL_GATE_LO + LANES

NT_DIMS = (((1,), (1,)), ((), ()))
TN_DIMS = (((0,), (0,)), ((), ()))


def _split_bf16(x):
    hi = x.astype(BF16)
    lo = (x - hi.astype(F32)).astype(BF16)
    return hi, lo


def _layer_norm(x):
    mu = jnp.mean(x, axis=-1, keepdims=True)
    d = x - mu
    var = jnp.mean(d * d, axis=-1, keepdims=True)
    return d * lax.rsqrt(var + LN_EPS)


def _ada_kernel(c_ref, w_ref, b_ref, o_ref):
    c = c_ref[...]
    s = c * jax.nn.sigmoid(c)
    o_ref[0] = jnp.dot(s, w_ref[0], precision=HIGHEST, preferred_element_type=F32) + b_ref[0]


def _ada(c_all, w_ada, b_ada):
    rows = c_all.shape[0]
    cols = w_ada.shape[-1]
    tile = 1536
    return pl.pallas_call(
        _ada_kernel,
        grid=(DEPTH, cols // tile),
        in_specs=[
            pl.BlockSpec((rows, D_MODEL), lambda l, j: (0, 0)),
            pl.BlockSpec((1, D_MODEL, tile), lambda l, j: (l, 0, j)),
            pl.BlockSpec((1, 1, tile), lambda l, j: (l, 0, j)),
        ],
        out_specs=pl.BlockSpec((1, rows, tile), lambda l, j: (l, 0, j)),
        out_shape=jax.ShapeDtypeStruct((DEPTH, rows, cols), F32),
        compiler_params=pltpu.CompilerParams(vmem_limit_bytes=VMEM_LIMIT_BYTES),
        name="ada",
    )(c_all, w_ada, b_ada.reshape(DEPTH, 1, cols))


def _mlstm_chunk(q, k, v, ig_col, ig_row, b_col, b_row, causal, c_state, n_state, m_state):
    length = q.shape[0]
    dlog = jnp.where(causal, b_col - b_row + ig_row, -jnp.inf)
    inter = b_col + m_state
    m_t = jnp.maximum(inter, jnp.max(dlog, axis=1, keepdims=True))
    w_intra = jnp.exp(dlog - m_t)
    w_inter = jnp.exp(inter - m_t)
    qb = q.astype(BF16)
    kb = k.astype(BF16)
    scores = lax.dot_general(qb, kb, NT_DIMS, preferred_element_type=F32)
    a = w_intra * scores
    num = jnp.dot(a.astype(BF16), v.astype(BF16), preferred_element_type=F32)
    num = num + w_inter * lax.dot_general(qb, c_state.astype(BF16), NT_DIMS, preferred_element_type=F32)
    den = jnp.sum(a, axis=1, keepdims=True) + w_inter * jnp.sum(q * n_state, axis=1, keepdims=True)
    h = num / jnp.maximum(jnp.abs(den), jnp.exp(-m_t))
    b_end = b_col[length - 1:length, :]
    dend = b_end - b_col + ig_col
    m_new = jnp.maximum(b_end + m_state, jnp.max(dend, axis=0, keepdims=True))
    wc = jnp.exp(dend - m_new)
    dec = jnp.exp(b_end + m_state - m_new)
    vw = (v * wc).astype(BF16)
    c_new = dec * c_state + lax.dot_general(vw, kb, TN_DIMS, preferred_element_type=F32)
    n_new = dec * n_state + jnp.sum(k * wc, axis=0, keepdims=True)
    return h, c_new, n_new, m_new


def _mixer_kernel(x_ref, mod_ref, c_in_ref, n_in_ref, m_in_ref, hist_ref,
                  w_in_ref, wgate_ref, bgate_ref, mhg_ref, sgug_ref, sgub_ref, ws_ref, bs_ref,
                  wpool_ref, pscale_ref, wo_ref, ln1g_ref, ln1b_ref,
                  x1_ref, c_out_ref, n_out_ref, m_out_ref, tail_ref, vn_ref,
                  proj_sc, ycat_sc, c_sc, n_sc, m_sc, carry_sc, pbuf_sc,
                  *, nb, rows, chunk, n_valid, sgu_blk, pos0):
    ci = pl.program_id(1)
    n_chunks = pl.num_programs(1)
    m_rows = nb * rows

    @pl.when(ci == 0)
    def _():
        c_sc[...] = c_in_ref[...]
        n_sc[...] = n_in_ref[...]
        m_sc[...] = m_in_ref[...]
        carry_sc[...] = hist_ref[...]

    x3 = x_ref[...]
    mod = mod_ref[...]
    sh1, sc1, g1 = mod[:, 0:1, :], mod[:, 1:2, :], mod[:, 2:3, :]
    h3 = x3 * (1.0 + sc1) + sh1
    h2d = h3.reshape(m_rows, D_MODEL)
    h_hi = h2d.astype(BF16)
    proj_sc[...] = jnp.dot(h_hi, w_in_ref[...], preferred_element_type=F32)
    h_lo = (h2d - h_hi.astype(F32)).astype(BF16)
    proj_sc[:, COL_GATE:COL_GATE + LANES] = (
        proj_sc[:, COL_GATE:COL_GATE + LANES] + proj_sc[:, COL_GATE_LO:COL_GATE_LO + LANES]
        + jnp.dot(h_lo, wgate_ref[...], preferred_element_type=F32))

    r_io = lax.broadcasted_iota(jnp.int32, (chunk, chunk), 0)
    c_io = lax.broadcasted_iota(jnp.int32, (chunk, chunk), 1)
    causal = c_io <= r_io
    tri = jnp.where(causal, 1.0, 0.0).astype(BF16)
    row_id = lax.broadcasted_iota(jnp.int32, (chunk, 1), 0)
    valid_col = row_id < n_valid
    bgate = bgate_ref[...]
    mhg = mhg_ref[...]

    def seq_body(s, carry):
        for j in range(rows // chunk):
            row0 = s * rows + j * chunk
            if not isinstance(row0, int):
                row0 = pl.multiple_of(row0, SUBLANES)
            rsl = pl.ds(row0, chunk)
            gates = proj_sc[rsl, COL_GATE:COL_GATE + LANES] + bgate
            lf = jnp.where(valid_col, jax.nn.log_sigmoid(gates), 0.0)
            lf_hi, lf_lo = _split_bf16(lf)
            bcum = (jnp.dot(tri, lf_hi, preferred_element_type=F32)
                    + jnp.dot(tri, lf_lo, preferred_element_type=F32))
            ig_all = jnp.where(valid_col, gates, -jnp.inf)
            bcum_t = bcum.T
            ig_t = ig_all.T
            for h in range(MLSTM_HEADS):
                hs = slice(h * MLSTM_DH, (h + 1) * MLSTM_DH)
                q = proj_sc[rsl, COL_Q + h * MLSTM_DH:COL_Q + (h + 1) * MLSTM_DH] * (MLSTM_DH ** -0.5)
                k = proj_sc[rsl, COL_K + h * MLSTM_DH:COL_K + (h + 1) * MLSTM_DH]
                v = proj_sc[rsl, COL_V + h * MLSTM_DH:COL_V + (h + 1) * MLSTM_DH]
                o = proj_sc[rsl, COL_O + h * MLSTM_DH:COL_O + (h + 1) * MLSTM_DH]
                fcol = MLSTM_HEADS + h
                hh, c_new, n_new, m_new = _mlstm_chunk(
                    q, k, v,
                    ig_all[:, h:h + 1], ig_t[h:h + 1, 0:chunk],
                    bcum[:, fcol:fcol + 1], bcum_t[fcol:fcol + 1, 0:chunk],
                    causal, c_sc[s, h], n_sc[s, pl.ds(h, 1), :], m_sc[s, pl.ds(h, 1), 0:1])
                c_sc[s, h] = c_new
                n_sc[s, pl.ds(h, 1), :] = n_new
                m_sc[s, pl.ds(h, 1), :] = jnp.broadcast_to(m_new, (1, LANES))
                hn = _layer_norm(hh) * mhg[:, hs]
                ycat_sc[rsl, hs] = jax.nn.sigmoid(o) * hn
        return carry

    if nb == 1:
        seq_body(0, 0)
    else:
        lax.fori_loop(0, nb, seq_body, 0)

    u_s = proj_sc[:, COL_U:COL_U + SGU_WIDTH]
    v_s = proj_sc[:, COL_VS:COL_VS + SGU_WIDTH]
    gr = lax.broadcasted_iota(jnp.int32, (SGU_WIDTH, SGU_WIDTH), 0) // SGU_DH
    gc = lax.broadcasted_iota(jnp.int32, (SGU_WIDTH, SGU_WIDTH), 1) // SGU_DH
    avg = jnp.where(gr == gc, 1.0 / SGU_DH, 0.0).astype(BF16)

    def group_mean(t):
        hi, lo = _split_bf16(t)
        return jnp.dot(hi, avg, preferred_element_type=F32) + jnp.dot(lo, avg, preferred_element_type=F32)

    mu = group_mean(v_s)
    dv = v_s - mu
    var = group_mean(dv * dv)
    vn = dv * lax.rsqrt(var + LN_EPS) * sgug_ref[...] + sgub_ref[...]
    if vn_ref is not None:
        vn_ref[...] = vn.reshape(nb, rows, SGU_WIDTH)
    sr = lax.broadcasted_iota(jnp.int32, (SGU_CHUNK, SGU_CHUNK), 0)
    scol = lax.broadcasted_iota(jnp.int32, (SGU_CHUNK, SGU_CHUNK), 1)
    smask = (scol <= sr) & ((sr // sgu_blk) == (scol // sgu_blk))
    lane_grp = lax.broadcasted_iota(jnp.int32, (SGU_CHUNK, SGU_WIDTH), 1) // SGU_DH
    vnb = vn.astype(BF16)
    for r in range(m_rows // SGU_CHUNK):
        rs = slice(r * SGU_CHUNK, (r + 1) * SGU_CHUNK)
        mix = jnp.zeros((SGU_CHUNK, SGU_WIDTH), F32)
        for g in range(SGU_HEADS):
            wg = jnp.where(smask, ws_ref[g], 0.0).astype(BF16)
            mg = jnp.dot(wg, vnb[rs], preferred_element_type=F32)
            mix = jnp.where(lane_grp == g, mg, mix)
        ycat_sc[rs, MLSTM_WIDTH:MLSTM_WIDTH + SGU_WIDTH] = u_s[rs] * (mix + bs_ref[...])

    pbuf_sc[:, 0:POOL_HIST, :] = carry_sc[...]
    pbuf_sc[:, POOL_HIST:POOL_HIST + rows, :] = proj_sc[:, COL_P:COL_P + POOL_WIDTH].reshape(nb, rows, POOL_WIDTH)
    lane_w = lax.broadcasted_iota(jnp.int32, (1, 1, POOL_WIDTH), 2) // (POOL_WIDTH // len(POOL_WINDOWS))
    x0 = pbuf_sc[:, POOL_HIST:POOL_HIST + rows, :]
    acc = x0
    wsum = jnp.zeros_like(x0)
    for kk in range(1, max(POOL_WINDOWS)):
        acc = acc + pbuf_sc[:, POOL_HIST - kk:POOL_HIST - kk + rows, :]
        if (kk + 1) in POOL_WINDOWS:
            wsum = jnp.where(lane_w == POOL_WINDOWS.index(kk + 1), acc, wsum)
    t_io = lax.broadcasted_iota(jnp.int32, (1, rows, POOL_WIDTH), 1)
    win = jnp.left_shift(2, lane_w)
    cnt = jnp.minimum(pos0 + ci * rows + t_io + 1, win).astype(F32)
    pooled = wsum / cnt - x0
    y_c = jnp.dot(pooled.reshape(m_rows, POOL_WIDTH).astype(BF16), wpool_ref[...],
                  preferred_element_type=F32) * pscale_ref[...]
    ycat_sc[:, MLSTM_WIDTH + SGU_WIDTH:D_MODEL] = y_c
    carry_sc[...] = pbuf_sc[:, rows:rows + POOL_HIST, :]

    y = jnp.dot(ycat_sc[...].astype(BF16), wo_ref[...], preferred_element_type=F32)
    z = ALPHA * x3 + g1 * y.reshape(nb, rows, D_MODEL)
    x1_ref[...] = _layer_norm(z) * ln1g_ref[...] + ln1b_ref[...]

    @pl.when(ci == n_chunks - 1)
    def _():
        c_out_ref[...] = c_sc[...]
        n_out_ref[...] = n_sc[...]
        m_out_ref[...] = m_sc[...]
        tail_ref[...] = pbuf_sc[:, n_valid + rows - chunk:n_valid + rows - chunk + POOL_HIST, :]


def _mixer(x, mod, c0, c0_layer, n0, m0, hist, lw, *, nb, rows, chunk, n_valid, sgu_blk, pos0, want_vn):
    bsz, seq, _ = x.shape
    grid = (bsz // nb, seq // rows)
    m_rows = nb * rows
    kern = functools.partial(_mixer_kernel, nb=nb, rows=rows, chunk=chunk, n_valid=n_valid,
                             sgu_blk=sgu_blk, pos0=pos0)
    if not want_vn:
        def kern(*refs, _k=kern):
            return _k(*refs[:24], None, *refs[24:])

    def full(shape):
        return pl.BlockSpec(shape, lambda b, c: (0,) * len(shape))

    per_b3 = lambda s1, s2: pl.BlockSpec((nb, s1, s2), lambda b, c: (b, 0, 0))
    in_specs = [
        pl.BlockSpec((nb, rows, D_MODEL), lambda b, c: (b, c, 0)),
        per_b3(SUBLANES, D_MODEL),
        pl.BlockSpec((None, nb, MLSTM_HEADS, MLSTM_DH, MLSTM_DH), lambda b, c: (c0_layer, b, 0, 0, 0)),
        per_b3(MLSTM_HEADS, MLSTM_DH),
        per_b3(MLSTM_HEADS, LANES),
        per_b3(POOL_HIST, POOL_WIDTH),
        full((D_MODEL, IN_COLS_PADDED)),
        full((D_MODEL, LANES)),
        full((1, LANES)),
        full((1, MLSTM_WIDTH)),
        full((1, SGU_WIDTH)),
        full((1, SGU_WIDTH)),
        full((SGU_HEADS, SGU_CHUNK, SGU_CHUNK)),
        full((SGU_CHUNK, SGU_WIDTH)),
        full((POOL_WIDTH, POOL_WIDTH)),
        full((1, POOL_WIDTH)),
        full((D_MODEL, D_MODEL)),
        full((1, D_MODEL)),
        full((1, D_MODEL)),
    ]
    out_specs = [
        pl.BlockSpec((nb, rows, D_MODEL), lambda b, c: (b, c, 0)),
        pl.BlockSpec((nb, MLSTM_HEADS, MLSTM_DH, MLSTM_DH), lambda b, c: (b, 0, 0, 0)),
        per_b3(MLSTM_HEADS, MLSTM_DH),
        per_b3(MLSTM_HEADS, LANES),
        per_b3(POOL_HIST, POOL_WIDTH),
    ]
    out_shape = [
        jax.ShapeDtypeStruct((bsz, seq, D_MODEL), F32),
        jax.ShapeDtypeStruct((bsz, MLSTM_HEADS, MLSTM_DH, MLSTM_DH), F32),
        jax.ShapeDtypeStruct((bsz, MLSTM_HEADS, MLSTM_DH), F32),
        jax.ShapeDtypeStruct((bsz, MLSTM_HEADS, LANES), F32),
        jax.ShapeDtypeStruct((bsz, POOL_HIST, POOL_WIDTH), F32),
    ]
    if want_vn:
        out_specs.append(pl.BlockSpec((nb, rows, SGU_WIDTH), lambda b, c: (b, c, 0)))
        out_shape.append(jax.ShapeDtypeStruct((bsz, seq, SGU_WIDTH), F32))
    scratch = [
        pltpu.VMEM((m_rows, IN_COLS_PADDED), F32),
        pltpu.VMEM((m_rows, D_MODEL), F32),
        pltpu.VMEM((nb, MLSTM_HEADS, MLSTM_DH, MLSTM_DH), F32),
        pltpu.VMEM((nb, MLSTM_HEADS, MLSTM_DH), F32),
        pltpu.VMEM((nb, MLSTM_HEADS, LANES), F32),
        pltpu.VMEM((nb, POOL_HIST, POOL_WIDTH), F32),
        pltpu.VMEM((nb, POOL_HIST + rows, POOL_WIDTH), F32),
    ]
    return pl.pallas_call(
        kern,
        grid=grid,
        in_specs=in_specs,
        out_specs=out_specs,
        out_shape=out_shape,
        scratch_shapes=scratch,
        compiler_params=pltpu.CompilerParams(
            dimension_semantics=("arbitrary", "arbitrary"), vmem_limit_bytes=VMEM_LIMIT_BYTES),
        name="mixer_sample" if want_vn else "mixer_prompt",
    )(x, mod, c0, n0, m0, hist, lw["w_in"], lw["w_gate"], lw["b_gate"], lw["mh_g"], lw["sgu_g"], lw["sgu_b"],
      lw["w_s_sample"] if want_vn else lw["w_s"], lw["b_s_sample"] if want_vn else lw["b_s"],
      lw["w_pool"], lw["pool_scale"], lw["w_o"], lw["ln1_g"], lw["ln1_b"])


def _sort16_pairs():
    pairs = []

    def merge(lo, hi, r):
        step = r * 2
        if step < hi - lo:
            merge(lo, hi, step)
            merge(lo + r, hi, step)
            for i in range(lo + r, hi - r, step):
                pairs.append((i, i + r))
        else:
            pairs.append((lo, lo + r))

    def sort(lo, hi):
        if hi - lo >= 1:
            mid = lo + (hi - lo) // 2
            sort(lo, mid)
            sort(mid + 1, hi)
            merge(lo, hi, 1)

    sort(0, PEER_TOPK - 1)
    return tuple(pairs)


SORT16_PAIRS = _sort16_pairs()


def _sort16_desc(vals):
    vals = list(vals)
    for i, j in SORT16_PAIRS:
        hi = jnp.maximum(vals[i], vals[j])
        lo = jnp.minimum(vals[i], vals[j])
        vals[i], vals[j] = hi, lo
    return vals


def _merge_top16(xs, ys):
    vals = [jnp.maximum(xs[i], ys[PEER_TOPK - 1 - i]) for i in range(PEER_TOPK)]
    d = PEER_TOPK // 2
    while d >= 1:
        for i in range(PEER_TOPK):
            if (i & d) == 0:
                hi = jnp.maximum(vals[i], vals[i + d])
                lo = jnp.minimum(vals[i], vals[i + d])
                vals[i], vals[i + d] = hi, lo
        d //= 2
    return vals


def _zero_of(words):
    bits = lax.bitcast_convert_type(words, jnp.uint32)
    bits = lax.shift_right_logical(lax.shift_right_logical(bits, jnp.uint32(16)), jnp.uint32(16))
    return lax.bitcast_convert_type(bits, F32)


def _top16_desc(tiles, zero_dep=None):
    tiles = list(tiles)
    neg = jnp.full_like(tiles[0], -jnp.inf)
    while len(tiles) % PEER_TOPK:
        tiles.append(neg)
    best = None
    for g in range(len(tiles) // PEER_TOPK):
        grp = tiles[g * PEER_TOPK:(g + 1) * PEER_TOPK]
        if zero_dep is not None:
            grp = [grp[0] + zero_dep] + grp[1:]
        grp = _sort16_desc(grp)
        best = grp if best is None else _merge_top16(best, grp)
        zero_dep = _zero_of(best[PEER_TOPK - 1])
    return best


S_PITCH = PEER_NKEYS + SUBLANES


def _peer_select_kernel(x_ref, sh_ref, sc_ref, wq_ref, keys_ref,
                        thr_ref, c1_ref, s2_ref, e2_ref,
                        q_sc, sa_sc, sb_sc, *, tn):
    s_sc = (sa_sc, sb_sc)
    x = x_ref[...]
    sh = sh_ref[...].reshape(-1, D_MODEL)
    sc = sc_ref[...].reshape(-1, D_MODEL)
    h2 = (x * (1.0 + sc) + sh).astype(BF16)
    q_sc[...] = lax.dot_general(wq_ref[...], h2, NT_DIMS, preferred_element_type=F32).astype(BF16)
    for h in range(PEER_HEADS):
        for p in range(2):
            r0 = (h * 2 + p) * PEER_DK
            s_hp = jnp.dot(keys_ref[h * 2 + p], q_sc[r0:r0 + PEER_DK, :], preferred_element_type=F32)
            for lg in range(tn // LANES):
                s_sc[p][lg, h * S_PITCH:h * S_PITCH + PEER_NKEYS, :] = s_hp[:, lg * LANES:(lg + 1) * LANES]
    dep = None
    for lg in range(tn // LANES):
        lanes = pl.ds(lg * LANES, LANES)
        tops = []
        for p in range(2):
            s_lg = s_sc[p].at[lg]
            tiles = [s_lg[pl.ds(i, PEER_HEADS, stride=S_PITCH), :] for i in range(PEER_NKEYS)]
            tops.append(_top16_desc(tiles, dep))
            dep = _zero_of(tops[-1][PEER_TOPK - 1])
        a, b = tops
        cands = []
        for r1 in range(PEER_TOPK):
            for r2 in range(PEER_TOPK // (r1 + 1)):
                cands.append(b[r2] + a[r1])
        c = _top16_desc(cands, dep)
        tau = c[PEER_TOPK - 1]
        z = jnp.zeros_like(tau)
        for kk in range(PEER_TOPK):
            z = z + jnp.exp(c[kk] - c[0])
        zinv = 0.5 / z
        dep = _zero_of(zinv)
        for h in range(PEER_HEADS):
            rows_in = slice(h * S_PITCH, h * S_PITCH + PEER_NKEYS)
            rows_out = slice(h * PEER_NKEYS, (h + 1) * PEER_NKEYS)
            s1 = sa_sc[lg, rows_in, :] + dep[0:1, :]
            s2 = sb_sc[lg, rows_in, :]
            tau_h = tau[h:h + 1, :]
            thr = jnp.full_like(s1, jnp.inf)
            for r in range(PEER_TOPK):
                cand = b[r][h:h + 1, :]
                thr = jnp.where((s1 + cand) >= tau_h, cand, thr)
            thr_ref[lg, rows_out, :] = thr
            dep = _zero_of(thr[0:SUBLANES, :])
            s2_ref[lg, rows_out, :] = s2
            c1_ref[lg, rows_out, :] = jnp.exp(s1 - a[0][h:h + 1, :]) * zinv[h:h + 1, :]
            e2_ref[lg, rows_out, :] = jnp.exp(s2 - b[0][h:h + 1, :])


def _mod_spec(arr, tn, tiles_per_row):
    if arr.ndim == 3:
        return pl.BlockSpec((1, 1, D_MODEL), lambda i, *_: (i // tiles_per_row, 0, 0))
    return pl.BlockSpec((tn, D_MODEL), lambda i, *_: (i, 0))


def _peer_select(x1, sh2, sc2, lw, *, tn, tiles_per_row):
    n_tok = x1.shape[0]
    assert n_tok % tn == 0
    sel_rows = PEER_HEADS * PEER_NKEYS
    n_lg = tn // LANES
    big = lambda: pl.BlockSpec((n_lg, sel_rows, LANES), lambda i: (i, 0, 0))
    return pl.pallas_call(
        functools.partial(_peer_select_kernel, tn=tn),
        grid=(n_tok // tn,),
        in_specs=[
            pl.BlockSpec((tn, D_MODEL), lambda i: (i, 0)),
            _mod_spec(sh2, tn, tiles_per_row),
            _mod_spec(sc2, tn, tiles_per_row),
            pl.BlockSpec((2 * PEER_HEADS * PEER_DK, D_MODEL), lambda i: (0, 0)),
            pl.BlockSpec((2 * PEER_HEADS, PEER_NKEYS, PEER_DK), lambda i: (0, 0, 0)),
        ],
        out_specs=[big(), big(), big(), big()],
        out_shape=[jax.ShapeDtypeStruct((n_tok // LANES, sel_rows, LANES), F32)] * 4,
        scratch_shapes=[
            pltpu.VMEM((2 * PEER_HEADS * PEER_DK, tn), BF16),
            pltpu.VMEM((tn // LANES, PEER_HEADS * S_PITCH, LANES), F32),
            pltpu.VMEM((tn // LANES, PEER_HEADS * S_PITCH, LANES), F32),
        ],
        compiler_params=pltpu.CompilerParams(
            dimension_semantics=("arbitrary",), vmem_limit_bytes=VMEM_LIMIT_BYTES),
        name="peer_select",
    )(x1, sh2, sc2, lw["w_pq_t"], lw["peer_keys"])


GATE_SUB = 32
PEER_STAGE = 2 * PEER_NKEYS
TILES_PER_STEP = 2


def _peer_dense_kernel(x_ref, sh_ref, sc_ref, g_ref, thr_ref, c1_ref, s2_ref, e2_ref,
                       ua_ref, ub_ref, vta_ref, vtb_prev_ref, lng_ref, lnb_ref, o_ref,
                       xb_sc, acc_sc, at_a, at_b, coef_a, coef_b, thrb_sc, c1b_sc, *, tn, te):
    j = pl.program_id(1)
    n_j = pl.num_programs(1) - 1
    stage = at_a.shape[0]
    n_st = te // stage
    keys_per_tile = te // PEER_NKEYS
    keys_per_step = TILES_PER_STEP * keys_per_tile
    out_chunk = D_MODEL // n_st
    assert n_st % 2 == 0

    @pl.when(j == 0)
    def _():
        sh = sh_ref[...].reshape(-1, D_MODEL)
        sc = sc_ref[...].reshape(-1, D_MODEL)
        xb_sc[...] = (x_ref[...] * (1.0 + sc) + sh).T.astype(BF16)
        acc_sc[...] = jnp.zeros_like(acc_sc)
        coef_b[...] = jnp.zeros_like(coef_b)

    def fill_key_tables():
        for lg in range(tn // LANES):
            for h in range(PEER_HEADS):
                krows = pl.ds(pl.multiple_of(h * PEER_NKEYS + j * keys_per_step, keys_per_step), keys_per_step)
                thr_t = thr_ref[lg, krows, :]
                c1_t = c1_ref[lg, krows, :]
                for k in range(keys_per_step):
                    thrb_sc[k, lg * PEER_HEADS + h] = jnp.broadcast_to(thr_t[k:k + 1, :], (SUBLANES, LANES))
                    c1b_sc[k, lg * PEER_HEADS + h] = jnp.broadcast_to(c1_t[k:k + 1, :], (SUBLANES, LANES))

    def scores(u_ref, st):
        rows = pl.ds(st * stage, stage)
        return jnp.dot(u_ref[rows, :], xb_sc[...], preferred_element_type=F32)

    def values_chunk(vt_ref, coef_ref, st):
        rows = pl.ds(st * out_chunk, out_chunk)
        acc_sc[rows, :] += jnp.dot(vt_ref[rows, :], coef_ref[...], preferred_element_type=F32)

    def gate_stage(key_base, st, at_ref, coef_ref):
        n_keys = stage // PEER_NKEYS
        gshape = (GATE_SUB // SUBLANES, SUBLANES, LANES)
        zero_dep = jnp.zeros((SUBLANES, LANES), F32)
        for lg in range(tn // LANES):
            lanes = slice(lg * LANES, (lg + 1) * LANES)
            for sub in range(PEER_NKEYS // GATE_SUB):
                gates = [jnp.broadcast_to(zero_dep[None], gshape) for _ in range(n_keys)]
                for h in range(PEER_HEADS):
                    hrows = slice(h * PEER_NKEYS + sub * GATE_SUB, h * PEER_NKEYS + (sub + 1) * GATE_SUB)
                    s2 = s2_ref[lg, hrows, :].reshape(gshape)
                    e2 = e2_ref[lg, hrows, :].reshape(gshape)
                    for qq in range(n_keys):
                        thrb = thrb_sc[key_base + st * n_keys + qq, lg * PEER_HEADS + h]
                        c1b = c1b_sc[key_base + st * n_keys + qq, lg * PEER_HEADS + h]
                        gates[qq] = gates[qq] + jnp.where(s2 >= thrb, e2 * c1b, 0.0)
                for qq in range(n_keys):
                    r0 = qq * PEER_NKEYS + sub * GATE_SUB
                    a = at_ref[r0:r0 + GATE_SUB, lanes]
                    act = a * (1.0 + lax.erf(a * (2.0 ** -0.5)))
                    out_rows = pl.ds(st * stage + r0, GATE_SUB)
                    coef = gates[qq].reshape(GATE_SUB, LANES) * act
                    coef_ref[out_rows, lanes] = coef.astype(BF16)
                bits = lax.bitcast_convert_type(coef[0:SUBLANES, :], jnp.uint32)
                bits = lax.shift_right_logical(lax.shift_right_logical(bits, jnp.uint32(16)), jnp.uint32(16))
                zero_dep = lax.bitcast_convert_type(bits, F32)

    def tile_phase(u_ref, u_next_ref, key_base, coef_cur, vt_prev_ref, coef_prev):
        for st in range(n_st):
            at_cur, at_nxt = (at_a, at_b) if st % 2 == 0 else (at_b, at_a)
            gate_stage(key_base, st, at_cur, coef_cur)
            if st + 1 < n_st:
                at_nxt[...] = scores(u_ref, st + 1)
            elif u_next_ref is not None:
                at_nxt[...] = scores(u_next_ref, 0)
            values_chunk(vt_prev_ref, coef_prev, st)

    @pl.when(j < n_j)
    def _():
        at_a[...] = scores(ua_ref, 0)
        fill_key_tables()
        tile_phase(ua_ref, ub_ref, 0, coef_a, vtb_prev_ref, coef_b)
        tile_phase(ub_ref, None, keys_per_tile, coef_b, vta_ref, coef_a)

    @pl.when(j == n_j)
    def _():
        for st in range(n_st):
            values_chunk(vtb_prev_ref, coef_b, st)
        g2 = g_ref[...].reshape(-1, D_MODEL)
        z = ALPHA * x_ref[...] + g2 * acc_sc[...].T
        o_ref[...] = _layer_norm(z) * lng_ref[...] + lnb_ref[...]


def _peer_dense(x1, sh2, sc2, g2, sel, lw, *, tn, te, tiles_per_row):
    n_tok = x1.shape[0]
    assert n_tok % tn == 0
    sel_rows = PEER_HEADS * PEER_NKEYS
    n_lg = tn // LANES
    thr, c1, s2, e2 = sel
    big = lambda: pl.BlockSpec((n_lg, sel_rows, LANES), lambda i, j: (i, 0, 0))
    n_j = PEER_EXPERTS // (TILES_PER_STEP * te)
    layer = lw["layer"]
    tile_a = lambda j: TILES_PER_STEP * jnp.minimum(j, n_j - 1)
    table = pltpu.VMEM((TILES_PER_STEP * te // PEER_NKEYS, n_lg * PEER_HEADS, SUBLANES, LANES), F32)
    return pl.pallas_call(
        functools.partial(_peer_dense_kernel, tn=tn, te=te),
        grid=(n_tok // tn, n_j + 1),
        in_specs=[
            pl.BlockSpec((tn, D_MODEL), lambda i, j: (i, 0)),
            _mod_spec(sh2, tn, tiles_per_row),
            _mod_spec(sc2, tn, tiles_per_row),
            _mod_spec(g2, tn, tiles_per_row),
            big(), big(), big(), big(),
            pl.BlockSpec((None, te, D_MODEL), lambda i, j: (layer, tile_a(j), 0)),
            pl.BlockSpec((None, te, D_MODEL), lambda i, j: (layer, tile_a(j) + 1, 0)),
            pl.BlockSpec((None, D_MODEL, te), lambda i, j: (layer, 0, tile_a(j))),
            pl.BlockSpec((None, D_MODEL, te), lambda i, j: (layer, 0, jnp.maximum(TILES_PER_STEP * j - 1, 0))),
            pl.BlockSpec((1, D_MODEL), lambda i, j: (0, 0)),
            pl.BlockSpec((1, D_MODEL), lambda i, j: (0, 0)),
        ],
        out_specs=pl.BlockSpec((tn, D_MODEL), lambda i, j: (i, 0)),
        out_shape=jax.ShapeDtypeStruct((n_tok, D_MODEL), F32),
        scratch_shapes=[
            pltpu.VMEM((D_MODEL, tn), BF16),
            pltpu.VMEM((D_MODEL, tn), F32),
            pltpu.VMEM((PEER_STAGE, tn), F32),
            pltpu.VMEM((PEER_STAGE, tn), F32),
            pltpu.VMEM((te, tn), BF16),
            pltpu.VMEM((te, tn), BF16),
            table, table,
        ],
        compiler_params=pltpu.CompilerParams(
            dimension_semantics=("arbitrary", "arbitrary"), vmem_limit_bytes=VMEM_LIMIT_BYTES),
        name="peer_dense",
    )(x1, sh2, sc2, g2, thr, c1, s2, e2, lw["peer_u"], lw["peer_u"], lw["peer_v_t"], lw["peer_v_t"],
      lw["ln2_g"], lw["ln2_b"])


def _prep_layer(l, w_in, b_gate, mh_g, sgu_g, sgu_b, w_s, b_s, w_pool, pool_scale, w_o, ln1_g, ln1_b,
                w_pq, peer_keys, ln2_g, ln2_b, sample_rows):
    n_gate = 2 * MLSTM_HEADS
    g0 = 4 * MLSTM_WIDTH
    w = w_in[l]
    w_gate = jnp.pad(w[:, g0:g0 + n_gate], ((0, 0), (0, LANES - n_gate)))
    w_gate_hi = w_gate.astype(BF16)
    w_gate_lo = (w_gate - w_gate_hi.astype(F32)).astype(BF16)
    w_in_p = jnp.concatenate(
        [w[:, :g0].astype(BF16), w_gate_hi, w[:, g0 + n_gate:].astype(BF16), w_gate_lo], axis=1)
    bs_full = jnp.repeat(jnp.swapaxes(b_s[l], 0, 1), SGU_DH, axis=1)
    reps = SGU_CHUNK // sample_rows
    eye = jnp.eye(len(POOL_WINDOWS), dtype=F32)
    dg = POOL_WIDTH // len(POOL_WINDOWS)
    w_pool_bd = (eye[:, None, :, None] * w_pool[l][:, :, None, :]).reshape(POOL_WIDTH, POOL_WIDTH)
    row = lambda a: a.reshape(1, -1)
    return {
        "w_in": w_in_p,
        "w_gate": w_gate_hi,
        "b_gate": jnp.pad(b_gate[l], (0, LANES - n_gate)).reshape(1, LANES),
        "mh_g": row(mh_g[l]), "sgu_g": row(sgu_g[l]), "sgu_b": row(sgu_b[l]),
        "w_s": w_s[l],
        "b_s": bs_full,
        "w_s_sample": jnp.tile(w_s[l][:, :sample_rows, :sample_rows], (1, reps, reps)),
        "b_s_sample": jnp.tile(bs_full[:sample_rows], (reps, 1)),
        "w_pool": w_pool_bd.astype(BF16),
        "pool_scale": row(pool_scale[l]),
        "w_o": w_o[l].astype(BF16),
        "ln1_g": row(ln1_g[l]), "ln1_b": row(ln1_b[l]),
        "w_pq_t": w_pq[l].T.astype(BF16),
        "peer_keys": peer_keys[l].reshape(2 * PEER_HEADS, PEER_NKEYS, PEER_DK).astype(BF16),
        "ln2_g": row(ln2_g[l]), "ln2_b": row(ln2_b[l]),
    }


def kernel(x_prompt, x_sample, state_mlstm_C, state_mlstm_n, state_mlstm_m, state_pool, c_prompt, c_sample,
           w_ada, b_ada, w_in, b_gate, mh_g, sgu_g, sgu_b, w_s, b_s, w_pool, pool_scale, w_o, ln1_g, ln1_b,
           w_pq, peer_keys, peer_u, peer_v, ln2_g, ln2_b):
    bp, seq, _ = x_prompt.shape
    bs, dec_seq, _ = x_sample.shape
    sample_rows = SUBLANES
    prompt_rows = min(seq, MIXER_ROWS)
    peer_tn = PEER_TOKEN_TILE
    peer_te = PEER_EXPERT_TILE
    sel_tn = SELECT_TOKEN_TILE

    ada = _ada(jnp.concatenate([c_prompt, c_sample], axis=0), w_ada, b_ada)

    xp = x_prompt
    xs = jnp.pad(x_sample, ((0, 0), (0, sample_rows - dec_seq), (0, 0)))
    zeros_c = jnp.zeros((1, bp, MLSTM_HEADS, MLSTM_DH, MLSTM_DH), F32)
    zeros_n = jnp.zeros((bp, MLSTM_HEADS, MLSTM_DH), F32)
    zeros_m = jnp.zeros((bp, MLSTM_HEADS, LANES), F32)
    zeros_hist = jnp.zeros((bp, POOL_HIST, POOL_WIDTH), F32)
    peer_u_b = peer_u.astype(BF16)
    peer_v_t = jnp.swapaxes(peer_v, 1, 2).astype(BF16)
    outs = [[] for _ in range(9)]
    for l in range(DEPTH):
        lw = _prep_layer(l, w_in, b_gate, mh_g, sgu_g, sgu_b, w_s, b_s, w_pool, pool_scale, w_o, ln1_g, ln1_b,
                         w_pq, peer_keys, ln2_g, ln2_b, sample_rows)
        lw.update(peer_u=peer_u_b, peer_v_t=peer_v_t, layer=l)
        mods = jnp.pad(ada[l].reshape(bp + bs, N_ADA, D_MODEL), ((0, 0), (0, SUBLANES - N_ADA), (0, 0)))
        mod_p, mod_s = mods[:bp], mods[bp:]

        x1p, cp, np_, mp, tailp = _mixer(
            xp, mod_p, zeros_c, 0, zeros_n, zeros_m, zeros_hist, lw,
            nb=1, rows=prompt_rows, chunk=SGU_CHUNK, n_valid=SGU_CHUNK, sgu_blk=SGU_CHUNK, pos0=0,
            want_vn=False)
        m_in = jnp.broadcast_to(state_mlstm_m[l][:, :, None], (bs, MLSTM_HEADS, LANES))
        hist_s = jnp.pad(state_pool[l], ((0, 0), (1, 0), (0, 0)))
        x1s, cs, ns, ms, tails, vns = _mixer(
            xs, mod_s, state_mlstm_C, l, state_mlstm_n[l], m_in, hist_s, lw,
            nb=16, rows=sample_rows, chunk=sample_rows, n_valid=dec_seq, sgu_blk=sample_rows, pos0=PAST_LEN,
            want_vn=True)

        x1p2 = x1p.reshape(bp * seq, D_MODEL)
        shp, scp, gp = mod_p[:, 3:4], mod_p[:, 4:5], mod_p[:, 5:6]
        sel = _peer_select(x1p2, shp, scp, lw, tn=sel_tn, tiles_per_row=seq // sel_tn)
        xp = _peer_dense(x1p2, shp, scp, gp, sel, lw, tn=peer_tn, te=peer_te,
                         tiles_per_row=seq // peer_tn).reshape(bp, seq, D_MODEL)
        x1s2 = x1s[:, :dec_seq].reshape(bs * dec_seq, D_MODEL)
        shs, scs, gs = (jnp.repeat(mod_s[:, i], dec_seq, axis=0) for i in (3, 4, 5))
        n_s = bs * dec_seq
        sel = _peer_select(x1s2, shs, scs, lw, tn=min(sel_tn, n_s), tiles_per_row=1)
        xs_new = _peer_dense(x1s2, shs, scs, gs, sel, lw, tn=min(peer_tn, n_s), te=peer_te, tiles_per_row=1)
        xs_new = xs_new.reshape(bs, dec_seq, D_MODEL)
        xs = jnp.pad(xs_new, ((0, 0), (0, sample_rows - dec_seq), (0, 0)))

        for lst, val in zip(outs, (cp, np_, mp[:, :, 0], tailp[:, 1:], cs, ns, ms[:, :, 0], tails[:, 1:],
                                   vns[:, :dec_seq])):
            lst.append(val)
    return (xp, xs_new) + tuple(jnp.stack(o) for o in outs)
```

```python
import functools

import jax
import jax.numpy as jnp
from jax import lax
from jax.experimental import pallas as pl
from jax.experimental.pallas import tpu as pltpu

F32 = jnp.float32
BF16 = jnp.bfloat16
HIGHEST = lax.Precision.HIGHEST

D_MODEL = 1024
DEPTH = 2
N_ADA = 6
MLSTM_HEADS = 4
MLSTM_DH = 128
MLSTM_WIDTH = MLSTM_HEADS * MLSTM_DH
SGU_WIDTH = 256
SGU_HEADS = 4
SGU_DH = SGU_WIDTH // SGU_HEADS
SGU_CHUNK = 128
POOL_WIDTH = 256
POOL_WINDOWS = (2, 4, 8, 16)
POOL_HIST = 16
PEER_HEADS = 8
PEER_NKEYS = 128
PEER_TOPK = 16
PEER_DK = 128
PEER_EXPERTS = PEER_NKEYS * PEER_NKEYS
PAST_LEN = 16384
ALPHA = (2 * DEPTH) ** 0.25
LN_EPS = 1e-5

LANES = 128
SUBLANES = 8
VMEM_LIMIT_BYTES = 56 * 1024 * 1024

MIXER_ROWS = 256
SELECT_TOKEN_TILE = 512
PEER_TOKEN_TILE = 512
PEER_EXPERT_TILE = D_MODEL

COL_Q = 0
COL_K = MLSTM_WIDTH
COL_V = 2 * MLSTM_WIDTH
COL_O = 3 * MLSTM_WIDTH
COL_GATE = 4 * MLSTM_WIDTH
COL_U = COL_GATE + LANES
COL_VS = COL_U + SGU_WIDTH
COL_P = COL_VS + SGU_WIDTH
COL_GATE_LO = COL_P + POOL_WIDTH
IN_COLS_PADDED = COL_GATE_LO + LANES

NT_DIMS = (((1,), (1,)), ((), ()))
TN_DIMS = (((0,), (0,)), ((), ()))


def _split_bf16(x):
    hi = x.astype(BF16)
    lo = (x - hi.astype(F32)).astype(BF16)
    return hi, lo


def _layer_norm(x):
    mu = jnp.mean(x, axis=-1, keepdims=True)
    d = x - mu
    var = jnp.mean(d * d, axis=-1, keepdims=True)
    return d * lax.rsqrt(var + LN_EPS)


def _ada_kernel(c_ref, w_ref, b_ref, o_ref):
    c = c_ref[...]
    s = c * jax.nn.sigmoid(c)
    o_ref[0] = jnp.dot(s, w_ref[0], precision=HIGHEST, preferred_element_type=F32) + b_ref[0]


def _ada(c_all, w_ada, b_ada):
    rows = c_all.shape[0]
    cols = w_ada.shape[-1]
    tile = 1536
    return pl.pallas_call(
        _ada_kernel,
        grid=(DEPTH, cols // tile),
        in_specs=[
            pl.BlockSpec((rows, D_MODEL), lambda l, j: (0, 0)),
            pl.BlockSpec((1, D_MODEL, tile), lambda l, j: (l, 0, j)),
            pl.BlockSpec((1, 1, tile), lambda l, j: (l, 0, j)),
        ],
        out_specs=pl.BlockSpec((1, rows, tile), lambda l, j: (l, 0, j)),
        out_shape=jax.ShapeDtypeStruct((DEPTH, rows, cols), F32),
        compiler_params=pltpu.CompilerParams(vmem_limit_bytes=VMEM_LIMIT_BYTES),
        name="ada",
    )(c_all, w_ada, b_ada.reshape(DEPTH, 1, cols))


def _mlstm_chunk(q, k, v, ig_col, ig_row, b_col, b_row, causal, c_state, n_state, m_state):
    length = q.shape[0]
    dlog = jnp.where(causal, b_col - b_row + ig_row, -jnp.inf)
    inter = b_col + m_state
    m_t = jnp.maximum(inter, jnp.max(dlog, axis=1, keepdims=True))
    w_intra = jnp.exp(dlog - m_t)
    w_inter = jnp.exp(inter - m_t)
    qb = q.astype(BF16)
    kb = k.astype(BF16)
    scores = lax.dot_general(qb, kb, NT_DIMS, preferred_element_type=F32)
    a = w_intra * scores
    num = jnp.dot(a.astype(BF16), v.astype(BF16), preferred_element_type=F32)
    num = num + w_inter * lax.dot_general(qb, c_state.astype(BF16), NT_DIMS, preferred_element_type=F32)
    den = jnp.sum(a, axis=1, keepdims=True) + w_inter * jnp.sum(q * n_state, axis=1, keepdims=True)
    h = num / jnp.maximum(jnp.abs(den), jnp.exp(-m_t))
    b_end = b_col[length - 1:length, :]
    dend = b_end - b_col + ig_col
    m_new = jnp.maximum(b_end + m_state, jnp.max(dend, axis=0, keepdims=True))
    wc = jnp.exp(dend - m_new)
    dec = jnp.exp(b_end + m_state - m_new)
    vw = (v * wc).astype(BF16)
    c_new = dec * c_state + lax.dot_general(vw, kb, TN_DIMS, preferred_element_type=F32)
    n_new = dec * n_state + jnp.sum(k * wc, axis=0, keepdims=True)
    return h, c_new, n_new, m_new


def _mixer_kernel(x_ref, mod_ref, c_in_ref, n_in_ref, m_in_ref, hist_ref,
                  w_in_ref, wgate_ref, bgate_ref, mhg_ref, sgug_ref, sgub_ref, ws_ref, bs_ref,
                  wpool_ref, pscale_ref, wo_ref, ln1g_ref, ln1b_ref,
                  x1_ref, c_out_ref, n_out_ref, m_out_ref, tail_ref, vn_ref,
                  proj_sc, ycat_sc, c_sc, n_sc, m_sc, carry_sc, pbuf_sc,
                  *, nb, rows, chunk, n_valid, sgu_blk, pos0):
    ci = pl.program_id(1)
    n_chunks = pl.num_programs(1)
    m_rows = nb * rows

    @pl.when(ci == 0)
    def _():
        c_sc[...] = c_in_ref[...]
        n_sc[...] = n_in_ref[...]
        m_sc[...] = m_in_ref[...]
        carry_sc[...] = hist_ref[...]

    x3 = x_ref[...]
    mod = mod_ref[...]
    sh1, sc1, g1 = mod[:, 0:1, :], mod[:, 1:2, :], mod[:, 2:3, :]
    h3 = x3 * (1.0 + sc1) + sh1
    h2d = h3.reshape(m_rows, D_MODEL)
    h_hi = h2d.astype(BF16)
    proj_sc[...] = jnp.dot(h_hi, w_in_ref[...], preferred_element_type=F32)
    h_lo = (h2d - h_hi.astype(F32)).astype(BF16)
    proj_sc[:, COL_GATE:COL_GATE + LANES] = (
        proj_sc[:, COL_GATE:COL_GATE + LANES] + proj_sc[:, COL_GATE_LO:COL_GATE_LO + LANES]
        + jnp.dot(h_lo, wgate_ref[...], preferred_element_type=F32))

    r_io = lax.broadcasted_iota(jnp.int32, (chunk, chunk), 0)
    c_io = lax.broadcasted_iota(jnp.int32, (chunk, chunk), 1)
    causal = c_io <= r_io
    tri = jnp.where(causal, 1.0, 0.0).astype(BF16)
    row_id = lax.broadcasted_iota(jnp.int32, (chunk, 1), 0)
    valid_col = row_id < n_valid
    bgate = bgate_ref[...]
    mhg = mhg_ref[...]

    def seq_body(s, carry):
        for j in range(rows // chunk):
            row0 = s * rows + j * chunk
            if not isinstance(row0, int):
                row0 = pl.multiple_of(row0, SUBLANES)
            rsl = pl.ds(row0, chunk)
            gates = proj_sc[rsl, COL_GATE:COL_GATE + LANES] + bgate
            lf = jnp.where(valid_col, jax.nn.log_sigmoid(gates), 0.0)
            if chunk >= LANES:
                lf_hi, lf_lo = _split_bf16(lf)
                bcum = (jnp.dot(tri, lf_hi, preferred_element_type=F32)
                        + jnp.dot(tri, lf_lo, preferred_element_type=F32))
            else:
                bcum = jnp.dot(tri.astype(F32), lf, precision=HIGHEST, preferred_element_type=F32)
            ig_all = jnp.where(valid_col, gates, -jnp.inf)
            bcum_t = bcum.T
            ig_t = ig_all.T
            for h in range(MLSTM_HEADS):
                hs = slice(h * MLSTM_DH, (h + 1) * MLSTM_DH)
                q = proj_sc[rsl, COL_Q + h * MLSTM_DH:COL_Q + (h + 1) * MLSTM_DH] * (MLSTM_DH ** -0.5)
                k = proj_sc[rsl, COL_K + h * MLSTM_DH:COL_K + (h + 1) * MLSTM_DH]
                v = proj_sc[rsl, COL_V + h * MLSTM_DH:COL_V + (h + 1) * MLSTM_DH]
                o = proj_sc[rsl, COL_O + h * MLSTM_DH:COL_O + (h + 1) * MLSTM_DH]
                fcol = MLSTM_HEADS + h
                hh, c_new, n_new, m_new = _mlstm_chunk(
                    q, k, v,
                    ig_all[:, h:h + 1], ig_t[h:h + 1, 0:chunk],
                    bcum[:, fcol:fcol + 1], bcum_t[fcol:fcol + 1, 0:chunk],
                    causal, c_sc[s, h], n_sc[s, pl.ds(h, 1), :], m_sc[s, pl.ds(h, 1), 0:1])
                c_sc[s, h] = c_new
                n_sc[s, pl.ds(h, 1), :] = n_new
                m_sc[s, pl.ds(h, 1), :] = jnp.broadcast_to(m_new, (1, LANES))
                hn = _layer_norm(hh) * mhg[:, hs]
                ycat_sc[rsl, hs] = jax.nn.sigmoid(o) * hn
        return carry

    if nb == 1:
        seq_body(0, 0)
    else:
        lax.fori_loop(0, nb, seq_body, 0)

    u_s = proj_sc[:, COL_U:COL_U + SGU_WIDTH]
    v_s = proj_sc[:, COL_VS:COL_VS + SGU_WIDTH]
    gr = lax.broadcasted_iota(jnp.int32, (SGU_WIDTH, SGU_WIDTH), 0) // SGU_DH
    gc = lax.broadcasted_iota(jnp.int32, (SGU_WIDTH, SGU_WIDTH), 1) // SGU_DH
    avg = jnp.where(gr == gc, 1.0 / SGU_DH, 0.0).astype(BF16)

    def group_mean(t):
        hi, lo = _split_bf16(t)
        return jnp.dot(hi, avg, preferred_element_type=F32) + jnp.dot(lo, avg, preferred_element_type=F32)

    mu = group_mean(v_s)
    dv = v_s - mu
    var = group_mean(dv * dv)
    vn = dv * lax.rsqrt(var + LN_EPS) * sgug_ref[...] + sgub_ref[...]
    if vn_ref is not None:
        vn_ref[...] = vn.reshape(nb, rows, SGU_WIDTH)
    sr = lax.broadcasted_iota(jnp.int32, (SGU_CHUNK, SGU_CHUNK), 0)
    scol = lax.broadcasted_iota(jnp.int32, (SGU_CHUNK, SGU_CHUNK), 1)
    smask = (scol <= sr) & ((sr // sgu_blk) == (scol // sgu_blk))
    lane_grp = lax.broadcasted_iota(jnp.int32, (SGU_CHUNK, SGU_WIDTH), 1) // SGU_DH
    vnb = vn.astype(BF16)
    for r in range(m_rows // SGU_CHUNK):
        rs = slice(r * SGU_CHUNK, (r + 1) * SGU_CHUNK)
        mix = jnp.zeros((SGU_CHUNK, SGU_WIDTH), F32)
        for g in range(SGU_HEADS):
            wg = jnp.where(smask, ws_ref[g], 0.0).astype(BF16)
            mg = jnp.dot(wg, vnb[rs], preferred_element_type=F32)
            mix = jnp.where(lane_grp == g, mg, mix)
        ycat_sc[rs, MLSTM_WIDTH:MLSTM_WIDTH + SGU_WIDTH] = u_s[rs] * (mix + bs_ref[...])

    pbuf_sc[:, 0:POOL_HIST, :] = carry_sc[...]
    pbuf_sc[:, POOL_HIST:POOL_HIST + rows, :] = proj_sc[:, COL_P:COL_P + POOL_WIDTH].reshape(nb, rows, POOL_WIDTH)
    lane_w = lax.broadcasted_iota(jnp.int32, (1, 1, POOL_WIDTH), 2) // (POOL_WIDTH // len(POOL_WINDOWS))
    x0 = pbuf_sc[:, POOL_HIST:POOL_HIST + rows, :]
    acc = x0
    wsum = jnp.zeros_like(x0)
    for kk in range(1, max(POOL_WINDOWS)):
        acc = acc + pbuf_sc[:, POOL_HIST - kk:POOL_HIST - kk + rows, :]
        if (kk + 1) in POOL_WINDOWS:
            wsum = jnp.where(lane_w == POOL_WINDOWS.index(kk + 1), acc, wsum)
    t_io = lax.broadcasted_iota(jnp.int32, (1, rows, POOL_WIDTH), 1)
    win = jnp.left_shift(2, lane_w)
    cnt = jnp.minimum(pos0 + ci * rows + t_io + 1, win).astype(F32)
    pooled = wsum / cnt - x0
    y_c = jnp.dot(pooled.reshape(m_rows, POOL_WIDTH).astype(BF16), wpool_ref[...],
                  preferred_element_type=F32) * pscale_ref[...]
    ycat_sc[:, MLSTM_WIDTH + SGU_WIDTH:D_MODEL] = y_c
    carry_sc[...] = pbuf_sc[:, rows:rows + POOL_HIST, :]

    y = jnp.dot(ycat_sc[...].astype(BF16), wo_ref[...], preferred_element_type=F32)
    z = ALPHA * x3 + g1 * y.reshape(nb, rows, D_MODEL)
    x1_ref[...] = _layer_norm(z) * ln1g_ref[...] + ln1b_ref[...]

    @pl.when(ci == n_chunks - 1)
    def _():
        c_out_ref[...] = c_sc[...]
        n_out_ref[...] = n_sc[...]
        m_out_ref[...] = m_sc[...]
        tail_ref[...] = pbuf_sc[:, n_valid + rows - chunk:n_valid + rows - chunk + POOL_HIST, :]


def _mixer(x, mod, c0, c0_layer, n0, m0, hist, lw, *, nb, rows, chunk, n_valid, sgu_blk, pos0, want_vn):
    bsz, seq, _ = x.shape
    grid = (bsz // nb, seq // rows)
    m_rows = nb * rows
    kern = functools.partial(_mixer_kernel, nb=nb, rows=rows, chunk=chunk, n_valid=n_valid,
                             sgu_blk=sgu_blk, pos0=pos0)
    if not want_vn:
        def kern(*refs, _k=kern):
            return _k(*refs[:24], None, *refs[24:])

    def full(shape):
        return pl.BlockSpec(shape, lambda b, c: (0,) * len(shape))

    per_b3 = lambda s1, s2: pl.BlockSpec((nb, s1, s2), lambda b, c: (b, 0, 0))
    in_specs = [
        pl.BlockSpec((nb, rows, D_MODEL), lambda b, c: (b, c, 0)),
        per_b3(SUBLANES, D_MODEL),
        pl.BlockSpec((None, nb, MLSTM_HEADS, MLSTM_DH, MLSTM_DH), lambda b, c: (c0_layer, b, 0, 0, 0)),
        per_b3(MLSTM_HEADS, MLSTM_DH),
        per_b3(MLSTM_HEADS, LANES),
        per_b3(POOL_HIST, POOL_WIDTH),
        full((D_MODEL, IN_COLS_PADDED)),
        full((D_MODEL, LANES)),
        full((1, LANES)),
        full((1, MLSTM_WIDTH)),
        full((1, SGU_WIDTH)),
        full((1, SGU_WIDTH)),
        full((SGU_HEADS, SGU_CHUNK, SGU_CHUNK)),
        full((SGU_CHUNK, SGU_WIDTH)),
        full((POOL_WIDTH, POOL_WIDTH)),
        full((1, POOL_WIDTH)),
        full((D_MODEL, D_MODEL)),
        full((1, D_MODEL)),
        full((1, D_MODEL)),
    ]
    out_specs = [
        pl.BlockSpec((nb, rows, D_MODEL), lambda b, c: (b, c, 0)),
        pl.BlockSpec((nb, MLSTM_HEADS, MLSTM_DH, MLSTM_DH), lambda b, c: (b, 0, 0, 0)),
        per_b3(MLSTM_HEADS, MLSTM_DH),
        per_b3(MLSTM_HEADS, LANES),
        per_b3(POOL_HIST, POOL_WIDTH),
    ]
    out_shape = [
        jax.ShapeDtypeStruct((bsz, seq, D_MODEL), F32),
        jax.ShapeDtypeStruct((bsz, MLSTM_HEADS, MLSTM_DH, MLSTM_DH), F32),
        jax.ShapeDtypeStruct((bsz, MLSTM_HEADS, MLSTM_DH), F32),
        jax.ShapeDtypeStruct((bsz, MLSTM_HEADS, LANES), F32),
        jax.ShapeDtypeStruct((bsz, POOL_HIST, POOL_WIDTH), F32),
    ]
    if want_vn:
        out_specs.append(pl.BlockSpec((nb, rows, SGU_WIDTH), lambda b, c: (b, c, 0)))
        out_shape.append(jax.ShapeDtypeStruct((bsz, seq, SGU_WIDTH), F32))
    scratch = [
        pltpu.VMEM((m_rows, IN_COLS_PADDED), F32),
        pltpu.VMEM((m_rows, D_MODEL), F32),
        pltpu.VMEM((nb, MLSTM_HEADS, MLSTM_DH, MLSTM_DH), F32),
        pltpu.VMEM((nb, MLSTM_HEADS, MLSTM_DH), F32),
        pltpu.VMEM((nb, MLSTM_HEADS, LANES), F32),
        pltpu.VMEM((nb, POOL_HIST, POOL_WIDTH), F32),
        pltpu.VMEM((nb, POOL_HIST + rows, POOL_WIDTH), F32),
    ]
    return pl.pallas_call(
        kern,
        grid=grid,
        in_specs=in_specs,
        out_specs=out_specs,
        out_shape=out_shape,
        scratch_shapes=scratch,
        compiler_params=pltpu.CompilerParams(
            dimension_semantics=("arbitrary", "arbitrary"), vmem_limit_bytes=VMEM_LIMIT_BYTES),
        name="mixer_sample" if want_vn else "mixer_prompt",
    )(x, mod, c0, n0, m0, hist, lw["w_in"], lw["w_gate"], lw["b_gate"], lw["mh_g"], lw["sgu_g"], lw["sgu_b"],
      lw["w_s_sample"] if want_vn else lw["w_s"], lw["b_s_sample"] if want_vn else lw["b_s"],
      lw["w_pool"], lw["pool_scale"], lw["w_o"], lw["ln1_g"], lw["ln1_b"])


def _sort16_pairs():
    pairs = []

    def merge(lo, hi, r):
        step = r * 2
        if step < hi - lo:
            merge(lo, hi, step)
            merge(lo + r, hi, step)
            for i in range(lo + r, hi - r, step):
                pairs.append((i, i + r))
        else:
            pairs.append((lo, lo + r))

    def sort(lo, hi):
        if hi - lo >= 1:
            mid = lo + (hi - lo) // 2
            sort(lo, mid)
            sort(mid + 1, hi)
            merge(lo, hi, 1)

    sort(0, PEER_TOPK - 1)
    return tuple(pairs)


SORT16_PAIRS = _sort16_pairs()


def _sort16_desc(vals):
    vals = list(vals)
    for i, j in SORT16_PAIRS:
        hi = jnp.maximum(vals[i], vals[j])
        lo = jnp.minimum(vals[i], vals[j])
        vals[i], vals[j] = hi, lo
    return vals


def _merge_top16(xs, ys):
    vals = [jnp.maximum(xs[i], ys[PEER_TOPK - 1 - i]) for i in range(PEER_TOPK)]
    d = PEER_TOPK // 2
    while d >= 1:
        for i in range(PEER_TOPK):
            if (i & d) == 0:
                hi = jnp.maximum(vals[i], vals[i + d])
                lo = jnp.minimum(vals[i], vals[i + d])
                vals[i], vals[i + d] = hi, lo
        d //= 2
    return vals


def _zero_of(words):
    bits = lax.bitcast_convert_type(words, jnp.uint32)
    bits = lax.shift_right_logical(lax.shift_right_logical(bits, jnp.uint32(16)), jnp.uint32(16))
    return lax.bitcast_convert_type(bits, F32)


def _top16_desc(tiles, zero_dep=None):
    tiles = list(tiles)
    neg = jnp.full_like(tiles[0], -jnp.inf)
    while len(tiles) % PEER_TOPK:
        tiles.append(neg)
    best = None
    for g in range(len(tiles) // PEER_TOPK):
        grp = tiles[g * PEER_TOPK:(g + 1) * PEER_TOPK]
        if zero_dep is not None:
            grp = [grp[0] + zero_dep] + grp[1:]
        grp = _sort16_desc(grp)
        best = grp if best is None else _merge_top16(best, grp)
        zero_dep = _zero_of(best[PEER_TOPK - 1])
    return best


S_PITCH = PEER_NKEYS + SUBLANES


def _peer_select_kernel(x_ref, sh_ref, sc_ref, wq_ref, keys_ref,
                        thr_ref, c1_ref, s2_ref, e2_ref,
                        q_sc, sa_sc, sb_sc, *, tn):
    s_sc = (sa_sc, sb_sc)
    x = x_ref[...]
    sh = sh_ref[...].reshape(-1, D_MODEL)
    sc = sc_ref[...].reshape(-1, D_MODEL)
    h2 = (x * (1.0 + sc) + sh).astype(BF16)
    q_sc[...] = lax.dot_general(wq_ref[...], h2, NT_DIMS, preferred_element_type=F32).astype(BF16)
    for h in range(PEER_HEADS):
        for p in range(2):
            r0 = (h * 2 + p) * PEER_DK
            s_hp = jnp.dot(keys_ref[h * 2 + p], q_sc[r0:r0 + PEER_DK, :], preferred_element_type=F32)
            for lg in range(tn // LANES):
                s_sc[p][lg, h * S_PITCH:h * S_PITCH + PEER_NKEYS, :] = s_hp[:, lg * LANES:(lg + 1) * LANES]
    dep = None
    for lg in range(tn // LANES):
        lanes = pl.ds(lg * LANES, LANES)
        tops = []
        for p in range(2):
            s_lg = s_sc[p].at[lg]
            tiles = [s_lg[pl.ds(i, PEER_HEADS, stride=S_PITCH), :] for i in range(PEER_NKEYS)]
            tops.append(_top16_desc(tiles, dep))
            dep = _zero_of(tops[-1][PEER_TOPK - 1])
        a, b = tops
        cands = []
        for r1 in range(PEER_TOPK):
            for r2 in range(PEER_TOPK // (r1 + 1)):
                cands.append(b[r2] + a[r1])
        c = _top16_desc(cands, dep)
        tau = c[PEER_TOPK - 1]
        z = jnp.zeros_like(tau)
        for kk in range(PEER_TOPK):
            z = z + jnp.exp(c[kk] - c[0])
        zinv = 0.5 / z
        dep = _zero_of(zinv)
        for h in range(PEER_HEADS):
            rows_in = slice(h * S_PITCH, h * S_PITCH + PEER_NKEYS)
            rows_out = slice(h * PEER_NKEYS, (h + 1) * PEER_NKEYS)
            s1 = sa_sc[lg, rows_in, :] + dep[0:1, :]
            s2 = sb_sc[lg, rows_in, :]
            tau_h = tau[h:h + 1, :]
            thr = jnp.full_like(s1, jnp.inf)
            for r in range(PEER_TOPK):
                cand = b[r][h:h + 1, :]
                thr = jnp.where((s1 + cand) >= tau_h, cand, thr)
            thr_ref[lg, rows_out, :] = thr
            dep = _zero_of(thr[0:SUBLANES, :])
            s2_ref[lg, rows_out, :] = s2
            c1_ref[lg, rows_out, :] = jnp.exp(s1 - a[0][h:h + 1, :]) * zinv[h:h + 1, :]
            e2_ref[lg, rows_out, :] = jnp.exp(s2 - b[0][h:h + 1, :])


def _mod_spec(arr, tn, tiles_per_row):
    if arr.ndim == 3:
        return pl.BlockSpec((1, 1, D_MODEL), lambda i, *_: (i // tiles_per_row, 0, 0))
    return pl.BlockSpec((tn, D_MODEL), lambda i, *_: (i, 0))


def _peer_select(x1, sh2, sc2, lw, *, tn, tiles_per_row):
    n_tok = x1.shape[0]
    assert n_tok % tn == 0
    sel_rows = PEER_HEADS * PEER_NKEYS
    n_lg = tn // LANES
    big = lambda: pl.BlockSpec((n_lg, sel_rows, LANES), lambda i: (i, 0, 0))
    return pl.pallas_call(
        functools.partial(_peer_select_kernel, tn=tn),
        grid=(n_tok // tn,),
        in_specs=[
            pl.BlockSpec((tn, D_MODEL), lambda i: (i, 0)),
            _mod_spec(sh2, tn, tiles_per_row),
            _mod_spec(sc2, tn, tiles_per_row),
            pl.BlockSpec((2 * PEER_HEADS * PEER_DK, D_MODEL), lambda i: (0, 0)),
            pl.BlockSpec((2 * PEER_HEADS, PEER_NKEYS, PEER_DK), lambda i: (0, 0, 0)),
        ],
        out_specs=[big(), big(), big(), big()],
        out_shape=[jax.ShapeDtypeStruct((n_tok // LANES, sel_rows, LANES), F32)] * 4,
        scratch_shapes=[
            pltpu.VMEM((2 * PEER_HEADS * PEER_DK, tn), BF16),
            pltpu.VMEM((tn // LANES, PEER_HEADS * S_PITCH, LANES), F32),
            pltpu.VMEM((tn // LANES, PEER_HEADS * S_PITCH, LANES), F32),
        ],
        compiler_params=pltpu.CompilerParams(
            dimension_semantics=("arbitrary",), vmem_limit_bytes=VMEM_LIMIT_BYTES),
        name="peer_select",
    )(x1, sh2, sc2, lw["w_pq_t"], lw["peer_keys"])


GATE_SUB = 32
PEER_STAGE = 2 * PEER_NKEYS
TILES_PER_STEP = 2


def _peer_dense_kernel(x_ref, sh_ref, sc_ref, g_ref, thr_ref, c1_ref, s2_ref, e2_ref,
                       ua_ref, ub_ref, vta_ref, vtb_prev_ref, lng_ref, lnb_ref, o_ref,
                       xb_sc, acc_sc, at_a, at_b, coef_a, coef_b, thrb_sc, c1b_sc, *, tn, te):
    j = pl.program_id(1)
    n_j = pl.num_programs(1) - 1
    stage = at_a.shape[0]
    n_st = te // stage
    keys_per_tile = te // PEER_NKEYS
    keys_per_step = TILES_PER_STEP * keys_per_tile
    out_chunk = D_MODEL // n_st
    assert n_st % 2 == 0

    @pl.when(j == 0)
    def _():
        sh = sh_ref[...].reshape(-1, D_MODEL)
        sc = sc_ref[...].reshape(-1, D_MODEL)
        xb_sc[...] = (x_ref[...] * (1.0 + sc) + sh).T.astype(BF16)
        acc_sc[...] = jnp.zeros_like(acc_sc)
        coef_b[...] = jnp.zeros_like(coef_b)

    def fill_key_tables():
        for lg in range(tn // LANES):
            for h in range(PEER_HEADS):
                krows = pl.ds(pl.multiple_of(h * PEER_NKEYS + j * keys_per_step, keys_per_step), keys_per_step)
                thr_t = thr_ref[lg, krows, :]
                c1_t = c1_ref[lg, krows, :]
                for k in range(keys_per_step):
                    thrb_sc[k, lg * PEER_HEADS + h] = jnp.broadcast_to(thr_t[k:k + 1, :], (SUBLANES, LANES))
                    c1b_sc[k, lg * PEER_HEADS + h] = jnp.broadcast_to(c1_t[k:k + 1, :], (SUBLANES, LANES))

    def scores(u_ref, st):
        rows = pl.ds(st * stage, stage)
        return jnp.dot(u_ref[rows, :], xb_sc[...], preferred_element_type=F32)

    def values_chunk(vt_ref, coef_ref, st):
        rows = pl.ds(st * out_chunk, out_chunk)
        acc_sc[rows, :] += jnp.dot(vt_ref[rows, :], coef_ref[...], preferred_element_type=F32)

    def gate_stage(key_base, st, at_ref, coef_ref):
        n_keys = stage // PEER_NKEYS
        gshape = (GATE_SUB // SUBLANES, SUBLANES, LANES)
        zero_dep = jnp.zeros((SUBLANES, LANES), F32)
        for lg in range(tn // LANES):
            lanes = slice(lg * LANES, (lg + 1) * LANES)
            for sub in range(PEER_NKEYS // GATE_SUB):
                gates = [jnp.broadcast_to(zero_dep[None], gshape) for _ in range(n_keys)]
                for h in range(PEER_HEADS):
                    hrows = slice(h * PEER_NKEYS + sub * GATE_SUB, h * PEER_NKEYS + (sub + 1) * GATE_SUB)
                    s2 = s2_ref[lg, hrows, :].reshape(gshape)
                    e2 = e2_ref[lg, hrows, :].reshape(gshape)
                    for qq in range(n_keys):
                        thrb = thrb_sc[key_base + st * n_keys + qq, lg * PEER_HEADS + h]
                        c1b = c1b_sc[key_base + st * n_keys + qq, lg * PEER_HEADS + h]
                        gates[qq] = gates[qq] + jnp.where(s2 >= thrb, e2 * c1b, 0.0)
                for qq in range(n_keys):
                    r0 = qq * PEER_NKEYS + sub * GATE_SUB
                    a = at_ref[r0:r0 + GATE_SUB, lanes]
                    act = a * (1.0 + lax.erf(a * (2.0 ** -0.5)))
                    out_rows = pl.ds(st * stage + r0, GATE_SUB)
                    coef = gates[qq].reshape(GATE_SUB, LANES) * act
                    coef_ref[out_rows, lanes] = coef.astype(BF16)
                bits = lax.bitcast_convert_type(coef[0:SUBLANES, :], jnp.uint32)
                bits = lax.shift_right_logical(lax.shift_right_logical(bits, jnp.uint32(16)), jnp.uint32(16))
                zero_dep = lax.bitcast_convert_type(bits, F32)

    def tile_phase(u_ref, u_next_ref, key_base, coef_cur, vt_prev_ref, coef_prev):
        for st in range(n_st):
            at_cur, at_nxt = (at_a, at_b) if st % 2 == 0 else (at_b, at_a)
            gate_stage(key_base, st, at_cur, coef_cur)
            if st + 1 < n_st:
                at_nxt[...] = scores(u_ref, st + 1)
            elif u_next_ref is not None:
                at_nxt[...] = scores(u_next_ref, 0)
            values_chunk(vt_prev_ref, coef_prev, st)

    @pl.when(j < n_j)
    def _():
        at_a[...] = scores(ua_ref, 0)
        fill_key_tables()
        tile_phase(ua_ref, ub_ref, 0, coef_a, vtb_prev_ref, coef_b)
        tile_phase(ub_ref, None, keys_per_tile, coef_b, vta_ref, coef_a)

    @pl.when(j == n_j)
    def _():
        for st in range(n_st):
            values_chunk(vtb_prev_ref, coef_b, st)
        g2 = g_ref[...].reshape(-1, D_MODEL)
        z = ALPHA * x_ref[...] + g2 * acc_sc[...].T
        o_ref[...] = _layer_norm(z) * lng_ref[...] + lnb_ref[...]


def _peer_dense(x1, sh2, sc2, g2, sel, lw, *, tn, te, tiles_per_row):
    n_tok = x1.shape[0]
    assert n_tok % tn == 0
    sel_rows = PEER_HEADS * PEER_NKEYS
    n_lg = tn // LANES
    thr, c1, s2, e2 = sel
    big = lambda: pl.BlockSpec((n_lg, sel_rows, LANES), lambda i, j: (i, 0, 0))
    n_j = PEER_EXPERTS // (TILES_PER_STEP * te)
    layer = lw["layer"]
    tile_a = lambda j: TILES_PER_STEP * jnp.minimum(j, n_j - 1)
    table = pltpu.VMEM((TILES_PER_STEP * te // PEER_NKEYS, n_lg * PEER_HEADS, SUBLANES, LANES), F32)
    return pl.pallas_call(
        functools.partial(_peer_dense_kernel, tn=tn, te=te),
        grid=(n_tok // tn, n_j + 1),
        in_specs=[
            pl.BlockSpec((tn, D_MODEL), lambda i, j: (i, 0)),
            _mod_spec(sh2, tn, tiles_per_row),
            _mod_spec(sc2, tn, tiles_per_row),
            _mod_spec(g2, tn, tiles_per_row),
            big(), big(), big(), big(),
            pl.BlockSpec((None, te, D_MODEL), lambda i, j: (layer, tile_a(j), 0)),
            pl.BlockSpec((None, te, D_MODEL), lambda i, j: (layer, tile_a(j) + 1, 0)),
            pl.BlockSpec((None, D_MODEL, te), lambda i, j: (layer, 0, tile_a(j))),
            pl.BlockSpec((None, D_MODEL, te), lambda i, j: (layer, 0, jnp.maximum(TILES_PER_STEP * j - 1, 0))),
            pl.BlockSpec((1, D_MODEL), lambda i, j: (0, 0)),
            pl.BlockSpec((1, D_MODEL), lambda i, j: (0, 0)),
        ],
        out_specs=pl.BlockSpec((tn, D_MODEL), lambda i, j: (i, 0)),
        out_shape=jax.ShapeDtypeStruct((n_tok, D_MODEL), F32),
        scratch_shapes=[
            pltpu.VMEM((D_MODEL, tn), BF16),
            pltpu.VMEM((D_MODEL, tn), F32),
            pltpu.VMEM((PEER_STAGE, tn), F32),
            pltpu.VMEM((PEER_STAGE, tn), F32),
            pltpu.VMEM((te, tn), BF16),
            pltpu.VMEM((te, tn), BF16),
            table, table,
        ],
        compiler_params=pltpu.CompilerParams(
            dimension_semantics=("arbitrary", "arbitrary"), vmem_limit_bytes=VMEM_LIMIT_BYTES),
        name="peer_dense",
    )(x1, sh2, sc2, g2, thr, c1, s2, e2, lw["peer_u"], lw["peer_u"], lw["peer_v_t"], lw["peer_v_t"],
      lw["ln2_g"], lw["ln2_b"])


def _prep_layer(l, w_in, b_gate, mh_g, sgu_g, sgu_b, w_s, b_s, w_pool, pool_scale, w_o, ln1_g, ln1_b,
                w_pq, peer_keys, ln2_g, ln2_b, sample_rows):
    n_gate = 2 * MLSTM_HEADS
    g0 = 4 * MLSTM_WIDTH
    w = w_in[l]
    w_gate = jnp.pad(w[:, g0:g0 + n_gate], ((0, 0), (0, LANES - n_gate)))
    w_gate_hi = w_gate.astype(BF16)
    w_gate_lo = (w_gate - w_gate_hi.astype(F32)).astype(BF16)
    w_in_p = jnp.concatenate(
        [w[:, :g0].astype(BF16), w_gate_hi, w[:, g0 + n_gate:].astype(BF16), w_gate_lo], axis=1)
    bs_full = jnp.repeat(jnp.swapaxes(b_s[l], 0, 1), SGU_DH, axis=1)
    reps = SGU_CHUNK // sample_rows
    eye = jnp.eye(len(POOL_WINDOWS), dtype=F32)
    dg = POOL_WIDTH // len(POOL_WINDOWS)
    w_pool_bd = (eye[:, None, :, None] * w_pool[l][:, :, None, :]).reshape(POOL_WIDTH, POOL_WIDTH)
    row = lambda a: a.reshape(1, -1)
    return {
        "w_in": w_in_p,
        "w_gate": w_gate_hi,
        "b_gate": jnp.pad(b_gate[l], (0, LANES - n_gate)).reshape(1, LANES),
        "mh_g": row(mh_g[l]), "sgu_g": row(sgu_g[l]), "sgu_b": row(sgu_b[l]),
        "w_s": w_s[l],
        "b_s": bs_full,
        "w_s_sample": jnp.tile(w_s[l][:, :sample_rows, :sample_rows], (1, reps, reps)),
        "b_s_sample": jnp.tile(bs_full[:sample_rows], (reps, 1)),
        "w_pool": w_pool_bd.astype(BF16),
        "pool_scale": row(pool_scale[l]),
        "w_o": w_o[l].astype(BF16),
        "ln1_g": row(ln1_g[l]), "ln1_b": row(ln1_b[l]),
        "w_pq_t": w_pq[l].T.astype(BF16),
        "peer_keys": peer_keys[l].reshape(2 * PEER_HEADS, PEER_NKEYS, PEER_DK).astype(BF16),
        "ln2_g": row(ln2_g[l]), "ln2_b": row(ln2_b[l]),
    }


def kernel(x_prompt, x_sample, state_mlstm_C, state_mlstm_n, state_mlstm_m, state_pool, c_prompt, c_sample,
           w_ada, b_ada, w_in, b_gate, mh_g, sgu_g, sgu_b, w_s, b_s, w_pool, pool_scale, w_o, ln1_g, ln1_b,
           w_pq, peer_keys, peer_u, peer_v, ln2_g, ln2_b):
    bp, seq, _ = x_prompt.shape
    bs, dec_seq, _ = x_sample.shape
    sample_rows = SUBLANES
    prompt_rows = min(seq, MIXER_ROWS)
    peer_tn = PEER_TOKEN_TILE
    peer_te = PEER_EXPERT_TILE
    sel_tn = SELECT_TOKEN_TILE

    ada = _ada(jnp.concatenate([c_prompt, c_sample], axis=0), w_ada, b_ada)

    xp = x_prompt
    xs = jnp.pad(x_sample, ((0, 0), (0, sample_rows - dec_seq), (0, 0)))
    zeros_c = jnp.zeros((1, bp, MLSTM_HEADS, MLSTM_DH, MLSTM_DH), F32)
    zeros_n = jnp.zeros((bp, MLSTM_HEADS, MLSTM_DH), F32)
    zeros_m = jnp.zeros((bp, MLSTM_HEADS, LANES), F32)
    zeros_hist = jnp.zeros((bp, POOL_HIST, POOL_WIDTH), F32)
    peer_u_b = peer_u.astype(BF16)
    peer_v_t = jnp.swapaxes(peer_v, 1, 2).astype(BF16)
    outs = [[] for _ in range(9)]
    for l in range(DEPTH):
        lw = _prep_layer(l, w_in, b_gate, mh_g, sgu_g, sgu_b, w_s, b_s, w_pool, pool_scale, w_o, ln1_g, ln1_b,
                         w_pq, peer_keys, ln2_g, ln2_b, sample_rows)
        lw.update(peer_u=peer_u_b, peer_v_t=peer_v_t, layer=l)
        mods = jnp.pad(ada[l].reshape(bp + bs, N_ADA, D_MODEL), ((0, 0), (0, SUBLANES - N_ADA), (0, 0)))
        mod_p, mod_s = mods[:bp], mods[bp:]

        x1p, cp, np_, mp, tailp = _mixer(
            xp, mod_p, zeros_c, 0, zeros_n, zeros_m, zeros_hist, lw,
            nb=1, rows=prompt_rows, chunk=SGU_CHUNK, n_valid=SGU_CHUNK, sgu_blk=SGU_CHUNK, pos0=0,
            want_vn=False)
        m_in = jnp.broadcast_to(state_mlstm_m[l][:, :, None], (bs, MLSTM_HEADS, LANES))
        hist_s = jnp.pad(state_pool[l], ((0, 0), (1, 0), (0, 0)))
        x1s, cs, ns, ms, tails, vns = _mixer(
            xs, mod_s, state_mlstm_C, l, state_mlstm_n[l], m_in, hist_s, lw,
            nb=16, rows=sample_rows, chunk=sample_rows, n_valid=dec_seq, sgu_blk=sample_rows, pos0=PAST_LEN,
            want_vn=True)

        x1p2 = x1p.reshape(bp * seq, D_MODEL)
        shp, scp, gp = mod_p[:, 3:4], mod_p[:, 4:5], mod_p[:, 5:6]
        sel = _peer_select(x1p2, shp, scp, lw, tn=sel_tn, tiles_per_row=seq // sel_tn)
        xp = _peer_dense(x1p2, shp, scp, gp, sel, lw, tn=peer_tn, te=peer_te,
                         tiles_per_row=seq // peer_tn).reshape(bp, seq, D_MODEL)
        x1s2 = x1s[:, :dec_seq].reshape(bs * dec_seq, D_MODEL)
        shs, scs, gs = (jnp.repeat(mod_s[:, i], dec_seq, axis=0) for i in (3, 4, 5))
        n_s = bs * dec_seq
        sel = _peer_select(x1s2, shs, scs, lw, tn=min(sel_tn, n_s), tiles_per_row=1)
        xs_new = _peer_dense(x1s2, shs, scs, gs, sel, lw, tn=min(peer_tn, n_s), te=peer_te, tiles_per_row=1)
        xs_new = xs_new.reshape(bs, dec_seq, D_MODEL)
        xs = jnp.pad(xs_new, ((0, 0), (0, sample_rows - dec_seq), (0, 0)))

        for lst, val in zip(outs, (cp, np_, mp[:, :, 0], tailp[:, 1:], cs, ns, ms[:, :, 0], tails[:, 1:],
                                   vns[:, :dec_seq])):
            lst.append(val)
    return (xp, xs_new) + tuple(jnp.stack(o) for o in outs)
```

```python
import functools

import jax
import jax.numpy as jnp
from jax import lax
from jax.experimental import pallas as pl
from jax.experimental.pallas import tpu as pltpu

F32 = jnp.float32
BF16 = jnp.bfloat16
HIGHEST = lax.Precision.HIGHEST

D_MODEL = 1024
DEPTH = 2
N_ADA = 6
MLSTM_HEADS = 4
MLSTM_DH = 128
MLSTM_WIDTH = MLSTM_HEADS * MLSTM_DH
SGU_WIDTH = 256
SGU_HEADS = 4
SGU_DH = SGU_WIDTH // SGU_HEADS
SGU_CHUNK = 128
POOL_WIDTH = 256
POOL_WINDOWS = (2, 4, 8, 16)
POOL_HIST = 16
PEER_HEADS = 8
PEER_NKEYS = 128
PEER_TOPK = 16
PEER_DK = 128
PEER_EXPERTS = PEER_NKEYS * PEER_NKEYS
PAST_LEN = 16384
ALPHA = (2 * DEPTH) ** 0.25
LN_EPS = 1e-5

LANES = 128
SUBLANES = 8
VMEM_LIMIT_BYTES = 56 * 1024 * 1024

MIXER_ROWS = 256
SELECT_TOKEN_TILE = 512
PEER_TOKEN_TILE = 512
PEER_EXPERT_TILE = D_MODEL

COL_Q = 0
COL_K = MLSTM_WIDTH
COL_V = 2 * MLSTM_WIDTH
COL_O = 3 * MLSTM_WIDTH
COL_GATE = 4 * MLSTM_WIDTH
COL_U = COL_GATE + LANES
COL_VS = COL_U + SGU_WIDTH
COL_P = COL_VS + SGU_WIDTH
COL_GATE_LO = COL_P + POOL_WIDTH
IN_COLS_PADDED = COL_GATE_LO + LANES

NT_DIMS = (((1,), (1,)), ((), ()))
TN_DIMS = (((0,), (0,)), ((), ()))


def _split_bf16(x):
    hi = x.astype(BF16)
    lo = (x - hi.astype(F32)).astype(BF16)
    return hi, lo


def _layer_norm(x):
    mu = jnp.mean(x, axis=-1, keepdims=True)
    d = x - mu
    var = jnp.mean(d * d, axis=-1, keepdims=True)
    return d * lax.rsqrt(var + LN_EPS)


def _ada_kernel(c_ref, w_ref, b_ref, o_ref):
    c = c_ref[...]
    s = c * jax.nn.sigmoid(c)
    o_ref[0] = jnp.dot(s, w_ref[0], precision=HIGHEST, preferred_element_type=F32) + b_ref[0]


def _ada(c_all, w_ada, b_ada):
    rows = c_all.shape[0]
    cols = w_ada.shape[-1]
    tile = 1536
    return pl.pallas_call(
        _ada_kernel,
        grid=(DEPTH, cols // tile),
        in_specs=[
            pl.BlockSpec((rows, D_MODEL), lambda l, j: (0, 0)),
            pl.BlockSpec((1, D_MODEL, tile), lambda l, j: (l, 0, j)),
            pl.BlockSpec((1, 1, tile), lambda l, j: (l, 0, j)),
        ],
        out_specs=pl.BlockSpec((1, rows, tile), lambda l, j: (l, 0, j)),
        out_shape=jax.ShapeDtypeStruct((DEPTH, rows, cols), F32),
        compiler_params=pltpu.CompilerParams(vmem_limit_bytes=VMEM_LIMIT_BYTES),
        name="ada",
    )(c_all, w_ada, b_ada.reshape(DEPTH, 1, cols))


def _mlstm_chunk_heads(heads, causal):
    qs, ks, vs, ig_cols, ig_rows, b_cols, b_rows, c_states, n_states, m_states = zip(*heads)
    n_h = len(qs)
    length = qs[0].shape[0]
    qb = [q.astype(BF16) for q in qs]
    kb = [k.astype(BF16) for k in ks]
    scores = [lax.dot_general(qb[i], kb[i], NT_DIMS, preferred_element_type=F32) for i in range(n_h)]
    q_c = [lax.dot_general(qb[i], c_states[i].astype(BF16), NT_DIMS, preferred_element_type=F32)
           for i in range(n_h)]
    a, w_inter, m_t = [], [], []
    for i in range(n_h):
        dlog = jnp.where(causal, b_cols[i] - b_rows[i] + ig_rows[i], -jnp.inf)
        inter = b_cols[i] + m_states[i]
        m_t.append(jnp.maximum(inter, jnp.max(dlog, axis=1, keepdims=True)))
        a.append(jnp.exp(dlog - m_t[i]) * scores[i])
        w_inter.append(jnp.exp(inter - m_t[i]))
    a_v = [jnp.dot(a[i].astype(BF16), vs[i].astype(BF16), preferred_element_type=F32) for i in range(n_h)]
    vw, dec, wc, m_new = [], [], [], []
    for i in range(n_h):
        b_end = b_cols[i][length - 1:length, :]
        dend = b_end - b_cols[i] + ig_cols[i]
        m_new.append(jnp.maximum(b_end + m_states[i], jnp.max(dend, axis=0, keepdims=True)))
        wc.append(jnp.exp(dend - m_new[i]))
        dec.append(jnp.exp(b_end + m_states[i] - m_new[i]))
        vw.append((vs[i] * wc[i]).astype(BF16))
    v_k = [lax.dot_general(vw[i], kb[i], TN_DIMS, preferred_element_type=F32) for i in range(n_h)]
    out = []
    for i in range(n_h):
        num = a_v[i] + w_inter[i] * q_c[i]
        den = (jnp.sum(a[i], axis=1, keepdims=True)
               + w_inter[i] * jnp.sum(qs[i] * n_states[i], axis=1, keepdims=True))
        h = num / jnp.maximum(jnp.abs(den), jnp.exp(-m_t[i]))
        c_new = dec[i] * c_states[i] + v_k[i]
        n_new = dec[i] * n_states[i] + jnp.sum(ks[i] * wc[i], axis=0, keepdims=True)
        out.append((h, c_new, n_new, m_new[i]))
    return out


def _mixer_kernel(x_ref, mod_ref, c_in_ref, n_in_ref, m_in_ref, hist_ref,
                  w_in_ref, wgate_ref, bgate_ref, mhg_ref, sgug_ref, sgub_ref, ws_ref, bs_ref,
                  wpool_ref, pscale_ref, wo_ref, ln1g_ref, ln1b_ref,
                  x1_ref, c_out_ref, n_out_ref, m_out_ref, tail_ref, vn_ref,
                  proj_sc, ycat_sc, c_sc, n_sc, m_sc, carry_sc, pbuf_sc,
                  *, nb, rows, chunk, n_valid, sgu_blk, pos0, single_chunk):
    ci = pl.program_id(1)
    n_chunks = pl.num_programs(1)
    m_rows = nb * rows
    if single_chunk:
        c_rd, n_rd, m_rd = c_in_ref, n_in_ref, m_in_ref
        c_wr, n_wr, m_wr = c_out_ref, n_out_ref, m_out_ref
    else:
        c_rd, n_rd, m_rd = c_sc, n_sc, m_sc
        c_wr, n_wr, m_wr = c_sc, n_sc, m_sc

    @pl.when(ci == 0)
    def _():
        if not single_chunk:
            c_sc[...] = c_in_ref[...]
            n_sc[...] = n_in_ref[...]
            m_sc[...] = m_in_ref[...]
        carry_sc[...] = hist_ref[...]

    x3 = x_ref[...]
    mod = mod_ref[...]
    sh1, sc1, g1 = mod[:, 0:1, :], mod[:, 1:2, :], mod[:, 2:3, :]
    h3 = x3 * (1.0 + sc1) + sh1
    h2d = h3.reshape(m_rows, D_MODEL)
    h_hi = h2d.astype(BF16)
    proj_sc[...] = jnp.dot(h_hi, w_in_ref[...], preferred_element_type=F32)
    h_lo = (h2d - h_hi.astype(F32)).astype(BF16)
    proj_sc[:, COL_GATE:COL_GATE + LANES] = (
        proj_sc[:, COL_GATE:COL_GATE + LANES] + proj_sc[:, COL_GATE_LO:COL_GATE_LO + LANES]
        + jnp.dot(h_lo, wgate_ref[...], preferred_element_type=F32))

    r_io = lax.broadcasted_iota(jnp.int32, (chunk, chunk), 0)
    c_io = lax.broadcasted_iota(jnp.int32, (chunk, chunk), 1)
    causal = c_io <= r_io
    tri = jnp.where(causal, 1.0, 0.0).astype(BF16)
    row_id = lax.broadcasted_iota(jnp.int32, (chunk, 1), 0)
    valid_col = row_id < n_valid
    bgate = bgate_ref[...]
    mhg = mhg_ref[...]

    seq_group = 2 if nb % 2 == 0 else 1

    def seq_body(sg, carry):
        for j in range(rows // chunk):
            heads, where = [], []
            for g in range(seq_group):
                s = sg * seq_group + g
                row0 = s * rows + j * chunk
                if not isinstance(row0, int):
                    row0 = pl.multiple_of(row0, SUBLANES)
                rsl = pl.ds(row0, chunk)
                gates = proj_sc[rsl, COL_GATE:COL_GATE + LANES] + bgate
                lf = jnp.where(valid_col, jax.nn.log_sigmoid(gates), 0.0)
                if chunk >= LANES:
                    lf_hi, lf_lo = _split_bf16(lf)
                    bcum = (jnp.dot(tri, lf_hi, preferred_element_type=F32)
                            + jnp.dot(tri, lf_lo, preferred_element_type=F32))
                else:
                    bcum = jnp.dot(tri.astype(F32), lf, precision=HIGHEST, preferred_element_type=F32)
                ig_all = jnp.where(valid_col, gates, -jnp.inf)
                bcum_t = bcum.T
                ig_t = ig_all.T
                for h in range(MLSTM_HEADS):
                    q = proj_sc[rsl, COL_Q + h * MLSTM_DH:COL_Q + (h + 1) * MLSTM_DH] * (MLSTM_DH ** -0.5)
                    k = proj_sc[rsl, COL_K + h * MLSTM_DH:COL_K + (h + 1) * MLSTM_DH]
                    v = proj_sc[rsl, COL_V + h * MLSTM_DH:COL_V + (h + 1) * MLSTM_DH]
                    fcol = MLSTM_HEADS + h
                    heads.append((q, k, v,
                                  ig_all[:, h:h + 1], ig_t[h:h + 1, 0:chunk],
                                  bcum[:, fcol:fcol + 1], bcum_t[fcol:fcol + 1, 0:chunk],
                                  c_rd[s, h], n_rd[s, pl.ds(h, 1), :], m_rd[s, pl.ds(h, 1), 0:1]))
                    where.append((s, h, rsl))
            for (s, h, rsl), (hh, c_new, n_new, m_new) in zip(where, _mlstm_chunk_heads(heads, causal)):
                hs = slice(h * MLSTM_DH, (h + 1) * MLSTM_DH)
                o = proj_sc[rsl, COL_O + h * MLSTM_DH:COL_O + (h + 1) * MLSTM_DH]
                c_wr[s, h] = c_new
                n_wr[s, pl.ds(h, 1), :] = n_new
                m_wr[s, pl.ds(h, 1), :] = jnp.broadcast_to(m_new, (1, LANES))
                hn = _layer_norm(hh) * mhg[:, hs]
                ycat_sc[rsl, hs] = jax.nn.sigmoid(o) * hn
        return carry

    if nb == seq_group:
        seq_body(0, 0)
    else:
        lax.fori_loop(0, nb // seq_group, seq_body, 0)

    u_s = proj_sc[:, COL_U:COL_U + SGU_WIDTH]
    v_s = proj_sc[:, COL_VS:COL_VS + SGU_WIDTH]
    gr = lax.broadcasted_iota(jnp.int32, (SGU_WIDTH, SGU_WIDTH), 0) // SGU_DH
    gc = lax.broadcasted_iota(jnp.int32, (SGU_WIDTH, SGU_WIDTH), 1) // SGU_DH
    avg = jnp.where(gr == gc, 1.0 / SGU_DH, 0.0).astype(BF16)

    def group_mean(t):
        hi, lo = _split_bf16(t)
        return jnp.dot(hi, avg, preferred_element_type=F32) + jnp.dot(lo, avg, preferred_element_type=F32)

    mu = group_mean(v_s)
    dv = v_s - mu
    var = group_mean(dv * dv)
    vn = dv * lax.rsqrt(var + LN_EPS) * sgug_ref[...] + sgub_ref[...]
    if vn_ref is not None:
        vn_ref[...] = vn.reshape(nb, rows, SGU_WIDTH)
    sr = lax.broadcasted_iota(jnp.int32, (SGU_CHUNK, SGU_CHUNK), 0)
    scol = lax.broadcasted_iota(jnp.int32, (SGU_CHUNK, SGU_CHUNK), 1)
    smask = (scol <= sr) & ((sr // sgu_blk) == (scol // sgu_blk))
    lane_grp = lax.broadcasted_iota(jnp.int32, (SGU_CHUNK, SGU_WIDTH), 1) // SGU_DH
    vnb = vn.astype(BF16)
    for r in range(m_rows // SGU_CHUNK):
        rs = slice(r * SGU_CHUNK, (r + 1) * SGU_CHUNK)
        mix = jnp.zeros((SGU_CHUNK, SGU_WIDTH), F32)
        for g in range(SGU_HEADS):
            wg = jnp.where(smask, ws_ref[g], 0.0).astype(BF16)
            mg = jnp.dot(wg, vnb[rs], preferred_element_type=F32)
            mix = jnp.where(lane_grp == g, mg, mix)
        ycat_sc[rs, MLSTM_WIDTH:MLSTM_WIDTH + SGU_WIDTH] = u_s[rs] * (mix + bs_ref[...])

    pbuf_sc[:, 0:POOL_HIST, :] = carry_sc[...]
    pbuf_sc[:, POOL_HIST:POOL_HIST + rows, :] = proj_sc[:, COL_P:COL_P + POOL_WIDTH].reshape(nb, rows, POOL_WIDTH)
    lane_w = lax.broadcasted_iota(jnp.int32, (1, 1, POOL_WIDTH), 2) // (POOL_WIDTH // len(POOL_WINDOWS))
    x0 = pbuf_sc[:, POOL_HIST:POOL_HIST + rows, :]
    acc = x0
    wsum = jnp.zeros_like(x0)
    for kk in range(1, max(POOL_WINDOWS)):
        acc = acc + pbuf_sc[:, POOL_HIST - kk:POOL_HIST - kk + rows, :]
        if (kk + 1) in POOL_WINDOWS:
            wsum = jnp.where(lane_w == POOL_WINDOWS.index(kk + 1), acc, wsum)
    t_io = lax.broadcasted_iota(jnp.int32, (1, rows, POOL_WIDTH), 1)
    win = jnp.left_shift(2, lane_w)
    cnt = jnp.minimum(pos0 + ci * rows + t_io + 1, win).astype(F32)
    pooled = wsum / cnt - x0
    y_c = jnp.dot(pooled.reshape(m_rows, POOL_WIDTH).astype(BF16), wpool_ref[...],
                  preferred_element_type=F32) * pscale_ref[...]
    ycat_sc[:, MLSTM_WIDTH + SGU_WIDTH:D_MODEL] = y_c
    carry_sc[...] = pbuf_sc[:, rows:rows + POOL_HIST, :]

    y = jnp.dot(ycat_sc[...].astype(BF16), wo_ref[...], preferred_element_type=F32)
    z = ALPHA * x3 + g1 * y.reshape(nb, rows, D_MODEL)
    x1_ref[...] = _layer_norm(z) * ln1g_ref[...] + ln1b_ref[...]

    @pl.when(ci == n_chunks - 1)
    def _():
        if not single_chunk:
            c_out_ref[...] = c_sc[...]
            n_out_ref[...] = n_sc[...]
            m_out_ref[...] = m_sc[...]
        tail_ref[...] = pbuf_sc[:, n_valid + rows - chunk:n_valid + rows - chunk + POOL_HIST, :]


def _mixer(x, mod, c0, c0_layer, n0, m0, hist, lw, *, nb, rows, chunk, n_valid, sgu_blk, pos0, want_vn):
    bsz, seq, _ = x.shape
    grid = (bsz // nb, seq // rows)
    m_rows = nb * rows
    kern = functools.partial(_mixer_kernel, nb=nb, rows=rows, chunk=chunk, n_valid=n_valid,
                             sgu_blk=sgu_blk, pos0=pos0, single_chunk=(seq == rows == chunk))
    if not want_vn:
        def kern(*refs, _k=kern):
            return _k(*refs[:24], None, *refs[24:])

    def full(shape):
        return pl.BlockSpec(shape, lambda b, c: (0,) * len(shape))

    per_b3 = lambda s1, s2: pl.BlockSpec((nb, s1, s2), lambda b, c: (b, 0, 0))
    in_specs = [
        pl.BlockSpec((nb, rows, D_MODEL), lambda b, c: (b, c, 0)),
        per_b3(SUBLANES, D_MODEL),
        pl.BlockSpec((None, nb, MLSTM_HEADS, MLSTM_DH, MLSTM_DH), lambda b, c: (c0_layer, b, 0, 0, 0)),
        per_b3(MLSTM_HEADS, MLSTM_DH),
        per_b3(MLSTM_HEADS, LANES),
        per_b3(POOL_HIST, POOL_WIDTH),
        full((D_MODEL, IN_COLS_PADDED)),
        full((D_MODEL, LANES)),
        full((1, LANES)),
        full((1, MLSTM_WIDTH)),
        full((1, SGU_WIDTH)),
        full((1, SGU_WIDTH)),
        full((SGU_HEADS, SGU_CHUNK, SGU_CHUNK)),
        full((SGU_CHUNK, SGU_WIDTH)),
        full((POOL_WIDTH, POOL_WIDTH)),
        full((1, POOL_WIDTH)),
        full((D_MODEL, D_MODEL)),
        full((1, D_MODEL)),
        full((1, D_MODEL)),
    ]
    out_specs = [
        pl.BlockSpec((nb, rows, D_MODEL), lambda b, c: (b, c, 0)),
        pl.BlockSpec((nb, MLSTM_HEADS, MLSTM_DH, MLSTM_DH), lambda b, c: (b, 0, 0, 0)),
        per_b3(MLSTM_HEADS, MLSTM_DH),
        per_b3(MLSTM_HEADS, LANES),
        per_b3(POOL_HIST, POOL_WIDTH),
    ]
    out_shape = [
        jax.ShapeDtypeStruct((bsz, seq, D_MODEL), F32),
        jax.ShapeDtypeStruct((bsz, MLSTM_HEADS, MLSTM_DH, MLSTM_DH), F32),
        jax.ShapeDtypeStruct((bsz, MLSTM_HEADS, MLSTM_DH), F32),
        jax.ShapeDtypeStruct((bsz, MLSTM_HEADS, LANES), F32),
        jax.ShapeDtypeStruct((bsz, POOL_HIST, POOL_WIDTH), F32),
    ]
    if want_vn:
        out_specs.append(pl.BlockSpec((nb, rows, SGU_WIDTH), lambda b, c: (b, c, 0)))
        out_shape.append(jax.ShapeDtypeStruct((bsz, seq, SGU_WIDTH), F32))
    scratch = [
        pltpu.VMEM((m_rows, IN_COLS_PADDED), F32),
        pltpu.VMEM((m_rows, D_MODEL), F32),
        pltpu.VMEM((nb, MLSTM_HEADS, MLSTM_DH, MLSTM_DH), F32),
        pltpu.VMEM((nb, MLSTM_HEADS, MLSTM_DH), F32),
        pltpu.VMEM((nb, MLSTM_HEADS, LANES), F32),
        pltpu.VMEM((nb, POOL_HIST, POOL_WIDTH), F32),
        pltpu.VMEM((nb, POOL_HIST + rows, POOL_WIDTH), F32),
    ]
    return pl.pallas_call(
        kern,
        grid=grid,
        in_specs=in_specs,
        out_specs=out_specs,
        out_shape=out_shape,
        scratch_shapes=scratch,
        compiler_params=pltpu.CompilerParams(
            dimension_semantics=("arbitrary", "arbitrary"), vmem_limit_bytes=VMEM_LIMIT_BYTES),
        name="mixer_sample" if want_vn else "mixer_prompt",
    )(x, mod, c0, n0, m0, hist, lw["w_in"], lw["w_gate"], lw["b_gate"], lw["mh_g"], lw["sgu_g"], lw["sgu_b"],
      lw["w_s_sample"] if want_vn else lw["w_s"], lw["b_s_sample"] if want_vn else lw["b_s"],
      lw["w_pool"], lw["pool_scale"], lw["w_o"], lw["ln1_g"], lw["ln1_b"])


def _sort16_pairs():
    pairs = []

    def merge(lo, hi, r):
        step = r * 2
        if step < hi - lo:
            merge(lo, hi, step)
            merge(lo + r, hi, step)
            for i in range(lo + r, hi - r, step):
                pairs.append((i, i + r))
        else:
            pairs.append((lo, lo + r))

    def sort(lo, hi):
        if hi - lo >= 1:
            mid = lo + (hi - lo) // 2
            sort(lo, mid)
            sort(mid + 1, hi)
            merge(lo, hi, 1)

    sort(0, PEER_TOPK - 1)
    return tuple(pairs)


SORT16_PAIRS = _sort16_pairs()


def _sort16_desc(vals):
    vals = list(vals)
    for i, j in SORT16_PAIRS:
        hi = jnp.maximum(vals[i], vals[j])
        lo = jnp.minimum(vals[i], vals[j])
        vals[i], vals[j] = hi, lo
    return vals


def _merge_top16(xs, ys):
    vals = [jnp.maximum(xs[i], ys[PEER_TOPK - 1 - i]) for i in range(PEER_TOPK)]
    d = PEER_TOPK // 2
    while d >= 1:
        for i in range(PEER_TOPK):
            if (i & d) == 0:
                hi = jnp.maximum(vals[i], vals[i + d])
                lo = jnp.minimum(vals[i], vals[i + d])
                vals[i], vals[i + d] = hi, lo
        d //= 2
    return vals


def _zero_of(words):
    bits = lax.bitcast_convert_type(words, jnp.uint32)
    bits = lax.shift_right_logical(lax.shift_right_logical(bits, jnp.uint32(16)), jnp.uint32(16))
    return lax.bitcast_convert_type(bits, F32)


def _top16_desc(tiles, zero_dep=None):
    tiles = list(tiles)
    neg = jnp.full_like(tiles[0], -jnp.inf)
    while len(tiles) % PEER_TOPK:
        tiles.append(neg)
    best = None
    for g in range(len(tiles) // PEER_TOPK):
        grp = tiles[g * PEER_TOPK:(g + 1) * PEER_TOPK]
        if zero_dep is not None:
            grp = [grp[0] + zero_dep] + grp[1:]
        grp = _sort16_desc(grp)
        best = grp if best is None else _merge_top16(best, grp)
        zero_dep = _zero_of(best[PEER_TOPK - 1])
    return best


S_PITCH = PEER_NKEYS + SUBLANES


def _peer_select_kernel(x_ref, sh_ref, sc_ref, wq_ref, keys_ref,
                        thr_ref, c1_ref, s2_ref, e2_ref,
                        q_sc, sa_sc, sb_sc, *, tn):
    s_sc = (sa_sc, sb_sc)
    x = x_ref[...]
    sh = sh_ref[...].reshape(-1, D_MODEL)
    sc = sc_ref[...].reshape(-1, D_MODEL)
    h2 = (x * (1.0 + sc) + sh).astype(BF16)
    q_sc[...] = lax.dot_general(wq_ref[...], h2, NT_DIMS, preferred_element_type=F32).astype(BF16)
    for h in range(PEER_HEADS):
        for p in range(2):
            r0 = (h * 2 + p) * PEER_DK
            s_hp = jnp.dot(keys_ref[h * 2 + p], q_sc[r0:r0 + PEER_DK, :], preferred_element_type=F32)
            for lg in range(tn // LANES):
                s_sc[p][lg, h * S_PITCH:h * S_PITCH + PEER_NKEYS, :] = s_hp[:, lg * LANES:(lg + 1) * LANES]
    dep = None
    for lg in range(tn // LANES):
        lanes = pl.ds(lg * LANES, LANES)
        tops = []
        for p in range(2):
            s_lg = s_sc[p].at[lg]
            tiles = [s_lg[pl.ds(i, PEER_HEADS, stride=S_PITCH), :] for i in range(PEER_NKEYS)]
            tops.append(_top16_desc(tiles, dep))
            dep = _zero_of(tops[-1][PEER_TOPK - 1])
        a, b = tops
        cands = []
        for r1 in range(PEER_TOPK):
            for r2 in range(PEER_TOPK // (r1 + 1)):
                cands.append(b[r2] + a[r1])
        c = _top16_desc(cands, dep)
        tau = c[PEER_TOPK - 1]
        z = jnp.zeros_like(tau)
        for kk in range(PEER_TOPK):
            z = z + jnp.exp(c[kk] - c[0])
        zinv = 0.5 / z
        dep = _zero_of(zinv)
        for h in range(PEER_HEADS):
            rows_in = slice(h * S_PITCH, h * S_PITCH + PEER_NKEYS)
            rows_out = slice(h * PEER_NKEYS, (h + 1) * PEER_NKEYS)
            s1 = sa_sc[lg, rows_in, :] + dep[0:1, :]
            s2 = sb_sc[lg, rows_in, :]
            tau_h = tau[h:h + 1, :]
            thr = jnp.full_like(s1, jnp.inf)
            for r in range(PEER_TOPK):
                cand = b[r][h:h + 1, :]
                thr = jnp.where((s1 + cand) >= tau_h, cand, thr)
            thr_ref[lg, rows_out, :] = thr
            dep = _zero_of(thr[0:SUBLANES, :])
            s2_ref[lg, rows_out, :] = s2
            c1_ref[lg, rows_out, :] = jnp.exp(s1 - a[0][h:h + 1, :]) * zinv[h:h + 1, :]
            e2_ref[lg, rows_out, :] = jnp.exp(s2 - b[0][h:h + 1, :])


def _mod_spec(arr, tn, tiles_per_row):
    if arr.ndim == 3:
        return pl.BlockSpec((1, 1, D_MODEL), lambda i, *_: (i // tiles_per_row, 0, 0))
    return pl.BlockSpec((tn, D_MODEL), lambda i, *_: (i, 0))


def _peer_select(x1, sh2, sc2, lw, *, tn, tiles_per_row):
    n_tok = x1.shape[0]
    assert n_tok % tn == 0
    sel_rows = PEER_HEADS * PEER_NKEYS
    n_lg = tn // LANES
    big = lambda: pl.BlockSpec((n_lg, sel_rows, LANES), lambda i: (i, 0, 0))
    return pl.pallas_call(
        functools.partial(_peer_select_kernel, tn=tn),
        grid=(n_tok // tn,),
        in_specs=[
            pl.BlockSpec((tn, D_MODEL), lambda i: (i, 0)),
            _mod_spec(sh2, tn, tiles_per_row),
            _mod_spec(sc2, tn, tiles_per_row),
            pl.BlockSpec((2 * PEER_HEADS * PEER_DK, D_MODEL), lambda i: (0, 0)),
            pl.BlockSpec((2 * PEER_HEADS, PEER_NKEYS, PEER_DK), lambda i: (0, 0, 0)),
        ],
        out_specs=[big(), big(), big(), big()],
        out_shape=[jax.ShapeDtypeStruct((n_tok // LANES, sel_rows, LANES), F32)] * 4,
        scratch_shapes=[
            pltpu.VMEM((2 * PEER_HEADS * PEER_DK, tn), BF16),
            pltpu.VMEM((tn // LANES, PEER_HEADS * S_PITCH, LANES), F32),
            pltpu.VMEM((tn // LANES, PEER_HEADS * S_PITCH, LANES), F32),
        ],
        compiler_params=pltpu.CompilerParams(
            dimension_semantics=("arbitrary",), vmem_limit_bytes=VMEM_LIMIT_BYTES),
        name="peer_select",
    )(x1, sh2, sc2, lw["w_pq_t"], lw["peer_keys"])


GATE_SUB = 32
PEER_STAGE = 2 * PEER_NKEYS
TILES_PER_STEP = 2


def _peer_dense_kernel(x_ref, sh_ref, sc_ref, g_ref, thr_ref, c1_ref, s2_ref, e2_ref,
                       ua_ref, ub_ref, vta_ref, vtb_prev_ref, lng_ref, lnb_ref, o_ref,
                       xb_sc, acc_sc, at_a, at_b, coef_a, coef_b, thrb_sc, c1b_sc, *, tn, te):
    j = pl.program_id(1)
    n_j = pl.num_programs(1) - 1
    stage = at_a.shape[0]
    n_st = te // stage
    keys_per_tile = te // PEER_NKEYS
    keys_per_step = TILES_PER_STEP * keys_per_tile
    out_chunk = D_MODEL // n_st
    assert n_st % 2 == 0

    @pl.when(j == 0)
    def _():
        sh = sh_ref[...].reshape(-1, D_MODEL)
        sc = sc_ref[...].reshape(-1, D_MODEL)
        xb_sc[...] = (x_ref[...] * (1.0 + sc) + sh).T.astype(BF16)
        acc_sc[...] = jnp.zeros_like(acc_sc)
        coef_b[...] = jnp.zeros_like(coef_b)

    def fill_key_tables():
        for lg in range(tn // LANES):
            for h in range(PEER_HEADS):
                krows = pl.ds(pl.multiple_of(h * PEER_NKEYS + j * keys_per_step, keys_per_step), keys_per_step)
                thr_t = thr_ref[lg, krows, :]
                c1_t = c1_ref[lg, krows, :]
                for k in range(keys_per_step):
                    thrb_sc[k, lg * PEER_HEADS + h] = jnp.broadcast_to(thr_t[k:k + 1, :], (SUBLANES, LANES))
                    c1b_sc[k, lg * PEER_HEADS + h] = jnp.broadcast_to(c1_t[k:k + 1, :], (SUBLANES, LANES))

    def scores(u_ref, st):
        rows = pl.ds(st * stage, stage)
        return jnp.dot(u_ref[rows, :], xb_sc[...], preferred_element_type=F32)

    def values_chunk(vt_ref, coef_ref, st):
        rows = pl.ds(st * out_chunk, out_chunk)
        acc_sc[rows, :] += jnp.dot(vt_ref[rows, :], coef_ref[...], preferred_element_type=F32)

    def gate_stage(key_base, st, at_ref, coef_ref):
        n_keys = stage // PEER_NKEYS
        gshape = (GATE_SUB // SUBLANES, SUBLANES, LANES)
        zero_dep = jnp.zeros((SUBLANES, LANES), F32)
        for lg in range(tn // LANES):
            lanes = slice(lg * LANES, (lg + 1) * LANES)
            for sub in range(PEER_NKEYS // GATE_SUB):
                gates = [jnp.broadcast_to(zero_dep[None], gshape) for _ in range(n_keys)]
                for h in range(PEER_HEADS):
                    hrows = slice(h * PEER_NKEYS + sub * GATE_SUB, h * PEER_NKEYS + (sub + 1) * GATE_SUB)
                    s2 = s2_ref[lg, hrows, :].reshape(gshape)
                    e2 = e2_ref[lg, hrows, :].reshape(gshape)
                    for qq in range(n_keys):
                        thrb = thrb_sc[key_base + st * n_keys + qq, lg * PEER_HEADS + h]
                        c1b = c1b_sc[key_base + st * n_keys + qq, lg * PEER_HEADS + h]
                        gates[qq] = gates[qq] + jnp.where(s2 >= thrb, e2 * c1b, 0.0)
                for qq in range(n_keys):
                    r0 = qq * PEER_NKEYS + sub * GATE_SUB
                    a = at_ref[r0:r0 + GATE_SUB, lanes]
                    act = a * (1.0 + lax.erf(a * (2.0 ** -0.5)))
                    out_rows = pl.ds(st * stage + r0, GATE_SUB)
                    coef = gates[qq].reshape(GATE_SUB, LANES) * act
                    coef_ref[out_rows, lanes] = coef.astype(BF16)
                bits = lax.bitcast_convert_type(coef[0:SUBLANES, :], jnp.uint32)
                bits = lax.shift_right_logical(lax.shift_right_logical(bits, jnp.uint32(16)), jnp.uint32(16))
                zero_dep = lax.bitcast_convert_type(bits, F32)

    def tile_phase(u_ref, u_next_ref, key_base, coef_cur, vt_prev_ref, coef_prev):
        for st in range(n_st):
            at_cur, at_nxt = (at_a, at_b) if st % 2 == 0 else (at_b, at_a)
            gate_stage(key_base, st, at_cur, coef_cur)
            if st + 1 < n_st:
                at_nxt[...] = scores(u_ref, st + 1)
            elif u_next_ref is not None:
                at_nxt[...] = scores(u_next_ref, 0)
            values_chunk(vt_prev_ref, coef_prev, st)

    @pl.when(j < n_j)
    def _():
        at_a[...] = scores(ua_ref, 0)
        fill_key_tables()
        tile_phase(ua_ref, ub_ref, 0, coef_a, vtb_prev_ref, coef_b)
        tile_phase(ub_ref, None, keys_per_tile, coef_b, vta_ref, coef_a)

    @pl.when(j == n_j)
    def _():
        for st in range(n_st):
            values_chunk(vtb_prev_ref, coef_b, st)
        g2 = g_ref[...].reshape(-1, D_MODEL)
        z = ALPHA * x_ref[...] + g2 * acc_sc[...].T
        o_ref[...] = _layer_norm(z) * lng_ref[...] + lnb_ref[...]


def _peer_dense(x1, sh2, sc2, g2, sel, lw, *, tn, te, tiles_per_row):
    n_tok = x1.shape[0]
    assert n_tok % tn == 0
    sel_rows = PEER_HEADS * PEER_NKEYS
    n_lg = tn // LANES
    thr, c1, s2, e2 = sel
    big = lambda: pl.BlockSpec((n_lg, sel_rows, LANES), lambda i, j: (i, 0, 0))
    n_j = PEER_EXPERTS // (TILES_PER_STEP * te)
    layer = lw["layer"]
    tile_a = lambda j: TILES_PER_STEP * jnp.minimum(j, n_j - 1)
    table = pltpu.VMEM((TILES_PER_STEP * te // PEER_NKEYS, n_lg * PEER_HEADS, SUBLANES, LANES), F32)
    return pl.pallas_call(
        functools.partial(_peer_dense_kernel, tn=tn, te=te),
        grid=(n_tok // tn, n_j + 1),
        in_specs=[
            pl.BlockSpec((tn, D_MODEL), lambda i, j: (i, 0)),
            _mod_spec(sh2, tn, tiles_per_row),
            _mod_spec(sc2, tn, tiles_per_row),
            _mod_spec(g2, tn, tiles_per_row),
            big(), big(), big(), big(),
            pl.BlockSpec((None, te, D_MODEL), lambda i, j: (layer, tile_a(j), 0)),
            pl.BlockSpec((None, te, D_MODEL), lambda i, j: (layer, tile_a(j) + 1, 0)),
            pl.BlockSpec((None, D_MODEL, te), lambda i, j: (layer, 0, tile_a(j))),
            pl.BlockSpec((None, D_MODEL, te), lambda i, j: (layer, 0, jnp.maximum(TILES_PER_STEP * j - 1, 0))),
            pl.BlockSpec((1, D_MODEL), lambda i, j: (0, 0)),
            pl.BlockSpec((1, D_MODEL), lambda i, j: (0, 0)),
        ],
        out_specs=pl.BlockSpec((tn, D_MODEL), lambda i, j: (i, 0)),
        out_shape=jax.ShapeDtypeStruct((n_tok, D_MODEL), F32),
        scratch_shapes=[
            pltpu.VMEM((D_MODEL, tn), BF16),
            pltpu.VMEM((D_MODEL, tn), F32),
            pltpu.VMEM((PEER_STAGE, tn), F32),
            pltpu.VMEM((PEER_STAGE, tn), F32),
            pltpu.VMEM((te, tn), BF16),
            pltpu.VMEM((te, tn), BF16),
            table, table,
        ],
        compiler_params=pltpu.CompilerParams(
            dimension_semantics=("arbitrary", "arbitrary"), vmem_limit_bytes=VMEM_LIMIT_BYTES),
        name="peer_dense",
    )(x1, sh2, sc2, g2, thr, c1, s2, e2, lw["peer_u"], lw["peer_u"], lw["peer_v_t"], lw["peer_v_t"],
      lw["ln2_g"], lw["ln2_b"])


def _prep_layer(l, w_in, b_gate, mh_g, sgu_g, sgu_b, w_s, b_s, w_pool, pool_scale, w_o, ln1_g, ln1_b,
                w_pq, peer_keys, ln2_g, ln2_b, sample_rows):
    n_gate = 2 * MLSTM_HEADS
    g0 = 4 * MLSTM_WIDTH
    w = w_in[l]
    w_gate = jnp.pad(w[:, g0:g0 + n_gate], ((0, 0), (0, LANES - n_gate)))
    w_gate_hi = w_gate.astype(BF16)
    w_gate_lo = (w_gate - w_gate_hi.astype(F32)).astype(BF16)
    w_in_p = jnp.concatenate(
        [w[:, :g0].astype(BF16), w_gate_hi, w[:, g0 + n_gate:].astype(BF16), w_gate_lo], axis=1)
    bs_full = jnp.repeat(jnp.swapaxes(b_s[l], 0, 1), SGU_DH, axis=1)
    reps = SGU_CHUNK // sample_rows
    eye = jnp.eye(len(POOL_WINDOWS), dtype=F32)
    dg = POOL_WIDTH // len(POOL_WINDOWS)
    w_pool_bd = (eye[:, None, :, None] * w_pool[l][:, :, None, :]).reshape(POOL_WIDTH, POOL_WIDTH)
    row = lambda a: a.reshape(1, -1)
    return {
        "w_in": w_in_p,
        "w_gate": w_gate_hi,
        "b_gate": jnp.pad(b_gate[l], (0, LANES - n_gate)).reshape(1, LANES),
        "mh_g": row(mh_g[l]), "sgu_g": row(sgu_g[l]), "sgu_b": row(sgu_b[l]),
        "w_s": w_s[l],
        "b_s": bs_full,
        "w_s_sample": jnp.tile(w_s[l][:, :sample_rows, :sample_rows], (1, reps, reps)),
        "b_s_sample": jnp.tile(bs_full[:sample_rows], (reps, 1)),
        "w_pool": w_pool_bd.astype(BF16),
        "pool_scale": row(pool_scale[l]),
        "w_o": w_o[l].astype(BF16),
        "ln1_g": row(ln1_g[l]), "ln1_b": row(ln1_b[l]),
        "w_pq_t": w_pq[l].T.astype(BF16),
        "peer_keys": peer_keys[l].reshape(2 * PEER_HEADS, PEER_NKEYS, PEER_DK).astype(BF16),
        "ln2_g": row(ln2_g[l]), "ln2_b": row(ln2_b[l]),
    }


def kernel(x_prompt, x_sample, state_mlstm_C, state_mlstm_n, state_mlstm_m, state_pool, c_prompt, c_sample,
           w_ada, b_ada, w_in, b_gate, mh_g, sgu_g, sgu_b, w_s, b_s, w_pool, pool_scale, w_o, ln1_g, ln1_b,
           w_pq, peer_keys, peer_u, peer_v, ln2_g, ln2_b):
    bp, seq, _ = x_prompt.shape
    bs, dec_seq, _ = x_sample.shape
    sample_rows = SUBLANES
    prompt_rows = min(seq, MIXER_ROWS)
    peer_tn = PEER_TOKEN_TILE
    peer_te = PEER_EXPERT_TILE
    sel_tn = SELECT_TOKEN_TILE

    ada = _ada(jnp.concatenate([c_prompt, c_sample], axis=0), w_ada, b_ada)

    xp = x_prompt
    xs = jnp.pad(x_sample, ((0, 0), (0, sample_rows - dec_seq), (0, 0)))
    zeros_c = jnp.zeros((1, bp, MLSTM_HEADS, MLSTM_DH, MLSTM_DH), F32)
    zeros_n = jnp.zeros((bp, MLSTM_HEADS, MLSTM_DH), F32)
    zeros_m = jnp.zeros((bp, MLSTM_HEADS, LANES), F32)
    zeros_hist = jnp.zeros((bp, POOL_HIST, POOL_WIDTH), F32)
    peer_u_b = peer_u.astype(BF16)
    peer_v_t = jnp.swapaxes(peer_v, 1, 2).astype(BF16)
    outs = [[] for _ in range(9)]
    for l in range(DEPTH):
        lw = _prep_layer(l, w_in, b_gate, mh_g, sgu_g, sgu_b, w_s, b_s, w_pool, pool_scale, w_o, ln1_g, ln1_b,
                         w_pq, peer_keys, ln2_g, ln2_b, sample_rows)
        lw.update(peer_u=peer_u_b, peer_v_t=peer_v_t, layer=l)
        mods = jnp.pad(ada[l].reshape(bp + bs, N_ADA, D_MODEL), ((0, 0), (0, SUBLANES - N_ADA), (0, 0)))
        mod_p, mod_s = mods[:bp], mods[bp:]

        x1p, cp, np_, mp, tailp = _mixer(
            xp, mod_p, zeros_c, 0, zeros_n, zeros_m, zeros_hist, lw,
            nb=1, rows=prompt_rows, chunk=SGU_CHUNK, n_valid=SGU_CHUNK, sgu_blk=SGU_CHUNK, pos0=0,
            want_vn=False)
        m_in = jnp.broadcast_to(state_mlstm_m[l][:, :, None], (bs, MLSTM_HEADS, LANES))
        hist_s = jnp.pad(state_pool[l], ((0, 0), (1, 0), (0, 0)))
        x1s, cs, ns, ms, tails, vns = _mixer(
            xs, mod_s, state_mlstm_C, l, state_mlstm_n[l], m_in, hist_s, lw,
            nb=16, rows=sample_rows, chunk=sample_rows, n_valid=dec_seq, sgu_blk=sample_rows, pos0=PAST_LEN,
            want_vn=True)

        x1p2 = x1p.reshape(bp * seq, D_MODEL)
        shp, scp, gp = mod_p[:, 3:4], mod_p[:, 4:5], mod_p[:, 5:6]
        sel = _peer_select(x1p2, shp, scp, lw, tn=sel_tn, tiles_per_row=seq // sel_tn)
        xp = _peer_dense(x1p2, shp, scp, gp, sel, lw, tn=peer_tn, te=peer_te,
                         tiles_per_row=seq // peer_tn).reshape(bp, seq, D_MODEL)
        x1s2 = x1s[:, :dec_seq].reshape(bs * dec_seq, D_MODEL)
        shs, scs, gs = (jnp.repeat(mod_s[:, i], dec_seq, axis=0) for i in (3, 4, 5))
        n_s = bs * dec_seq
        sel = _peer_select(x1s2, shs, scs, lw, tn=min(sel_tn, n_s), tiles_per_row=1)
        xs_new = _peer_dense(x1s2, shs, scs, gs, sel, lw, tn=min(peer_tn, n_s), te=peer_te, tiles_per_row=1)
        xs_new = xs_new.reshape(bs, dec_seq, D_MODEL)
        xs = jnp.pad(xs_new, ((0, 0), (0, sample_rows - dec_seq), (0, 0)))

        for lst, val in zip(outs, (cp, np_, mp[:, :, 0], tailp[:, 1:], cs, ns, ms[:, :, 0], tails[:, 1:],
                                   vns[:, :dec_seq])):
            lst.append(val)
    return (xp, xs_new) + tuple(jnp.stack(o) for o in outs)
```

```python
import functools

import jax
import jax.numpy as jnp
from jax import lax
from jax.experimental import pallas as pl
from jax.experimental.pallas import tpu as pltpu

F32 = jnp.float32
BF16 = jnp.bfloat16
HIGHEST = lax.Precision.HIGHEST

D_MODEL = 1024
DEPTH = 2
N_ADA = 6
MLSTM_HEADS = 4
MLSTM_DH = 128
MLSTM_WIDTH = MLSTM_HEADS * MLSTM_DH
SGU_WIDTH = 256
SGU_HEADS = 4
SGU_DH = SGU_WIDTH // SGU_HEADS
SGU_CHUNK = 128
POOL_WIDTH = 256
POOL_WINDOWS = (2, 4, 8, 16)
POOL_HIST = 16
PEER_HEADS = 8
PEER_NKEYS = 128
PEER_TOPK = 16
PEER_DK = 128
PEER_EXPERTS = PEER_NKEYS * PEER_NKEYS
PAST_LEN = 16384
ALPHA = (2 * DEPTH) ** 0.25
LN_EPS = 1e-5

LANES = 128
SUBLANES = 8
VMEM_LIMIT_BYTES = 56 * 1024 * 1024

MIXER_ROWS = 256
SELECT_TOKEN_TILE = 512
PEER_TOKEN_TILE = 256
PEER_EXPERT_TILE = D_MODEL

COL_Q = 0
COL_K = MLSTM_WIDTH
COL_V = 2 * MLSTM_WIDTH
COL_O = 3 * MLSTM_WIDTH
COL_GATE = 4 * MLSTM_WIDTH
COL_U = COL_GATE + LANES
COL_VS = COL_U + SGU_WIDTH
COL_P = COL_VS + SGU_WIDTH
COL_GATE_LO = COL_P + POOL_WIDTH
IN_COLS_PADDED = COL_GATE_LO + LANES

NT_DIMS = (((1,), (1,)), ((), ()))
TN_DIMS = (((0,), (0,)), ((), ()))


def _split_bf16(x):
    hi = x.astype(BF16)
    lo = (x - hi.astype(F32)).astype(BF16)
    return hi, lo


def _layer_norm(x):
    mu = jnp.mean(x, axis=-1, keepdims=True)
    d = x - mu
    var = jnp.mean(d * d, axis=-1, keepdims=True)
    return d * lax.rsqrt(var + LN_EPS)


def _ada_kernel(c_ref, w_ref, b_ref, o_ref):
    c = c_ref[...]
    s = c * jax.nn.sigmoid(c)
    o_ref[0] = jnp.dot(s, w_ref[0], precision=HIGHEST, preferred_element_type=F32) + b_ref[0]


def _ada(c_all, w_ada, b_ada):
    rows = c_all.shape[0]
    cols = w_ada.shape[-1]
    tile = 1536
    return pl.pallas_call(
        _ada_kernel,
        grid=(DEPTH, cols // tile),
        in_specs=[
            pl.BlockSpec((rows, D_MODEL), lambda l, j: (0, 0)),
            pl.BlockSpec((1, D_MODEL, tile), lambda l, j: (l, 0, j)),
            pl.BlockSpec((1, 1, tile), lambda l, j: (l, 0, j)),
        ],
        out_specs=pl.BlockSpec((1, rows, tile), lambda l, j: (l, 0, j)),
        out_shape=jax.ShapeDtypeStruct((DEPTH, rows, cols), F32),
        compiler_params=pltpu.CompilerParams(vmem_limit_bytes=VMEM_LIMIT_BYTES),
        name="ada",
    )(c_all, w_ada, b_ada.reshape(DEPTH, 1, cols))


def _mlstm_chunk_heads(heads, causal):
    qs, ks, vs, ig_cols, ig_rows, b_cols, b_rows, c_states, n_states, m_states = zip(*heads)
    n_h = len(qs)
    length = qs[0].shape[0]
    qb = [q.astype(BF16) for q in qs]
    kb = [k.astype(BF16) for k in ks]
    scores = [lax.dot_general(qb[i], kb[i], NT_DIMS, preferred_element_type=F32) for i in range(n_h)]
    q_c = [lax.dot_general(qb[i], c_states[i].astype(BF16), NT_DIMS, preferred_element_type=F32)
           for i in range(n_h)]
    a, w_inter, m_t = [], [], []
    for i in range(n_h):
        dlog = jnp.where(causal, b_cols[i] - b_rows[i] + ig_rows[i], -jnp.inf)
        inter = b_cols[i] + m_states[i]
        m_t.append(jnp.maximum(inter, jnp.max(dlog, axis=1, keepdims=True)))
        a.append(jnp.exp(dlog - m_t[i]) * scores[i])
        w_inter.append(jnp.exp(inter - m_t[i]))
    a_v = [jnp.dot(a[i].astype(BF16), vs[i].astype(BF16), preferred_element_type=F32) for i in range(n_h)]
    vw, dec, wc, m_new = [], [], [], []
    for i in range(n_h):
        b_end = b_cols[i][length - 1:length, :]
        dend = b_end - b_cols[i] + ig_cols[i]
        m_new.append(jnp.maximum(b_end + m_states[i], jnp.max(dend, axis=0, keepdims=True)))
        wc.append(jnp.exp(dend - m_new[i]))
        dec.append(jnp.exp(b_end + m_states[i] - m_new[i]))
        vw.append((vs[i] * wc[i]).astype(BF16))
    v_k = [lax.dot_general(vw[i], kb[i], TN_DIMS, preferred_element_type=F32) for i in range(n_h)]
    out = []
    for i in range(n_h):
        num = a_v[i] + w_inter[i] * q_c[i]
        den = (jnp.sum(a[i], axis=1, keepdims=True)
               + w_inter[i] * jnp.sum(qs[i] * n_states[i], axis=1, keepdims=True))
        h = num / jnp.maximum(jnp.abs(den), jnp.exp(-m_t[i]))
        c_new = dec[i] * c_states[i] + v_k[i]
        n_new = dec[i] * n_states[i] + jnp.sum(ks[i] * wc[i], axis=0, keepdims=True)
        out.append((h, c_new, n_new, m_new[i]))
    return out


def _mixer_kernel(x_ref, mod_ref, c_in_ref, n_in_ref, m_in_ref, hist_ref,
                  w_in_ref, wgate_ref, bgate_ref, mhg_ref, sgug_ref, sgub_ref, ws_ref, bs_ref,
                  wpool_ref, pscale_ref, wo_ref, ln1g_ref, ln1b_ref,
                  x1_ref, c_out_ref, n_out_ref, m_out_ref, tail_ref, vn_ref,
                  proj_sc, ycat_sc, c_sc, n_sc, m_sc, carry_sc, pbuf_sc,
                  *, nb, rows, chunk, n_valid, sgu_blk, pos0, single_chunk):
    ci = pl.program_id(1)
    n_chunks = pl.num_programs(1)
    m_rows = nb * rows
    if single_chunk:
        c_rd, n_rd, m_rd = c_in_ref, n_in_ref, m_in_ref
        c_wr, n_wr, m_wr = c_out_ref, n_out_ref, m_out_ref
    else:
        c_rd, n_rd, m_rd = c_sc, n_sc, m_sc
        c_wr, n_wr, m_wr = c_sc, n_sc, m_sc

    @pl.when(ci == 0)
    def _():
        if not single_chunk:
            c_sc[...] = c_in_ref[...]
            n_sc[...] = n_in_ref[...]
            m_sc[...] = m_in_ref[...]
        carry_sc[...] = hist_ref[...]

    x3 = x_ref[...]
    mod = mod_ref[...]
    sh1, sc1, g1 = mod[:, 0:1, :], mod[:, 1:2, :], mod[:, 2:3, :]
    h3 = x3 * (1.0 + sc1) + sh1
    h2d = h3.reshape(m_rows, D_MODEL)
    h_hi = h2d.astype(BF16)
    proj_sc[...] = jnp.dot(h_hi, w_in_ref[...], preferred_element_type=F32)
    h_lo = (h2d - h_hi.astype(F32)).astype(BF16)
    proj_sc[:, COL_GATE:COL_GATE + LANES] = (
        proj_sc[:, COL_GATE:COL_GATE + LANES] + proj_sc[:, COL_GATE_LO:COL_GATE_LO + LANES]
        + jnp.dot(h_lo, wgate_ref[...], preferred_element_type=F32))

    r_io = lax.broadcasted_iota(jnp.int32, (chunk, chunk), 0)
    c_io = lax.broadcasted_iota(jnp.int32, (chunk, chunk), 1)
    causal = c_io <= r_io
    tri = jnp.where(causal, 1.0, 0.0).astype(BF16)
    row_id = lax.broadcasted_iota(jnp.int32, (chunk, 1), 0)
    valid_col = row_id < n_valid
    bgate = bgate_ref[...]
    mhg = mhg_ref[...]

    seq_group = 2 if nb % 2 == 0 else 1

    def seq_body(sg, carry):
        for j in range(rows // chunk):
            heads, where = [], []
            for g in range(seq_group):
                s = sg * seq_group + g
                row0 = s * rows + j * chunk
                if not isinstance(row0, int):
                    row0 = pl.multiple_of(row0, SUBLANES)
                rsl = pl.ds(row0, chunk)
                gates = proj_sc[rsl, COL_GATE:COL_GATE + LANES] + bgate
                lf = jnp.where(valid_col, jax.nn.log_sigmoid(gates), 0.0)
                if chunk >= LANES:
                    lf_hi, lf_lo = _split_bf16(lf)
                    bcum = (jnp.dot(tri, lf_hi, preferred_element_type=F32)
                            + jnp.dot(tri, lf_lo, preferred_element_type=F32))
                else:
                    bcum = jnp.dot(tri.astype(F32), lf, precision=HIGHEST, preferred_element_type=F32)
                ig_all = jnp.where(valid_col, gates, -jnp.inf)
                bcum_t = bcum.T
                ig_t = ig_all.T
                for h in range(MLSTM_HEADS):
                    q = proj_sc[rsl, COL_Q + h * MLSTM_DH:COL_Q + (h + 1) * MLSTM_DH] * (MLSTM_DH ** -0.5)
                    k = proj_sc[rsl, COL_K + h * MLSTM_DH:COL_K + (h + 1) * MLSTM_DH]
                    v = proj_sc[rsl, COL_V + h * MLSTM_DH:COL_V + (h + 1) * MLSTM_DH]
                    fcol = MLSTM_HEADS + h
                    heads.append((q, k, v,
                                  ig_all[:, h:h + 1], ig_t[h:h + 1, 0:chunk],
                                  bcum[:, fcol:fcol + 1], bcum_t[fcol:fcol + 1, 0:chunk],
                                  c_rd[s, h], n_rd[s, pl.ds(h, 1), :], m_rd[s, pl.ds(h, 1), 0:1]))
                    where.append((s, h, rsl))
            for (s, h, rsl), (hh, c_new, n_new, m_new) in zip(where, _mlstm_chunk_heads(heads, causal)):
                hs = slice(h * MLSTM_DH, (h + 1) * MLSTM_DH)
                o = proj_sc[rsl, COL_O + h * MLSTM_DH:COL_O + (h + 1) * MLSTM_DH]
                c_wr[s, h] = c_new
                n_wr[s, pl.ds(h, 1), :] = n_new
                m_wr[s, pl.ds(h, 1), :] = jnp.broadcast_to(m_new, (1, LANES))
                hn = _layer_norm(hh) * mhg[:, hs]
                ycat_sc[rsl, hs] = jax.nn.sigmoid(o) * hn
        return carry

    if nb == seq_group:
        seq_body(0, 0)
    else:
        lax.fori_loop(0, nb // seq_group, seq_body, 0)

    u_s = proj_sc[:, COL_U:COL_U + SGU_WIDTH]
    v_s = proj_sc[:, COL_VS:COL_VS + SGU_WIDTH]
    gr = lax.broadcasted_iota(jnp.int32, (SGU_WIDTH, SGU_WIDTH), 0) // SGU_DH
    gc = lax.broadcasted_iota(jnp.int32, (SGU_WIDTH, SGU_WIDTH), 1) // SGU_DH
    avg = jnp.where(gr == gc, 1.0 / SGU_DH, 0.0).astype(BF16)

    def group_mean(t):
        hi, lo = _split_bf16(t)
        return jnp.dot(hi, avg, preferred_element_type=F32) + jnp.dot(lo, avg, preferred_element_type=F32)

    mu = group_mean(v_s)
    dv = v_s - mu
    var = group_mean(dv * dv)
    vn = dv * lax.rsqrt(var + LN_EPS) * sgug_ref[...] + sgub_ref[...]
    if vn_ref is not None:
        vn_ref[...] = vn.reshape(nb, rows, SGU_WIDTH)
    sr = lax.broadcasted_iota(jnp.int32, (SGU_CHUNK, SGU_CHUNK), 0)
    scol = lax.broadcasted_iota(jnp.int32, (SGU_CHUNK, SGU_CHUNK), 1)
    smask = (scol <= sr) & ((sr // sgu_blk) == (scol // sgu_blk))
    lane_grp = lax.broadcasted_iota(jnp.int32, (SGU_CHUNK, SGU_WIDTH), 1) // SGU_DH
    vnb = vn.astype(BF16)
    for r in range(m_rows // SGU_CHUNK):
        rs = slice(r * SGU_CHUNK, (r + 1) * SGU_CHUNK)
        mix = jnp.zeros((SGU_CHUNK, SGU_WIDTH), F32)
        for g in range(SGU_HEADS):
            wg = jnp.where(smask, ws_ref[g], 0.0).astype(BF16)
            mg = jnp.dot(wg, vnb[rs], preferred_element_type=F32)
            mix = jnp.where(lane_grp == g, mg, mix)
        ycat_sc[rs, MLSTM_WIDTH:MLSTM_WIDTH + SGU_WIDTH] = u_s[rs] * (mix + bs_ref[...])

    pbuf_sc[:, 0:POOL_HIST, :] = carry_sc[...]
    pbuf_sc[:, POOL_HIST:POOL_HIST + rows, :] = proj_sc[:, COL_P:COL_P + POOL_WIDTH].reshape(nb, rows, POOL_WIDTH)
    lane_w = lax.broadcasted_iota(jnp.int32, (1, 1, POOL_WIDTH), 2) // (POOL_WIDTH // len(POOL_WINDOWS))
    x0 = pbuf_sc[:, POOL_HIST:POOL_HIST + rows, :]
    acc = x0
    wsum = jnp.zeros_like(x0)
    for kk in range(1, max(POOL_WINDOWS)):
        acc = acc + pbuf_sc[:, POOL_HIST - kk:POOL_HIST - kk + rows, :]
        if (kk + 1) in POOL_WINDOWS:
            wsum = jnp.where(lane_w == POOL_WINDOWS.index(kk + 1), acc, wsum)
    t_io = lax.broadcasted_iota(jnp.int32, (1, rows, POOL_WIDTH), 1)
    win = jnp.left_shift(2, lane_w)
    cnt = jnp.minimum(pos0 + ci * rows + t_io + 1, win).astype(F32)
    pooled = wsum / cnt - x0
    y_c = jnp.dot(pooled.reshape(m_rows, POOL_WIDTH).astype(BF16), wpool_ref[...],
                  preferred_element_type=F32) * pscale_ref[...]
    ycat_sc[:, MLSTM_WIDTH + SGU_WIDTH:D_MODEL] = y_c
    carry_sc[...] = pbuf_sc[:, rows:rows + POOL_HIST, :]

    y = jnp.dot(ycat_sc[...].astype(BF16), wo_ref[...], preferred_element_type=F32)
    z = ALPHA * x3 + g1 * y.reshape(nb, rows, D_MODEL)
    x1_ref[...] = _layer_norm(z) * ln1g_ref[...] + ln1b_ref[...]

    @pl.when(ci == n_chunks - 1)
    def _():
        if not single_chunk:
            c_out_ref[...] = c_sc[...]
            n_out_ref[...] = n_sc[...]
            m_out_ref[...] = m_sc[...]
        tail_ref[...] = pbuf_sc[:, n_valid + rows - chunk:n_valid + rows - chunk + POOL_HIST, :]


def _mixer(x, mod, c0, c0_layer, n0, m0, hist, lw, *, nb, rows, chunk, n_valid, sgu_blk, pos0, want_vn):
    bsz, seq, _ = x.shape
    grid = (bsz // nb, seq // rows)
    m_rows = nb * rows
    kern = functools.partial(_mixer_kernel, nb=nb, rows=rows, chunk=chunk, n_valid=n_valid,
                             sgu_blk=sgu_blk, pos0=pos0, single_chunk=(seq == rows == chunk))
    if not want_vn:
        def kern(*refs, _k=kern):
            return _k(*refs[:24], None, *refs[24:])

    def full(shape):
        return pl.BlockSpec(shape, lambda b, c: (0,) * len(shape))

    per_b3 = lambda s1, s2: pl.BlockSpec((nb, s1, s2), lambda b, c: (b, 0, 0))
    in_specs = [
        pl.BlockSpec((nb, rows, D_MODEL), lambda b, c: (b, c, 0)),
        per_b3(SUBLANES, D_MODEL),
        pl.BlockSpec((None, nb, MLSTM_HEADS, MLSTM_DH, MLSTM_DH), lambda b, c: (c0_layer, b, 0, 0, 0)),
        per_b3(MLSTM_HEADS, MLSTM_DH),
        per_b3(MLSTM_HEADS, LANES),
        per_b3(POOL_HIST, POOL_WIDTH),
        full((D_MODEL, IN_COLS_PADDED)),
        full((D_MODEL, LANES)),
        full((1, LANES)),
        full((1, MLSTM_WIDTH)),
        full((1, SGU_WIDTH)),
        full((1, SGU_WIDTH)),
        full((SGU_HEADS, SGU_CHUNK, SGU_CHUNK)),
        full((SGU_CHUNK, SGU_WIDTH)),
        full((POOL_WIDTH, POOL_WIDTH)),
        full((1, POOL_WIDTH)),
        full((D_MODEL, D_MODEL)),
        full((1, D_MODEL)),
        full((1, D_MODEL)),
    ]
    out_specs = [
        pl.BlockSpec((nb, rows, D_MODEL), lambda b, c: (b, c, 0)),
        pl.BlockSpec((nb, MLSTM_HEADS, MLSTM_DH, MLSTM_DH), lambda b, c: (b, 0, 0, 0)),
        per_b3(MLSTM_HEADS, MLSTM_DH),
        per_b3(MLSTM_HEADS, LANES),
        per_b3(POOL_HIST, POOL_WIDTH),
    ]
    out_shape = [
        jax.ShapeDtypeStruct((bsz, seq, D_MODEL), F32),
        jax.ShapeDtypeStruct((bsz, MLSTM_HEADS, MLSTM_DH, MLSTM_DH), F32),
        jax.ShapeDtypeStruct((bsz, MLSTM_HEADS, MLSTM_DH), F32),
        jax.ShapeDtypeStruct((bsz, MLSTM_HEADS, LANES), F32),
        jax.ShapeDtypeStruct((bsz, POOL_HIST, POOL_WIDTH), F32),
    ]
    if want_vn:
        out_specs.append(pl.BlockSpec((nb, rows, SGU_WIDTH), lambda b, c: (b, c, 0)))
        out_shape.append(jax.ShapeDtypeStruct((bsz, seq, SGU_WIDTH), F32))
    scratch = [
        pltpu.VMEM((m_rows, IN_COLS_PADDED), F32),
        pltpu.VMEM((m_rows, D_MODEL), F32),
        pltpu.VMEM((nb, MLSTM_HEADS, MLSTM_DH, MLSTM_DH), F32),
        pltpu.VMEM((nb, MLSTM_HEADS, MLSTM_DH), F32),
        pltpu.VMEM((nb, MLSTM_HEADS, LANES), F32),
        pltpu.VMEM((nb, POOL_HIST, POOL_WIDTH), F32),
        pltpu.VMEM((nb, POOL_HIST + rows, POOL_WIDTH), F32),
    ]
    return pl.pallas_call(
        kern,
        grid=grid,
        in_specs=in_specs,
        out_specs=out_specs,
        out_shape=out_shape,
        scratch_shapes=scratch,
        compiler_params=pltpu.CompilerParams(
            dimension_semantics=("arbitrary", "arbitrary"), vmem_limit_bytes=VMEM_LIMIT_BYTES),
        name="mixer_sample" if want_vn else "mixer_prompt",
    )(x, mod, c0, n0, m0, hist, lw["w_in"], lw["w_gate"], lw["b_gate"], lw["mh_g"], lw["sgu_g"], lw["sgu_b"],
      lw["w_s_sample"] if want_vn else lw["w_s"], lw["b_s_sample"] if want_vn else lw["b_s"],
      lw["w_pool"], lw["pool_scale"], lw["w_o"], lw["ln1_g"], lw["ln1_b"])


def _sort16_pairs():
    pairs = []

    def merge(lo, hi, r):
        step = r * 2
        if step < hi - lo:
            merge(lo, hi, step)
            merge(lo + r, hi, step)
            for i in range(lo + r, hi - r, step):
                pairs.append((i, i + r))
        else:
            pairs.append((lo, lo + r))

    def sort(lo, hi):
        if hi - lo >= 1:
            mid = lo + (hi - lo) // 2
            sort(lo, mid)
            sort(mid + 1, hi)
            merge(lo, hi, 1)

    sort(0, PEER_TOPK - 1)
    return tuple(pairs)


SORT16_PAIRS = _sort16_pairs()


def _sort16_desc(vals):
    vals = list(vals)
    for i, j in SORT16_PAIRS:
        hi = jnp.maximum(vals[i], vals[j])
        lo = jnp.minimum(vals[i], vals[j])
        vals[i], vals[j] = hi, lo
    return vals


def _merge_top16(xs, ys):
    vals = [jnp.maximum(xs[i], ys[PEER_TOPK - 1 - i]) for i in range(PEER_TOPK)]
    d = PEER_TOPK // 2
    while d >= 1:
        for i in range(PEER_TOPK):
            if (i & d) == 0:
                hi = jnp.maximum(vals[i], vals[i + d])
                lo = jnp.minimum(vals[i], vals[i + d])
                vals[i], vals[i + d] = hi, lo
        d //= 2
    return vals


def _zero_of(words):
    bits = lax.bitcast_convert_type(words, jnp.uint32)
    bits = lax.shift_right_logical(lax.shift_right_logical(bits, jnp.uint32(16)), jnp.uint32(16))
    return lax.bitcast_convert_type(bits, F32)


def _top16_desc(tiles, zero_dep=None):
    tiles = list(tiles)
    neg = jnp.full_like(tiles[0], -jnp.inf)
    while len(tiles) % PEER_TOPK:
        tiles.append(neg)
    best = None
    for g in range(len(tiles) // PEER_TOPK):
        grp = tiles[g * PEER_TOPK:(g + 1) * PEER_TOPK]
        if zero_dep is not None:
            grp = [grp[0] + zero_dep] + grp[1:]
        grp = _sort16_desc(grp)
        best = grp if best is None else _merge_top16(best, grp)
        zero_dep = _zero_of(best[PEER_TOPK - 1])
    return best


S_PITCH = PEER_NKEYS + SUBLANES


def _peer_select_kernel(x_ref, sh_ref, sc_ref, wq_ref, keys_ref,
                        thr_ref, c1_ref, s2_ref, e2_ref,
                        q_sc, sa_sc, sb_sc, *, tn):
    s_sc = (sa_sc, sb_sc)
    x = x_ref[...]
    sh = sh_ref[...].reshape(-1, D_MODEL)
    sc = sc_ref[...].reshape(-1, D_MODEL)
    h2 = (x * (1.0 + sc) + sh).astype(BF16)
    q_sc[...] = lax.dot_general(wq_ref[...], h2, NT_DIMS, preferred_element_type=F32).astype(BF16)
    for h in range(PEER_HEADS):
        for p in range(2):
            r0 = (h * 2 + p) * PEER_DK
            s_hp = jnp.dot(keys_ref[h * 2 + p], q_sc[r0:r0 + PEER_DK, :], preferred_element_type=F32)
            for lg in range(tn // LANES):
                s_sc[p][lg, h * S_PITCH:h * S_PITCH + PEER_NKEYS, :] = s_hp[:, lg * LANES:(lg + 1) * LANES]
    dep = None
    for lg in range(tn // LANES):
        lanes = pl.ds(lg * LANES, LANES)
        tops = []
        for p in range(2):
            s_lg = s_sc[p].at[lg]
            tiles = [s_lg[pl.ds(i, PEER_HEADS, stride=S_PITCH), :] for i in range(PEER_NKEYS)]
            tops.append(_top16_desc(tiles, dep))
            dep = _zero_of(tops[-1][PEER_TOPK - 1])
        a, b = tops
        cands = []
        for r1 in range(PEER_TOPK):
            for r2 in range(PEER_TOPK // (r1 + 1)):
                cands.append(b[r2] + a[r1])
        c = _top16_desc(cands, dep)
        tau = c[PEER_TOPK - 1]
        z = jnp.zeros_like(tau)
        for kk in range(PEER_TOPK):
            z = z + jnp.exp(c[kk] - c[0])
        zinv = 0.5 / z
        dep = _zero_of(zinv)
        for h in range(PEER_HEADS):
            rows_in = slice(h * S_PITCH, h * S_PITCH + PEER_NKEYS)
            rows_out = slice(h * PEER_NKEYS, (h + 1) * PEER_NKEYS)
            s1 = sa_sc[lg, rows_in, :] + dep[0:1, :]
            s2 = sb_sc[lg, rows_in, :]
            tau_h = tau[h:h + 1, :]
            thr = jnp.full_like(s1, jnp.inf)
            for r in range(PEER_TOPK):
                cand = b[r][h:h + 1, :]
                thr = jnp.where((s1 + cand) >= tau_h, cand, thr)
            thr_ref[lg, rows_out, :] = thr
            dep = _zero_of(thr[0:SUBLANES, :])
            s2_ref[lg, rows_out, :] = s2
            c1_ref[lg, rows_out, :] = jnp.exp(s1 - a[0][h:h + 1, :]) * zinv[h:h + 1, :]
            e2_ref[lg, rows_out, :] = jnp.exp(s2 - b[0][h:h + 1, :])


def _mod_spec(arr, tn, tiles_per_row):
    if arr.ndim == 3:
        return pl.BlockSpec((1, 1, D_MODEL), lambda i, *_: (i // tiles_per_row, 0, 0))
    return pl.BlockSpec((tn, D_MODEL), lambda i, *_: (i, 0))


def _peer_select(x1, sh2, sc2, lw, *, tn, tiles_per_row):
    n_tok = x1.shape[0]
    assert n_tok % tn == 0
    sel_rows = PEER_HEADS * PEER_NKEYS
    n_lg = tn // LANES
    big = lambda: pl.BlockSpec((n_lg, sel_rows, LANES), lambda i: (i, 0, 0))
    return pl.pallas_call(
        functools.partial(_peer_select_kernel, tn=tn),
        grid=(n_tok // tn,),
        in_specs=[
            pl.BlockSpec((tn, D_MODEL), lambda i: (i, 0)),
            _mod_spec(sh2, tn, tiles_per_row),
            _mod_spec(sc2, tn, tiles_per_row),
            pl.BlockSpec((2 * PEER_HEADS * PEER_DK, D_MODEL), lambda i: (0, 0)),
            pl.BlockSpec((2 * PEER_HEADS, PEER_NKEYS, PEER_DK), lambda i: (0, 0, 0)),
        ],
        out_specs=[big(), big(), big(), big()],
        out_shape=[jax.ShapeDtypeStruct((n_tok // LANES, sel_rows, LANES), F32)] * 4,
        scratch_shapes=[
            pltpu.VMEM((2 * PEER_HEADS * PEER_DK, tn), BF16),
            pltpu.VMEM((tn // LANES, PEER_HEADS * S_PITCH, LANES), F32),
            pltpu.VMEM((tn // LANES, PEER_HEADS * S_PITCH, LANES), F32),
        ],
        compiler_params=pltpu.CompilerParams(
            dimension_semantics=("arbitrary",), vmem_limit_bytes=VMEM_LIMIT_BYTES),
        name="peer_select",
    )(x1, sh2, sc2, lw["w_pq_t"], lw["peer_keys"])


GATE_SUB = 32
PEER_STAGE = 2 * PEER_NKEYS
TILES_PER_STEP = 2


def _peer_dense_kernel(x_ref, sh_ref, sc_ref, g_ref, thr_ref, c1_ref, s2_ref, e2_ref,
                       ua_ref, ub_ref, vta_ref, vtb_prev_ref, lng_ref, lnb_ref, o_ref,
                       xb_sc, acc_sc, at_a, at_b, coef_a, coef_b, thrb_sc, c1b_sc, *, tn, te):
    j = pl.program_id(1)
    n_j = pl.num_programs(1) - 1
    stage = at_a.shape[0]
    n_st = te // stage
    keys_per_tile = te // PEER_NKEYS
    keys_per_step = TILES_PER_STEP * keys_per_tile
    out_chunk = D_MODEL // n_st
    assert n_st % 2 == 0

    @pl.when(j == 0)
    def _():
        sh = sh_ref[...].reshape(-1, D_MODEL)
        sc = sc_ref[...].reshape(-1, D_MODEL)
        xb_sc[...] = (x_ref[...] * (1.0 + sc) + sh).T.astype(BF16)
        acc_sc[...] = jnp.zeros_like(acc_sc)
        coef_b[...] = jnp.zeros_like(coef_b)

    def fill_key_tables():
        for lg in range(tn // LANES):
            for h in range(PEER_HEADS):
                krows = pl.ds(pl.multiple_of(h * PEER_NKEYS + j * keys_per_step, keys_per_step), keys_per_step)
                thr_t = thr_ref[lg, krows, :]
                c1_t = c1_ref[lg, krows, :]
                for k in range(keys_per_step):
                    thrb_sc[k, lg * PEER_HEADS + h] = jnp.broadcast_to(thr_t[k:k + 1, :], (SUBLANES, LANES))
                    c1b_sc[k, lg * PEER_HEADS + h] = jnp.broadcast_to(c1_t[k:k + 1, :], (SUBLANES, LANES))

    def scores(u_ref, st):
        rows = pl.ds(st * stage, stage)
        return jnp.dot(u_ref[rows, :], xb_sc[...], preferred_element_type=F32)

    def values_chunk(vt_ref, coef_ref, st):
        rows = pl.ds(st * out_chunk, out_chunk)
        acc_sc[rows, :] += jnp.dot(vt_ref[rows, :], coef_ref[...], preferred_element_type=F32)

    def gate_stage(key_base, st, at_ref, coef_ref):
        n_keys = stage // PEER_NKEYS
        gshape = (GATE_SUB // SUBLANES, SUBLANES, LANES)
        zero_dep = jnp.zeros((SUBLANES, LANES), F32)
        for lg in range(tn // LANES):
            lanes = slice(lg * LANES, (lg + 1) * LANES)
            for sub in range(PEER_NKEYS // GATE_SUB):
                gates = [jnp.broadcast_to(zero_dep[None], gshape) for _ in range(n_keys)]
                for h in range(PEER_HEADS):
                    hrows = slice(h * PEER_NKEYS + sub * GATE_SUB, h * PEER_NKEYS + (sub + 1) * GATE_SUB)
                    s2 = s2_ref[lg, hrows, :].reshape(gshape)
                    e2 = e2_ref[lg, hrows, :].reshape(gshape)
                    for qq in range(n_keys):
                        thrb = thrb_sc[key_base + st * n_keys + qq, lg * PEER_HEADS + h]
                        c1b = c1b_sc[key_base + st * n_keys + qq, lg * PEER_HEADS + h]
                        gates[qq] = gates[qq] + jnp.where(s2 >= thrb, e2 * c1b, 0.0)
                for qq in range(n_keys):
                    r0 = qq * PEER_NKEYS + sub * GATE_SUB
                    a = at_ref[r0:r0 + GATE_SUB, lanes]
                    act = a * (1.0 + lax.erf(a * (2.0 ** -0.5)))
                    out_rows = pl.ds(st * stage + r0, GATE_SUB)
                    coef = gates[qq].reshape(GATE_SUB, LANES) * act
                    coef_ref[out_rows, lanes] = coef.astype(BF16)
                bits = lax.bitcast_convert_type(coef[0:SUBLANES, :], jnp.uint32)
                bits = lax.shift_right_logical(lax.shift_right_logical(bits, jnp.uint32(16)), jnp.uint32(16))
                zero_dep = lax.bitcast_convert_type(bits, F32)

    def tile_phase(u_ref, u_next_ref, key_base, coef_cur, vt_prev_ref, coef_prev):
        for st in range(n_st):
            at_cur, at_nxt = (at_a, at_b) if st % 2 == 0 else (at_b, at_a)
            gate_stage(key_base, st, at_cur, coef_cur)
            if st + 1 < n_st:
                at_nxt[...] = scores(u_ref, st + 1)
            elif u_next_ref is not None:
                at_nxt[...] = scores(u_next_ref, 0)
            values_chunk(vt_prev_ref, coef_prev, st)

    @pl.when(j < n_j)
    def _():
        at_a[...] = scores(ua_ref, 0)
        fill_key_tables()
        tile_phase(ua_ref, ub_ref, 0, coef_a, vtb_prev_ref, coef_b)
        tile_phase(ub_ref, None, keys_per_tile, coef_b, vta_ref, coef_a)

    @pl.when(j == n_j)
    def _():
        for st in range(n_st):
            values_chunk(vtb_prev_ref, coef_b, st)
        g2 = g_ref[...].reshape(-1, D_MODEL)
        z = ALPHA * x_ref[...] + g2 * acc_sc[...].T
        o_ref[...] = _layer_norm(z) * lng_ref[...] + lnb_ref[...]


def _peer_dense(x1, sh2, sc2, g2, sel, lw, *, tn, te, tiles_per_row):
    n_tok = x1.shape[0]
    assert n_tok % tn == 0
    sel_rows = PEER_HEADS * PEER_NKEYS
    n_lg = tn // LANES
    thr, c1, s2, e2 = sel
    big = lambda: pl.BlockSpec((n_lg, sel_rows, LANES), lambda i, j: (i, 0, 0))
    n_j = PEER_EXPERTS // (TILES_PER_STEP * te)
    layer = lw["layer"]
    tile_a = lambda j: TILES_PER_STEP * jnp.minimum(j, n_j - 1)
    table = pltpu.VMEM((TILES_PER_STEP * te // PEER_NKEYS, n_lg * PEER_HEADS, SUBLANES, LANES), F32)
    return pl.pallas_call(
        functools.partial(_peer_dense_kernel, tn=tn, te=te),
        grid=(n_tok // tn, n_j + 1),
        in_specs=[
            pl.BlockSpec((tn, D_MODEL), lambda i, j: (i, 0)),
            _mod_spec(sh2, tn, tiles_per_row),
            _mod_spec(sc2, tn, tiles_per_row),
            _mod_spec(g2, tn, tiles_per_row),
            big(), big(), big(), big(),
            pl.BlockSpec((None, te, D_MODEL), lambda i, j: (layer, tile_a(j), 0)),
            pl.BlockSpec((None, te, D_MODEL), lambda i, j: (layer, tile_a(j) + 1, 0)),
            pl.BlockSpec((None, D_MODEL, te), lambda i, j: (layer, 0, tile_a(j))),
            pl.BlockSpec((None, D_MODEL, te), lambda i, j: (layer, 0, jnp.maximum(TILES_PER_STEP * j - 1, 0))),
            pl.BlockSpec((1, D_MODEL), lambda i, j: (0, 0)),
            pl.BlockSpec((1, D_MODEL), lambda i, j: (0, 0)),
        ],
        out_specs=pl.BlockSpec((tn, D_MODEL), lambda i, j: (i, 0)),
        out_shape=jax.ShapeDtypeStruct((n_tok, D_MODEL), F32),
        scratch_shapes=[
            pltpu.VMEM((D_MODEL, tn), BF16),
            pltpu.VMEM((D_MODEL, tn), F32),
            pltpu.VMEM((PEER_STAGE, tn), F32),
            pltpu.VMEM((PEER_STAGE, tn), F32),
            pltpu.VMEM((te, tn), BF16),
            pltpu.VMEM((te, tn), BF16),
            table, table,
        ],
        compiler_params=pltpu.CompilerParams(
            dimension_semantics=("arbitrary", "arbitrary"), vmem_limit_bytes=VMEM_LIMIT_BYTES),
        name="peer_dense",
    )(x1, sh2, sc2, g2, thr, c1, s2, e2, lw["peer_u"], lw["peer_u"], lw["peer_v_t"], lw["peer_v_t"],
      lw["ln2_g"], lw["ln2_b"])


def _prep_layer(l, w_in, b_gate, mh_g, sgu_g, sgu_b, w_s, b_s, w_pool, pool_scale, w_o, ln1_g, ln1_b,
                w_pq, peer_keys, ln2_g, ln2_b, sample_rows):
    n_gate = 2 * MLSTM_HEADS
    g0 = 4 * MLSTM_WIDTH
    w = w_in[l]
    w_gate = jnp.pad(w[:, g0:g0 + n_gate], ((0, 0), (0, LANES - n_gate)))
    w_gate_hi = w_gate.astype(BF16)
    w_gate_lo = (w_gate - w_gate_hi.astype(F32)).astype(BF16)
    w_in_p = jnp.concatenate(
        [w[:, :g0].astype(BF16), w_gate_hi, w[:, g0 + n_gate:].astype(BF16), w_gate_lo], axis=1)
    bs_full = jnp.repeat(jnp.swapaxes(b_s[l], 0, 1), SGU_DH, axis=1)
    reps = SGU_CHUNK // sample_rows
    eye = jnp.eye(len(POOL_WINDOWS), dtype=F32)
    dg = POOL_WIDTH // len(POOL_WINDOWS)
    w_pool_bd = (eye[:, None, :, None] * w_pool[l][:, :, None, :]).reshape(POOL_WIDTH, POOL_WIDTH)
    row = lambda a: a.reshape(1, -1)
    return {
        "w_in": w_in_p,
        "w_gate": w_gate_hi,
        "b_gate": jnp.pad(b_gate[l], (0, LANES - n_gate)).reshape(1, LANES),
        "mh_g": row(mh_g[l]), "sgu_g": row(sgu_g[l]), "sgu_b": row(sgu_b[l]),
        "w_s": w_s[l],
        "b_s": bs_full,
        "w_s_sample": jnp.tile(w_s[l][:, :sample_rows, :sample_rows], (1, reps, reps)),
        "b_s_sample": jnp.tile(bs_full[:sample_rows], (reps, 1)),
        "w_pool": w_pool_bd.astype(BF16),
        "pool_scale": row(pool_scale[l]),
        "w_o": w_o[l].astype(BF16),
        "ln1_g": row(ln1_g[l]), "ln1_b": row(ln1_b[l]),
        "w_pq_t": w_pq[l].T.astype(BF16),
        "peer_keys": peer_keys[l].reshape(2 * PEER_HEADS, PEER_NKEYS, PEER_DK).astype(BF16),
        "ln2_g": row(ln2_g[l]), "ln2_b": row(ln2_b[l]),
    }


def kernel(x_prompt, x_sample, state_mlstm_C, state_mlstm_n, state_mlstm_m, state_pool, c_prompt, c_sample,
           w_ada, b_ada, w_in, b_gate, mh_g, sgu_g, sgu_b, w_s, b_s, w_pool, pool_scale, w_o, ln1_g, ln1_b,
           w_pq, peer_keys, peer_u, peer_v, ln2_g, ln2_b):
    bp, seq, _ = x_prompt.shape
    bs, dec_seq, _ = x_sample.shape
    sample_rows = SUBLANES
    prompt_rows = min(seq, MIXER_ROWS)
    peer_tn = PEER_TOKEN_TILE
    peer_te = PEER_EXPERT_TILE
    sel_tn = SELECT_TOKEN_TILE

    ada = _ada(jnp.concatenate([c_prompt, c_sample], axis=0), w_ada, b_ada)

    xp = x_prompt
    xs = jnp.pad(x_sample, ((0, 0), (0, sample_rows - dec_seq), (0, 0)))
    zeros_c = jnp.zeros((1, bp, MLSTM_HEADS, MLSTM_DH, MLSTM_DH), F32)
    zeros_n = jnp.zeros((bp, MLSTM_HEADS, MLSTM_DH), F32)
    zeros_m = jnp.zeros((bp, MLSTM_HEADS, LANES), F32)
    zeros_hist = jnp.zeros((bp, POOL_HIST, POOL_WIDTH), F32)
    peer_u_b = peer_u.astype(BF16)
    peer_v_t = jnp.swapaxes(peer_v, 1, 2).astype(BF16)
    outs = [[] for _ in range(9)]
    for l in range(DEPTH):
        lw = _prep_layer(l, w_in, b_gate, mh_g, sgu_g, sgu_b, w_s, b_s, w_pool, pool_scale, w_o, ln1_g, ln1_b,
                         w_pq, peer_keys, ln2_g, ln2_b, sample_rows)
        lw.update(peer_u=peer_u_b, peer_v_t=peer_v_t, layer=l)
        mods = jnp.pad(ada[l].reshape(bp + bs, N_ADA, D_MODEL), ((0, 0), (0, SUBLANES - N_ADA), (0, 0)))
        mod_p, mod_s = mods[:bp], mods[bp:]

        x1p, cp, np_, mp, tailp = _mixer(
            xp, mod_p, zeros_c, 0, zeros_n, zeros_m, zeros_hist, lw,
            nb=1, rows=prompt_rows, chunk=SGU_CHUNK, n_valid=SGU_CHUNK, sgu_blk=SGU_CHUNK, pos0=0,
            want_vn=False)
        m_in = jnp.broadcast_to(state_mlstm_m[l][:, :, None], (bs, MLSTM_HEADS, LANES))
        hist_s = jnp.pad(state_pool[l], ((0, 0), (1, 0), (0, 0)))
        x1s, cs, ns, ms, tails, vns = _mixer(
            xs, mod_s, state_mlstm_C, l, state_mlstm_n[l], m_in, hist_s, lw,
            nb=16, rows=sample_rows, chunk=sample_rows, n_valid=dec_seq, sgu_blk=sample_rows, pos0=PAST_LEN,
            want_vn=True)

        x1p2 = x1p.reshape(bp * seq, D_MODEL)
        shp, scp, gp = mod_p[:, 3:4], mod_p[:, 4:5], mod_p[:, 5:6]
        sel = _peer_select(x1p2, shp, scp, lw, tn=sel_tn, tiles_per_row=seq // sel_tn)
        xp = _peer_dense(x1p2, shp, scp, gp, sel, lw, tn=peer_tn, te=peer_te,
                         tiles_per_row=seq // peer_tn).reshape(bp, seq, D_MODEL)
        x1s2 = x1s[:, :dec_seq].reshape(bs * dec_seq, D_MODEL)
        shs, scs, gs = (jnp.repeat(mod_s[:, i], dec_seq, axis=0) for i in (3, 4, 5))
        n_s = bs * dec_seq
        sel = _peer_select(x1s2, shs, scs, lw, tn=min(sel_tn, n_s), tiles_per_row=1)
        xs_new = _peer_dense(x1s2, shs, scs, gs, sel, lw, tn=min(peer_tn, n_s), te=peer_te, tiles_per_row=1)
        xs_new = xs_new.reshape(bs, dec_seq, D_MODEL)
        xs = jnp.pad(xs_new, ((0, 0), (0, sample_rows - dec_seq), (0, 0)))

        for lst, val in zip(outs, (cp, np_, mp[:, :, 0], tailp[:, 1:], cs, ns, ms[:, :, 0], tails[:, 1:],
                                   vns[:, :dec_seq])):
            lst.append(val)
    return (xp, xs_new) + tuple(jnp.stack(o) for o in outs)
```
